```python
import jax
import jax.numpy as jnp
from jax import lax
import numpy as np

D_MODEL = 2048
BATCH = 2
SEQ = 16384
DEPTH = 2

GRID_W = 64
CTX_LEN = 256
N_EVEN = (DEPTH + 1) // 2
N_ODD = DEPTH // 2
NORM_EPS = 1e-6
ROPE_BASE = 10000.0
MASK_VALUE = -1e30

FOURIER_GROUPS = 4
FOURIER_GROUP_DIM = 256
FOURIER_WIDTH = FOURIER_GROUPS * FOURIER_GROUP_DIM
RET_HEADS = 8
RET_QK_DIM = 128
RET_V_DIM = 128
RET_CHUNK = 128
RET_QK_WIDTH = RET_HEADS * RET_QK_DIM
RET_V_WIDTH = RET_HEADS * RET_V_DIM
EVEN_IN_SIZES = (FOURIER_WIDTH, RET_QK_WIDTH, RET_QK_WIDTH, RET_V_WIDTH, RET_V_WIDTH, RET_V_WIDTH)
EVEN_IN_WIDTH = sum(EVEN_IN_SIZES)
EVEN_OUT_WIDTH = FOURIER_WIDTH + RET_V_WIDTH

SWA_Q_HEADS = 16
SWA_KV_HEADS = 2
SWA_GROUP = SWA_Q_HEADS // SWA_KV_HEADS
SWA_HEAD_DIM = 64
SWA_WINDOW = 128
SWA_BLOCK = 128
MLA_HEADS = 8
MLA_Q_RANK = 512
MLA_KV_RANK = 256
MLA_NOPE = 128
MLA_ROPE = 64
MLA_V = 128
MLA_QK_DIM = MLA_NOPE + MLA_ROPE
MLA_BLOCK = 128
ODD_IN_SIZES = (SWA_Q_HEADS * SWA_HEAD_DIM, SWA_KV_HEADS * SWA_HEAD_DIM, SWA_KV_HEADS * SWA_HEAD_DIM, MLA_Q_RANK, MLA_KV_RANK, MLA_ROPE)
ODD_IN_WIDTH = sum(ODD_IN_SIZES)
ODD_OUT_WIDTH = SWA_Q_HEADS * SWA_HEAD_DIM + MLA_HEADS * MLA_V

PEER_HEADS = 8
PEER_KEYS = 128
PEER_EXPERTS = PEER_KEYS * PEER_KEYS
PEER_TOPK = 16
PEER_QDIM = 256
PEER_TOKEN_BLOCK = 128

kernel_name = "hybrid_fourier_retention_swa_mla_peer_dit"


def _split(t, sizes):
    return jnp.split(t, [int(v) for v in np.cumsum(sizes)[:-1]], axis=-1)


def rms_norm(x, g):
    xf = x.astype(jnp.float32)
    y = xf * lax.rsqrt(jnp.mean(xf * xf, axis=-1, keepdims=True) + NORM_EPS)
    return (y * g.astype(jnp.float32)).astype(x.dtype)


def adaln(cond, w_mod, b_mod):
    m = jax.nn.silu(cond) @ w_mod + b_mod
    return jnp.split(m[..., None, :], 6, axis=-1)


def axial_rope_tables(rows, d_rot):
    d_axis = d_rot // 2
    inv = ROPE_BASE ** (-jnp.arange(0, d_axis, 2, dtype=jnp.float32) / d_axis)
    t = jnp.arange(rows * GRID_W)
    row = (t // GRID_W).astype(jnp.float32)
    col = (t % GRID_W).astype(jnp.float32)
    ar = row[:, None] * inv[None, :]
    ac = col[:, None] * inv[None, :]
    ang = jnp.concatenate([ar, ar, ac, ac], axis=-1)
    return jnp.cos(ang), jnp.sin(ang)


def apply_rope(t, cos, sin):
    r1, r2, c1, c2 = jnp.split(t, 4, axis=-1)
    rot = jnp.concatenate([-r2, r1, -c2, c1], axis=-1)
    return t * cos[:, None, :].astype(t.dtype) + rot * sin[:, None, :].astype(t.dtype)


def fourier_mix(u):
    b, l, _ = u.shape
    ug = u.reshape(b, l, FOURIER_GROUPS, FOURIER_GROUP_DIM).astype(jnp.float32)
    y = jnp.fft.fft2(ug, axes=(1, 3), norm="ortho").real
    return y.reshape(b, l, FOURIER_WIDTH).astype(u.dtype)


def retention_scan(q, k, v, log_gamma, state0):
    b, l, h, dk = k.shape
    dv = v.shape[-1]
    n = l // RET_CHUNK
    kc = k.reshape(b, n, RET_CHUNK, h, dk).astype(jnp.float32)
    vc = v.reshape(b, n, RET_CHUNK, h, dv).astype(jnp.float32)
    pos = jnp.arange(RET_CHUNK, dtype=jnp.float32)
    lg = log_gamma[:, None]
    zeta = jnp.exp(lg * (RET_CHUNK - 1.0 - pos)).T
    d_state = jnp.einsum('bnjhd,bnjhe->nbhde', kc, vc * zeta[:, :, None])
    chunk_decay = jnp.exp(log_gamma * RET_CHUNK)[None, :, None, None]

    def step(state, ds):
        return state * chunk_decay + ds, state

    final, prev = lax.scan(step, state0, d_state)
    if q is None:
        return None, final
    qc = q.reshape(b, n, RET_CHUNK, h, dk).astype(jnp.float32)
    diff = pos[:, None] - pos[None, :]
    decay = jnp.where(diff >= 0, jnp.exp(log_gamma[:, None, None] * jnp.maximum(diff, 0.0)), 0.0)
    scores = jnp.einsum('bnihd,bnjhd->bnhij', qc, kc) * decay
    inner = jnp.einsum('bnhij,bnjhe->bnihe', scores, vc)
    xi = jnp.exp(lg * (pos + 1.0)).T
    cross = jnp.einsum('bnihd,nbhde->bnihe', qc * xi[:, :, None], prev)
    return (inner + cross).reshape(b, l, h, dv), final


def head_norm(o, g):
    b, l, h, dv = o.shape
    y = o * lax.rsqrt(jnp.mean(o * o, axis=-1, keepdims=True) + NORM_EPS)
    return y.reshape(b, l, h * dv) * g.astype(jnp.float32)


def even_mixer(h_lat, h_ctx, w_in, ret_gn, w_out, rows, need_ctx):
    rope = axial_rope_tables(rows, RET_QK_DIM)
    log_gamma = jnp.log(1.0 - jnp.exp2(-5.0 - jnp.arange(RET_HEADS, dtype=jnp.float32)))

    def project(h, rotate):
        b, l, _ = h.shape
        f, q, k, v, gf, gb = _split(h @ w_in, EVEN_IN_SIZES)
        q = q.reshape(b, l, RET_HEADS, RET_QK_DIM)
        k = k.reshape(b, l, RET_HEADS, RET_QK_DIM) * RET_QK_DIM ** -0.5
        v = v.reshape(b, l, RET_HEADS, RET_V_DIM)
        if rotate:
            q = apply_rope(q, *rope)
            k = apply_rope(k, *rope)
        return f, q, k, v, gf, gb

    f_l, q_l, k_l, v_l, gf_l, gb_l = project(h_lat, True)
    f_c, q_c, k_c, v_c, gf_c, gb_c = project(h_ctx, False)
    rev = lambda t: t[:, ::-1]
    zero = jnp.zeros((h_lat.shape[0], RET_HEADS, RET_QK_DIM, RET_V_DIM), jnp.float32)
    o_cf, st_f = retention_scan(q_c if need_ctx else None, k_c, v_c, log_gamma, zero)
    o_cb, st_b = retention_scan(rev(q_c) if need_ctx else None, rev(k_c), rev(v_c), log_gamma, zero)
    o_lf, _ = retention_scan(q_l, k_l, v_l, log_gamma, st_f)
    o_lb, _ = retention_scan(rev(q_l), rev(k_l), rev(v_l), log_gamma, st_b)

    def merge(f, o_f, o_b_rev, gf, gb):
        ret = head_norm(o_f, ret_gn) * jax.nn.silu(gf) + head_norm(rev(o_b_rev), ret_gn) * jax.nn.silu(gb)
        return jnp.concatenate([fourier_mix(f), ret.astype(f.dtype)], axis=-1) @ w_out

    y_lat = merge(f_l, o_lf, o_lb, gf_l, gb_l)
    y_ctx = merge(f_c, o_cf, o_cb, gf_c, gb_c) if need_ctx else None
    return y_lat, y_ctx


def sink_softmax(s, sink):
    sk = sink.astype(jnp.float32)[:, :, None, None]
    m = jnp.maximum(jnp.max(s, axis=-1, keepdims=True), sk)
    e = jnp.exp(s - m)
    return e / (jnp.sum(e, axis=-1, keepdims=True) + jnp.exp(sk - m))


def window_attention(q, k, v, k_ctx, v_ctx, sink):
    b, l, hq, dh = q.shape
    nb = l // SWA_BLOCK
    n_ctx = k_ctx.shape[1]
    qb = jnp.moveaxis(q.reshape(b, nb, SWA_BLOCK, SWA_KV_HEADS, SWA_GROUP, dh), 1, 0)
    pad = jnp.zeros((b, SWA_BLOCK, SWA_KV_HEADS, dh), k.dtype)
    kp = jnp.concatenate([pad, k, pad], axis=1)
    vp = jnp.concatenate([pad, v, pad], axis=1)
    offs_q = jnp.arange(SWA_BLOCK)
    offs_k = jnp.arange(3 * SWA_BLOCK) - SWA_BLOCK

    def block(args):
        qi, i = args
        start = i * SWA_BLOCK
        ki = lax.dynamic_slice_in_dim(kp, start, 3 * SWA_BLOCK, axis=1)
        vi = lax.dynamic_slice_in_dim(vp, start, 3 * SWA_BLOCK, axis=1)
        qpos = start + offs_q
        kpos = start + offs_k
        valid = (jnp.abs(qpos[:, None] - kpos[None, :]) <= SWA_WINDOW) & (kpos >= 0)[None, :] & (kpos < l)[None, :]
        s_loc = jnp.where(valid, jnp.einsum('bqhgd,bkhd->bhgqk', qi, ki).astype(jnp.float32), MASK_VALUE)
        s_ctx = jnp.einsum('bqhgd,bkhd->bhgqk', qi, k_ctx).astype(jnp.float32)
        p = sink_softmax(jnp.concatenate([s_ctx, s_loc], axis=-1), sink).astype(v.dtype)
        o = jnp.einsum('bhgqk,bkhd->bqhgd', p[..., :n_ctx], v_ctx) + jnp.einsum('bhgqk,bkhd->bqhgd', p[..., n_ctx:], vi)
        return o.reshape(b, SWA_BLOCK, hq * dh)

    out = lax.map(block, (qb, jnp.arange(nb)))
    return jnp.moveaxis(out, 0, 1).reshape(b, l, hq * dh)


def context_window_heads(q, k, v, sink):
    b, l, hq, dh = q.shape
    qg = q.reshape(b, l, SWA_KV_HEADS, SWA_GROUP, dh)
    s = jnp.einsum('bqhgd,bkhd->bhgqk', qg, k).astype(jnp.float32)
    p = sink_softmax(s, sink).astype(v.dtype)
    return jnp.einsum('bhgqk,bkhd->bqhgd', p, v).reshape(b, l, hq * dh)


def dense_block_attention(q, k, v):
    b, l, h, dq = q.shape
    nb = l // MLA_BLOCK
    qb = jnp.moveaxis(q.reshape(b, nb, MLA_BLOCK, h, dq), 1, 0)

    def block(qi):
        p = jax.nn.softmax(jnp.einsum('bqhd,bkhd->bhqk', qi, k).astype(jnp.float32), axis=-1).astype(v.dtype)
        return jnp.einsum('bhqk,bkhd->bqhd', p, v).reshape(b, MLA_BLOCK, h * v.shape[-1])

    out = lax.map(block, qb)
    return jnp.moveaxis(out, 0, 1).reshape(b, l, h * v.shape[-1])


def odd_mixer(h_lat, h_ctx, w_in, sink, q_norm, kv_norm, w_uq, w_ukv, w_out, rows, need_ctx):
    rope_swa = axial_rope_tables(rows, SWA_HEAD_DIM)
    rope_mla = axial_rope_tables(rows, MLA_ROPE)

    def project(h, rotate, with_q):
        b, l, _ = h.shape
        sq, sk, sv, cq, ckv, kr = _split(h @ w_in, ODD_IN_SIZES)
        sk = sk.reshape(b, l, SWA_KV_HEADS, SWA_HEAD_DIM)
        sv = sv.reshape(b, l, SWA_KV_HEADS, SWA_HEAD_DIM)
        mkv = (rms_norm(ckv, kv_norm) @ w_ukv).reshape(b, l, MLA_HEADS, MLA_NOPE + MLA_V)
        k_nope, mv = jnp.split(mkv, [MLA_NOPE], axis=-1)
        kr = kr[:, :, None, :]
        if rotate:
            sk = apply_rope(sk, *rope_swa)
            kr = apply_rope(kr, *rope_mla)
        mk = jnp.concatenate([k_nope, jnp.broadcast_to(kr, (b, l, MLA_HEADS, MLA_ROPE))], axis=-1)
        if not with_q:
            return None, sk, sv, None, mk, mv
        sq = sq.reshape(b, l, SWA_Q_HEADS, SWA_HEAD_DIM) * SWA_HEAD_DIM ** -0.5
        mq = (rms_norm(cq, q_norm) @ w_uq).reshape(b, l, MLA_HEADS, MLA_QK_DIM)
        q_nope, q_rope = jnp.split(mq, [MLA_NOPE], axis=-1)
        if rotate:
            sq = apply_rope(sq, *rope_swa)
            q_rope = apply_rope(q_rope, *rope_mla)
        mq = jnp.concatenate([q_nope, q_rope], axis=-1) * MLA_QK_DIM ** -0.5
        return sq, sk, sv, mq, mk, mv

    sq_l, sk_l, sv_l, mq_l, mk_l, mv_l = project(h_lat, True, True)
    sq_c, sk_c, sv_c, mq_c, mk_c, mv_c = project(h_ctx, False, need_ctx)
    sink_g = sink.reshape(SWA_KV_HEADS, SWA_GROUP)
    swa_l = window_attention(sq_l, sk_l, sv_l, sk_c, sv_c, sink_g)
    mla_l = dense_block_attention(mq_l, jnp.concatenate([mk_c, mk_l], axis=1), jnp.concatenate([mv_c, mv_l], axis=1))
    y_lat = jnp.concatenate([swa_l, mla_l], axis=-1) @ w_out
    if not need_ctx:
        return y_lat, None
    swa_c = context_window_heads(sq_c, sk_c, sv_c, sink_g)
    mla_c = dense_block_attention(mq_c, mk_c, mv_c)
    y_ctx = jnp.concatenate([swa_c, mla_c], axis=-1) @ w_out
    return y_lat, y_ctx


def peer(h, w_q, sub_keys, u_tab, v_tab):
    n_tok, d = h.shape
    hb = h.reshape(n_tok // PEER_TOKEN_BLOCK, PEER_TOKEN_BLOCK, d)

    def block(x):
        t = x.shape[0]
        q = (x @ w_q).reshape(t, PEER_HEADS, 2, PEER_QDIM // 2)
        s = jnp.einsum('thpc,hpkc->thpk', q, sub_keys).astype(jnp.float32)
        top_s, top_i = lax.top_k(s, PEER_TOPK)
        cand_s = top_s[:, :, 0, :, None] + top_s[:, :, 1, None, :]
        cand_i = top_i[:, :, 0, :, None] * PEER_KEYS + top_i[:, :, 1, None, :]
        best_s, best_j = lax.top_k(cand_s.reshape(t, PEER_HEADS, -1), PEER_TOPK)
        idx = jnp.take_along_axis(cand_i.reshape(t, PEER_HEADS, -1), best_j, axis=-1)
        g = jax.nn.softmax(best_s, axis=-1)
        a = jnp.einsum('td,thkd->thk', x, u_tab[idx]).astype(jnp.float32)
        w = (g * jax.nn.gelu(a)).astype(x.dtype)
        return jnp.einsum('thk,thkd->td', w, v_tab[idx])

    return lax.map(block, hb).reshape(n_tok, d)


def setup_inputs(seed: int = 0) -> dict:
    key = jax.random.key(seed)
    ks = iter(jax.random.split(key, 32))
    nrm = lambda shape, scale: jax.random.normal(next(ks), shape, jnp.float32) * scale
    d = D_MODEL
    return {
        "x": nrm((BATCH, SEQ, d), 1.0),
        "c": nrm((BATCH, d), 1.0),
        "ctx": nrm((BATCH, CTX_LEN, d), 1.0),
        "c_ctx": nrm((d,), 1.0),
        "w_mod": nrm((DEPTH, d, 6 * d), 0.5 * d ** -0.5),
        "b_mod": nrm((DEPTH, 6 * d), 0.01),
        "norm1": 1.0 + nrm((DEPTH, d), 0.1),
        "norm2": 1.0 + nrm((DEPTH, d), 0.1),
        "even_w_in": nrm((N_EVEN, d, EVEN_IN_WIDTH), d ** -0.5),
        "even_ret_gn": 1.0 + nrm((N_EVEN, RET_V_WIDTH), 0.1),
        "even_w_out": nrm((N_EVEN, EVEN_OUT_WIDTH, d), EVEN_OUT_WIDTH ** -0.5),
        "odd_w_in": nrm((N_ODD, d, ODD_IN_WIDTH), d ** -0.5),
        "odd_sink": nrm((N_ODD, SWA_Q_HEADS), 1.0),
        "odd_mla_q_norm": 1.0 + nrm((N_ODD, MLA_Q_RANK), 0.1),
        "odd_mla_kv_norm": 1.0 + nrm((N_ODD, MLA_KV_RANK), 0.1),
        "odd_mla_w_uq": nrm((N_ODD, MLA_Q_RANK, MLA_HEADS * MLA_QK_DIM), MLA_Q_RANK ** -0.5),
        "odd_mla_w_ukv": nrm((N_ODD, MLA_KV_RANK, MLA_HEADS * (MLA_NOPE + MLA_V)), MLA_KV_RANK ** -0.5),
        "odd_w_out": nrm((N_ODD, ODD_OUT_WIDTH, d), ODD_OUT_WIDTH ** -0.5),
        "peer_w_q": nrm((DEPTH, d, PEER_HEADS * PEER_QDIM), d ** -0.5),
        "peer_sub_keys": nrm((DEPTH, PEER_HEADS, 2, PEER_KEYS, PEER_QDIM // 2), (PEER_QDIM // 2) ** -0.5),
        "peer_u": nrm((DEPTH, PEER_EXPERTS, d), d ** -0.5),
        "peer_v": nrm((DEPTH, PEER_EXPERTS, d), 1.0),
        "final_norm": 1.0 + nrm((d,), 0.1),
    }


def reference(x, c, ctx, c_ctx, w_mod, b_mod, norm1, norm2, even_w_in, even_ret_gn, even_w_out,
              odd_w_in, odd_sink, odd_mla_q_norm, odd_mla_kv_norm, odd_mla_w_uq, odd_mla_w_ukv, odd_w_out,
              peer_w_q, peer_sub_keys, peer_u, peer_v, final_norm):
    b, s_len, d = x.shape
    n_ctx = ctx.shape[1]
    rows = s_len // GRID_W
    x_lat, x_ctx = x, ctx
    for layer in range(DEPTH):
        need_ctx = layer < DEPTH - 1
        j = layer // 2
        sh1, sc1, g1, sh2, sc2, g2 = adaln(c, w_mod[layer], b_mod[layer])
        sh1c, sc1c, g1c, sh2c, sc2c, g2c = adaln(c_ctx, w_mod[layer], b_mod[layer])
        h_lat = rms_norm(x_lat, norm1[layer]) * (1.0 + sc1) + sh1
        h_ctx = rms_norm(x_ctx, norm1[layer]) * (1.0 + sc1c) + sh1c
        if layer % 2 == 0:
            y_lat, y_ctx = even_mixer(h_lat, h_ctx, even_w_in[j], even_ret_gn[j], even_w_out[j], rows, need_ctx)
        else:
            y_lat, y_ctx = odd_mixer(h_lat, h_ctx, odd_w_in[j], odd_sink[j], odd_mla_q_norm[j], odd_mla_kv_norm[j],
                                     odd_mla_w_uq[j], odd_mla_w_ukv[j], odd_w_out[j], rows, need_ctx)
        x_lat = x_lat + g1 * y_lat
        h2_lat = rms_norm(x_lat, norm2[layer]) * (1.0 + sc2) + sh2
        if need_ctx:
            x_ctx = x_ctx + g1c * y_ctx
            h2_ctx = rms_norm(x_ctx, norm2[layer]) * (1.0 + sc2c) + sh2c
            tokens = jnp.concatenate([h2_ctx.reshape(-1, d), h2_lat.reshape(-1, d)], axis=0)
            f = peer(tokens, peer_w_q[layer], peer_sub_keys[layer], peer_u[layer], peer_v[layer])
            x_ctx = x_ctx + g2c * f[: b * n_ctx].reshape(b, n_ctx, d)
            f_lat = f[b * n_ctx:].reshape(b, s_len, d)
        else:
            f_lat = peer(h2_lat.reshape(-1, d), peer_w_q[layer], peer_sub_keys[layer], peer_u[layer], peer_v[layer]).reshape(b, s_len, d)
        x_lat = x_lat + g2 * f_lat
    return rms_norm(x_lat, final_norm)
```

```python
import functools
import math

import numpy as np
import jax
import jax.numpy as jnp
from jax import lax
from jax.experimental import pallas as pl
from jax.experimental.pallas import tpu as pltpu

F32 = jnp.float32
BF16 = jnp.bfloat16

GRID_W = 64
NORM_EPS = 1e-6
ROPE_BASE = 10000.0
MASK_VALUE = -1e30
LANES = 128

FOURIER_GROUPS = 4
FOURIER_GROUP_DIM = 256
FOURIER_WIDTH = FOURIER_GROUPS * FOURIER_GROUP_DIM
RET_HEADS = 8
RET_DIM = 128
RET_CHUNK = 128
RET_WIDTH = RET_HEADS * RET_DIM

SWA_Q_HEADS = 16
SWA_KV_HEADS = 2
SWA_GROUP = SWA_Q_HEADS // SWA_KV_HEADS
SWA_HEAD_DIM = 64
SWA_WINDOW = 128
SWA_BLOCK = 128
MLA_HEADS = 8
MLA_Q_RANK = 512
MLA_KV_RANK = 256
MLA_NOPE = 128
MLA_ROPE = 64
MLA_V = 128
MLA_QK_DIM = MLA_NOPE + MLA_ROPE
MLA_PAD = 2 * LANES

PEER_HEADS = 8
PEER_KEYS = 128
PEER_TOPK = 16
PEER_HALF = 128

VMEM_LIMIT = 56 * 1024 * 1024


def _cp(*sem):
    return pltpu.CompilerParams(dimension_semantics=sem, vmem_limit_bytes=VMEM_LIMIT)


def _adaln_kernel(c_ref, w_ref, b_ref, o_ref):
    c = c_ref[...]
    s = c * jax.nn.sigmoid(c)
    o_ref[0] = jnp.dot(s, w_ref[0], preferred_element_type=F32) + b_ref[0]


def _adaln(cond, w_mod, b_mod):
    depth, d, n = w_mod.shape
    r = cond.shape[0]
    tn = 1024
    return pl.pallas_call(
        _adaln_kernel,
        grid=(depth, n // tn),
        in_specs=[pl.BlockSpec((r, d), lambda l, j: (0, 0)),
                  pl.BlockSpec((1, d, tn), lambda l, j: (l, 0, j)),
                  pl.BlockSpec((1, 1, tn), lambda l, j: (l, 0, j))],
        out_specs=pl.BlockSpec((1, r, tn), lambda l, j: (l, 0, j)),
        out_shape=jax.ShapeDtypeStruct((depth, r, n), F32),
        compiler_params=_cp("parallel", "parallel"),
        name="adaln",
    )(cond, w_mod, b_mod.reshape(depth, 1, n))


def _hmod(x, g, mod, si):
    ms = jnp.mean(x * x, axis=-1, keepdims=True)
    y = x * lax.rsqrt(ms + NORM_EPS) * g
    return y * (1.0 + mod[si + 1:si + 2, :]) + mod[si:si + 1, :]


def _proj_kernel(x_ref, g_ref, mod_ref, w_ref, o_ref, h_scr, *, si):
    @pl.when(pl.program_id(2) == 0)
    def _():
        h_scr[...] = _hmod(x_ref[0], g_ref[...], mod_ref[0], si).astype(BF16)

    o_ref[0] = jnp.dot(h_scr[...], w_ref[...], preferred_element_type=F32).astype(o_ref.dtype)


def _proj(x, g, mod, w, *, si, tm, tn, name):
    b, t, d = x.shape
    n = w.shape[1]
    return pl.pallas_call(
        functools.partial(_proj_kernel, si=si),
        grid=(b, t // tm, n // tn),
        in_specs=[pl.BlockSpec((1, tm, d), lambda bi, i, j: (bi, i, 0)),
                  pl.BlockSpec((1, d), lambda bi, i, j: (0, 0)),
                  pl.BlockSpec((1, 6, d), lambda bi, i, j: (bi, 0, 0)),
                  pl.BlockSpec((d, tn), lambda bi, i, j: (0, j))],
        out_specs=pl.BlockSpec((1, tm, tn), lambda bi, i, j: (bi, i, j)),
        out_shape=jax.ShapeDtypeStruct((b, t, n), BF16),
        scratch_shapes=[pltpu.VMEM((tm, d), BF16)],
        compiler_params=_cp("parallel", "parallel", "arbitrary"),
        name=name,
    )(x, g.reshape(1, d), mod, w)


def _outproj_kernel(a_ref, b_ref, w_ref, x_ref, mod_ref, o_ref):
    ka = a_ref.shape[2]
    acc = jnp.dot(a_ref[0], w_ref[:ka, :], preferred_element_type=F32)
    acc = acc + jnp.dot(b_ref[0], w_ref[ka:, :], preferred_element_type=F32)
    o_ref[0] = x_ref[0] + mod_ref[0, 2:3, :] * acc


def _outproj(a, b2, w, x, mod, *, tm, tn, name):
    b, t, d = x.shape
    ka, kb = a.shape[2], b2.shape[2]
    return pl.pallas_call(
        _outproj_kernel,
        grid=(b, t // tm, d // tn),
        in_specs=[pl.BlockSpec((1, tm, ka), lambda bi, i, j: (bi, i, 0)),
                  pl.BlockSpec((1, tm, kb), lambda bi, i, j: (bi, i, 0)),
                  pl.BlockSpec((ka + kb, tn), lambda bi, i, j: (0, j)),
                  pl.BlockSpec((1, tm, tn), lambda bi, i, j: (bi, i, j)),
                  pl.BlockSpec((1, 6, tn), lambda bi, i, j: (bi, 0, j))],
        out_specs=pl.BlockSpec((1, tm, tn), lambda bi, i, j: (bi, i, j)),
        out_shape=jax.ShapeDtypeStruct((b, t, d), F32),
        compiler_params=_cp("parallel", "parallel", "arbitrary"),
        name=name,
    )(a, b2, w, x, mod)


def _rope_tables(n_pos, d_rot, width, identity=False):
    q = d_rot // 4
    if identity:
        return (jnp.ones((n_pos, width), F32), jnp.zeros((n_pos, width), F32), jnp.zeros((n_pos, width), F32))
    d_axis = d_rot // 2
    inv = ROPE_BASE ** (-jnp.arange(0, d_axis, 2, dtype=F32) / d_axis)
    t = jnp.arange(n_pos)
    row = (t // GRID_W).astype(F32)
    col = (t % GRID_W).astype(F32)
    ar = row[:, None] * inv[None, :]
    ac = col[:, None] * inv[None, :]
    ang = jnp.concatenate([ar, ar, ac, ac], axis=-1)
    cos, sin = jnp.cos(ang), jnp.sin(ang)
    quarter = np.arange(d_rot) // q
    up = jnp.asarray((quarter % 2 == 1).astype(np.float32))
    dn = jnp.asarray((quarter % 2 == 0).astype(np.float32))
    sin_p, sin_m = sin * up, -sin * dn
    reps = width // d_rot
    return tuple(jnp.tile(a, (1, reps)) for a in (cos, sin_p, sin_m))


def _pad_rope_tables(tabs, width):
    cos, sp, sm = tabs
    pad = width - cos.shape[1]
    return (jnp.pad(cos, ((0, 0), (0, pad)), constant_values=1.0), jnp.pad(sp, ((0, 0), (0, pad))),
            jnp.pad(sm, ((0, 0), (0, pad))))


def _rope(t, cos, sp, sm, q):
    w = t.shape[-1]
    return t * cos + pltpu.roll(t, q, 1) * sp + pltpu.roll(t, w - q, 1) * sm


def _dft_mats(n):
    k = np.arange(n)
    ang = 2.0 * np.pi * ((k[:, None] * k[None, :]) % n) / n
    return np.cos(ang), np.sin(ang)


def _fourier_a_kernel(u_ref, fa_ref, tc_ref, ts_ref, o_ref):
    l1 = u_ref.shape[1]
    t1 = jnp.dot(fa_ref[...], u_ref[0], preferred_element_type=F32)
    tr, ti = t1[:l1], t1[l1:]
    reps = tr.shape[1] // LANES
    c = jnp.concatenate([tc_ref[0]] * reps, axis=1)
    s = jnp.concatenate([ts_ref[0]] * reps, axis=1)
    o_ref[0, 0] = (tr * c + ti * s).astype(o_ref.dtype)
    o_ref[0, 1] = (ti * c - tr * s).astype(o_ref.dtype)


def _fourier_b_kernel(t_ref, fb_ref, cs_ref, o_ref):
    l2 = t_ref.shape[2]
    t2 = jnp.concatenate([t_ref[0, 0], t_ref[0, 1]], axis=0)
    v = jnp.dot(fb_ref[...], t2, preferred_element_type=F32).astype(BF16)
    vr, vi = v[:l2], v[l2:]
    outs = []
    for g in range(FOURIER_GROUPS):
        sl = slice(g * FOURIER_GROUP_DIM, (g + 1) * FOURIER_GROUP_DIM)
        vg = jnp.concatenate([vr[:, sl], vi[:, sl]], axis=1)
        outs.append(jnp.dot(vg, cs_ref[...], preferred_element_type=F32))
    o_ref[0] = jnp.concatenate(outs, axis=1).astype(o_ref.dtype)


def _fourier_lat(u):
    b, l, w = u.shape
    l1 = int(round(math.sqrt(l)))
    l2 = l // l1
    assert l1 * l2 == l and l1 % 16 == 0 and l2 % 16 == 0
    ca, sa = _dft_mats(l1)
    fa = jnp.asarray(np.concatenate([ca, -sa], axis=0), BF16)
    kk = np.arange(l1)[None, :, None] * np.arange(l2)[:, None, None]
    ang = 2.0 * np.pi * (kk % l) / l
    tc = jnp.asarray(np.broadcast_to(np.cos(ang), (l2, l1, LANES)), F32)
    ts = jnp.asarray(np.broadcast_to(np.sin(ang), (l2, l1, LANES)), F32)
    t2 = pl.pallas_call(
        _fourier_a_kernel,
        grid=(b, l2),
        in_specs=[pl.BlockSpec((1, l1, w), lambda bi, j: (bi, 0, j)),
                  pl.BlockSpec((2 * l1, l1), lambda bi, j: (0, 0)),
                  pl.BlockSpec((1, l1, LANES), lambda bi, j: (j, 0, 0)),
                  pl.BlockSpec((1, l1, LANES), lambda bi, j: (j, 0, 0))],
        out_specs=pl.BlockSpec((1, 2, l1, w), lambda bi, j: (bi, 0, 0, j)),
        out_shape=jax.ShapeDtypeStruct((b, 2, l1, l2 * w), BF16),
        compiler_params=_cp("parallel", "parallel"),
        name="fourier_a",
    )(u.reshape(b, l1, l2 * w), fa, tc, ts)
    cb, sb = _dft_mats(l2)
    scale = 1.0 / math.sqrt(l * FOURIER_GROUP_DIM)
    fb = jnp.asarray(np.block([[cb, sb], [-sb, cb]]) * scale, BF16)
    cc, sc = _dft_mats(FOURIER_GROUP_DIM)
    cs = jnp.asarray(np.concatenate([cc, sc], axis=0), BF16)
    y = pl.pallas_call(
        _fourier_b_kernel,
        grid=(b, l1),
        in_specs=[pl.BlockSpec((1, 2, l2, w), lambda bi, j: (bi, 0, j, 0)),
                  pl.BlockSpec((2 * l2, 2 * l2), lambda bi, j: (0, 0)),
                  pl.BlockSpec((2 * FOURIER_GROUP_DIM, FOURIER_GROUP_DIM), lambda bi, j: (0, 0))],
        out_specs=pl.BlockSpec((1, l2, w), lambda bi, j: (bi, 0, j)),
        out_shape=jax.ShapeDtypeStruct((b, l2, l1 * w), BF16),
        compiler_params=_cp("parallel", "parallel"),
        name="fourier_b",
    )(t2.reshape(b, 2, l1 * l2, w), fb, cs)
    return y.reshape(b, l, w)


def _fourier_small_kernel(u_ref, cl_ref, sl_ref, cs_ref, o_ref):
    gd = FOURIER_GROUP_DIM
    outs = []
    for g in range(FOURIER_GROUPS):
        ug = u_ref[0][:, g * gd:(g + 1) * gd]
        pq = jnp.dot(ug, cs_ref[...], preferred_element_type=F32).astype(BF16)
        y = jnp.dot(cl_ref[...], pq[:, :gd], preferred_element_type=F32)
        y = y - jnp.dot(sl_ref[...], pq[:, gd:], preferred_element_type=F32)
        outs.append(y)
    o_ref[0] = jnp.concatenate(outs, axis=1).astype(o_ref.dtype)


def _fourier_small(u):
    b, l, w = u.shape
    cl, sl = _dft_mats(l)
    scale = 1.0 / math.sqrt(l * FOURIER_GROUP_DIM)
    cc, sc = _dft_mats(FOURIER_GROUP_DIM)
    cs = jnp.asarray(np.concatenate([cc, sc], axis=1), BF16)
    return pl.pallas_call(
        _fourier_small_kernel,
        grid=(b,),
        in_specs=[pl.BlockSpec((1, l, w), lambda bi: (bi, 0, 0)),
                  pl.BlockSpec((l, l), lambda bi: (0, 0)),
                  pl.BlockSpec((l, l), lambda bi: (0, 0)),
                  pl.BlockSpec((FOURIER_GROUP_DIM, 2 * FOURIER_GROUP_DIM), lambda bi: (0, 0))],
        out_specs=pl.BlockSpec((1, l, w), lambda bi: (bi, 0, 0)),
        out_shape=jax.ShapeDtypeStruct((b, l, w), BF16),
        compiler_params=_cp("parallel"),
        name="fourier_small",
    )(u, jnp.asarray(cl * scale, BF16), jnp.asarray(sl * scale, BF16), cs)


def _ret_consts(reverse):
    c = RET_CHUNK
    lg = np.log(1.0 - np.exp2(-5.0 - np.arange(RET_HEADS, dtype=np.float32))).astype(np.float32)
    pos = np.arange(c, dtype=np.float32)
    diff = pos[:, None] - pos[None, :]
    if reverse:
        diff = -diff
    decay = np.where(diff >= 0, np.exp(lg[:, None, None] * np.maximum(diff, 0.0)), 0.0)
    if reverse:
        zeta = np.exp(lg[None, :] * pos[:, None])
        xi = np.exp(lg[None, :] * (c - pos[:, None]))
    else:
        zeta = np.exp(lg[None, :] * (c - 1.0 - pos[:, None]))
        xi = np.exp(lg[None, :] * (pos[:, None] + 1.0))
    wide = lambda a: np.repeat(a, RET_DIM, axis=1)
    chunk_decay = [float(v) for v in np.exp(lg * c)]
    return (jnp.asarray(decay, F32), jnp.asarray(wide(zeta), F32), jnp.asarray(wide(xi), F32), chunk_decay)


def _ret_kernel(*refs, chunk_decay, has_prev):
    if has_prev:
        (q_ref, k_ref, v_ref, g_ref, cos_ref, sp_ref, sm_ref, dec_ref, zeta_ref, xi_ref, gn_ref, s0_ref, prev_ref,
         o_ref, sf_ref, st_scr) = refs
    else:
        (q_ref, k_ref, v_ref, g_ref, cos_ref, sp_ref, sm_ref, dec_ref, zeta_ref, xi_ref, gn_ref, s0_ref,
         o_ref, sf_ref, st_scr) = refs
        prev_ref = None
    c = pl.program_id(1)

    @pl.when(c == 0)
    def _():
        st_scr[...] = s0_ref[0]

    cos, sp, sm = cos_ref[...], sp_ref[...], sm_ref[...]
    outs = []
    for h in range(RET_HEADS):
        sl = slice(h * RET_DIM, (h + 1) * RET_DIM)
        q = _rope(q_ref[0][:, sl].astype(F32), cos, sp, sm, RET_DIM // 4)
        k = _rope(k_ref[0][:, sl].astype(F32), cos, sp, sm, RET_DIM // 4) * (RET_DIM ** -0.5)
        v = v_ref[0][:, sl].astype(F32)
        kt = k.T.astype(BF16)
        scores = jnp.dot(q.astype(BF16), kt, preferred_element_type=F32) * dec_ref[h]
        inner = jnp.dot(scores.astype(BF16), v.astype(BF16), preferred_element_type=F32)
        state = st_scr[h]
        cross = jnp.dot((q * xi_ref[:, sl]).astype(BF16), state.astype(BF16), preferred_element_type=F32)
        o = inner + cross
        st_scr[h] = state * chunk_decay[h] + jnp.dot(kt, (v * zeta_ref[:, sl]).astype(BF16),
                                                      preferred_element_type=F32)
        y = o * lax.rsqrt(jnp.mean(o * o, axis=-1, keepdims=True) + NORM_EPS) * gn_ref[:, sl]
        gate = g_ref[0][:, sl].astype(F32)
        outs.append(y * (gate * jax.nn.sigmoid(gate)))
    res = jnp.concatenate(outs, axis=1)
    if prev_ref is not None:
        res = res + prev_ref[0].astype(F32)
    o_ref[0] = res.astype(o_ref.dtype)

    @pl.when(c == pl.num_programs(1) - 1)
    def _():
        sf_ref[0] = st_scr[...]


def _retention(proj, tabs, gn, state0, prev, *, reverse, gate_blk):
    b, t, _ = proj.shape
    w = RET_WIDTH
    n = t // RET_CHUNK
    decay, zeta, xi, chunk_decay = _ret_consts(reverse)
    pos = (lambda ci: n - 1 - ci) if reverse else (lambda ci: ci)
    blk = lambda col: pl.BlockSpec((1, RET_CHUNK, w), lambda bi, ci: (bi, pos(ci), col))
    tab = pl.BlockSpec((RET_CHUNK, LANES), lambda bi, ci: (pos(ci), 0))
    const2 = pl.BlockSpec((RET_CHUNK, w), lambda bi, ci: (0, 0))
    st = pl.BlockSpec((1, RET_HEADS, RET_DIM, RET_DIM), lambda bi, ci: (bi, 0, 0, 0))
    in_specs = [blk(1), blk(2), blk(3), blk(gate_blk), tab, tab, tab,
                pl.BlockSpec((RET_HEADS, RET_CHUNK, RET_CHUNK), lambda bi, ci: (0, 0, 0)), const2, const2,
                pl.BlockSpec((1, w), lambda bi, ci: (0, 0)), st]
    args = [proj, proj, proj, proj, *tabs, decay, zeta, xi, gn.reshape(1, w), state0]
    if prev is not None:
        in_specs.append(pl.BlockSpec((1, RET_CHUNK, w), lambda bi, ci: (bi, pos(ci), 0)))
        args.append(prev)
    return pl.pallas_call(
        functools.partial(_ret_kernel, chunk_decay=chunk_decay, has_prev=prev is not None),
        grid=(b, n),
        in_specs=in_specs,
        out_specs=[pl.BlockSpec((1, RET_CHUNK, w), lambda bi, ci: (bi, pos(ci), 0)), st],
        out_shape=[jax.ShapeDtypeStruct((b, t, w), BF16),
                   jax.ShapeDtypeStruct((b, RET_HEADS, RET_DIM, RET_DIM), F32)],
        scratch_shapes=[pltpu.VMEM((RET_HEADS, RET_DIM, RET_DIM), F32)],
        compiler_params=_cp("parallel", "arbitrary"),
        name="retention_bwd" if reverse else "retention_fwd",
    )(*args)


def _swa_kernel(q_ref, kc_ref, vc_ref, kl_ref, km_ref, kr_ref, vl_ref, vm_ref, vr_ref,
                cq_ref, spq_ref, smq_ref, cl_ref, spl_ref, sml_ref, cm_ref, spm_ref, smm_ref,
                cr_ref, spr_ref, smr_ref, sink_ref, o_ref, *, seq_len):
    i = pl.program_id(1)
    qd = SWA_HEAD_DIM // 4
    half = SWA_HEAD_DIM
    cq, spq, smq = cq_ref[...], spq_ref[...], smq_ref[...]
    rk = lambda kref, c, sp, sm: _rope(kref[0].astype(F32), c[...], sp[...], sm[...], qd)
    k_all = jnp.concatenate([kc_ref[0].astype(F32), rk(kl_ref, cl_ref, spl_ref, sml_ref),
                             rk(km_ref, cm_ref, spm_ref, smm_ref), rk(kr_ref, cr_ref, spr_ref, smr_ref)], axis=0)
    v_all = jnp.concatenate([vc_ref[0], vl_ref[0], vm_ref[0], vr_ref[0]], axis=0).astype(F32)
    n_ctx = kc_ref.shape[1]
    n_keys = k_all.shape[0]
    n_loc = 3 * SWA_BLOCK
    r = lax.broadcasted_iota(jnp.int32, (SWA_BLOCK, n_loc), 0)
    cidx = lax.broadcasted_iota(jnp.int32, (SWA_BLOCK, n_loc), 1)
    kpos = (i - 1) * SWA_BLOCK + cidx
    valid = (jnp.abs(SWA_BLOCK + r - cidx) <= SWA_WINDOW) & (kpos >= 0) & (kpos < seq_len)
    kt = k_all.T
    krow = lax.broadcasted_iota(jnp.int32, (LANES, n_keys), 0)
    kt_kv = [jnp.where(krow < half, kt, 0.0).astype(BF16), jnp.where(krow >= half, kt, 0.0).astype(BF16)]
    v_roll = pltpu.roll(v_all, half, 1)
    vlane = lax.broadcasted_iota(jnp.int32, (n_keys, LANES), 1)
    v_lo = [jnp.where(vlane < half, v_all, 0.0).astype(BF16), jnp.where(vlane < half, v_roll, 0.0).astype(BF16)]
    v_hi = [jnp.where(vlane >= half, v_roll, 0.0).astype(BF16), jnp.where(vlane >= half, v_all, 0.0).astype(BF16)]
    qlane = lax.broadcasted_iota(jnp.int32, (SWA_BLOCK, LANES), 1)
    outs = []
    for blk in range(SWA_Q_HEADS // 2):
        kv = (2 * blk) // SWA_GROUP
        qb = _rope(q_ref[0][:, blk * LANES:(blk + 1) * LANES].astype(F32), cq, spq, smq, qd) * (SWA_HEAD_DIM ** -0.5)
        qr = pltpu.roll(qb, half, 1)
        if kv == 0:
            q_even, q_odd = jnp.where(qlane < half, qb, 0.0), jnp.where(qlane < half, qr, 0.0)
        else:
            q_even, q_odd = jnp.where(qlane >= half, qr, 0.0), jnp.where(qlane >= half, qb, 0.0)
        out = None
        for qq, hq, vv in ((q_even, 2 * blk, v_lo[kv]), (q_odd, 2 * blk + 1, v_hi[kv])):
            s = jnp.dot(qq.astype(BF16), kt_kv[kv], preferred_element_type=F32)
            s = jnp.concatenate([s[:, :n_ctx], jnp.where(valid, s[:, n_ctx:], MASK_VALUE)], axis=1)
            sk = sink_ref[:, hq:hq + 1]
            m = jnp.maximum(jnp.max(s, axis=-1, keepdims=True), sk)
            e = jnp.exp(s - m)
            p = e / (jnp.sum(e, axis=-1, keepdims=True) + jnp.exp(sk - m))
            o = jnp.dot(p.astype(BF16), vv, preferred_element_type=F32)
            out = o if out is None else out + o
        outs.append(out)
    o_ref[0] = jnp.concatenate(outs, axis=1).astype(o_ref.dtype)


def _swa(proj_l, proj_c, tabs, sink, *, q_blk, k_blk, v_blk):
    b, l, _ = proj_l.shape
    n_ctx = proj_c.shape[1]
    nb = l // SWA_BLOCK
    qw = SWA_Q_HEADS * SWA_HEAD_DIM
    left = lambda i: jnp.maximum(i - 1, 0)
    right = lambda i: jnp.minimum(i + 1, nb - 1)
    kvspec = lambda col, f: pl.BlockSpec((1, SWA_BLOCK, LANES), lambda bi, i: (bi, f(i), col))
    ctxspec = lambda col: pl.BlockSpec((1, n_ctx, LANES), lambda bi, i: (bi, 0, col))
    tab = lambda f: pl.BlockSpec((SWA_BLOCK, LANES), lambda bi, i: (f(i), 0))
    ident = lambda i: i
    in_specs = [pl.BlockSpec((1, SWA_BLOCK, qw), lambda bi, i: (bi, i, q_blk)),
                ctxspec(k_blk), ctxspec(v_blk),
                kvspec(k_blk, left), kvspec(k_blk, ident), kvspec(k_blk, right),
                kvspec(v_blk, left), kvspec(v_blk, ident), kvspec(v_blk, right)]
    args = [proj_l, proj_c, proj_c, proj_l, proj_l, proj_l, proj_l, proj_l, proj_l]
    for f in (ident, left, ident, right):
        in_specs += [tab(f)] * 3
        args += list(tabs)
    in_specs.append(pl.BlockSpec((1, SWA_Q_HEADS), lambda bi, i: (0, 0)))
    args.append(sink.reshape(1, SWA_Q_HEADS).astype(F32))
    return pl.pallas_call(
        functools.partial(_swa_kernel, seq_len=l),
        grid=(b, nb),
        in_specs=in_specs,
        out_specs=pl.BlockSpec((1, SWA_BLOCK, qw), lambda bi, i: (bi, i, 0)),
        out_shape=jax.ShapeDtypeStruct((b, l, qw), BF16),
        compiler_params=_cp("parallel", "parallel"),
        name="swa",
    )(*args)


def _rms_rows(x, g):
    return x * lax.rsqrt(jnp.mean(x * x, axis=-1, keepdims=True) + NORM_EPS) * g


def _mla_q_kernel(x_ref, g_ref, w_ref, cos_ref, sp_ref, sm_ref, o_ref):
    x = _rms_rows(x_ref[0].astype(F32), g_ref[...]).astype(BF16)
    mq = jnp.dot(x, w_ref[...], preferred_element_type=F32)
    cos, sp, sm = cos_ref[...], sp_ref[...], sm_ref[...]
    scale = MLA_QK_DIM ** -0.5
    outs = []
    for h in range(MLA_HEADS):
        outs.append(mq[:, h * MLA_PAD:h * MLA_PAD + LANES] * scale)
        outs.append(_rope(mq[:, h * MLA_PAD + LANES:(h + 1) * MLA_PAD], cos, sp, sm, MLA_ROPE // 4) * scale)
    o_ref[0] = jnp.concatenate(outs, axis=1).astype(o_ref.dtype)


def _mla_q(proj, g, w, tabs, *, x_blk, tm):
    b, t, _ = proj.shape
    kq = w.shape[0]
    n = w.shape[1]
    tab = pl.BlockSpec((tm, LANES), lambda bi, i: (i, 0))
    return pl.pallas_call(
        _mla_q_kernel,
        grid=(b, t // tm),
        in_specs=[pl.BlockSpec((1, tm, kq), lambda bi, i: (bi, i, x_blk)),
                  pl.BlockSpec((1, kq), lambda bi, i: (0, 0)),
                  pl.BlockSpec((kq, n), lambda bi, i: (0, 0)), tab, tab, tab],
        out_specs=pl.BlockSpec((1, tm, n), lambda bi, i: (bi, i, 0)),
        out_shape=jax.ShapeDtypeStruct((b, t, n), BF16),
        compiler_params=_cp("parallel", "parallel"),
        name="mla_q",
    )(proj, g.reshape(1, kq), w, *tabs)


def _mla_kv_kernel(x_ref, kr_ref, g_ref, w_ref, cos_ref, sp_ref, sm_ref, k_ref, v_ref):
    x = _rms_rows(x_ref[0].astype(F32), g_ref[...]).astype(BF16)
    mkv = jnp.dot(x, w_ref[...], preferred_element_type=F32)
    kr = _rope(kr_ref[0].astype(F32), cos_ref[...], sp_ref[...], sm_ref[...], MLA_ROPE // 4)
    outs = []
    for h in range(MLA_HEADS):
        outs.append(mkv[:, h * MLA_NOPE:(h + 1) * MLA_NOPE])
        outs.append(kr)
    k_ref[0] = jnp.concatenate(outs, axis=1).astype(k_ref.dtype)
    v_ref[0] = mkv[:, MLA_HEADS * MLA_NOPE:].astype(v_ref.dtype)


def _mla_kv(proj, g, w, tabs, *, x_blk, kr_blk, tm):
    b, t, _ = proj.shape
    kk = w.shape[0]
    n = w.shape[1]
    tab = pl.BlockSpec((tm, LANES), lambda bi, i: (i, 0))
    return pl.pallas_call(
        _mla_kv_kernel,
        grid=(b, t // tm),
        in_specs=[pl.BlockSpec((1, tm, kk), lambda bi, i: (bi, i, x_blk)),
                  pl.BlockSpec((1, tm, LANES), lambda bi, i: (bi, i, kr_blk)),
                  pl.BlockSpec((1, kk), lambda bi, i: (0, 0)),
                  pl.BlockSpec((kk, n), lambda bi, i: (0, 0)), tab, tab, tab],
        out_specs=[pl.BlockSpec((1, tm, MLA_HEADS * MLA_PAD), lambda bi, i: (bi, i, 0)),
                   pl.BlockSpec((1, tm, MLA_HEADS * MLA_V), lambda bi, i: (bi, i, 0))],
        out_shape=[jax.ShapeDtypeStruct((b, t, MLA_HEADS * MLA_PAD), BF16),
                   jax.ShapeDtypeStruct((b, t, MLA_HEADS * MLA_V), BF16)],
        compiler_params=_cp("parallel", "parallel"),
        name="mla_kv",
    )(proj, proj, g.reshape(1, kk), w, *tabs)


def _flash_kernel(q_ref, k_ref, v_ref, o_ref, m_scr, l_scr, acc_scr):
    kj = pl.program_id(3)

    @pl.when(kj == 0)
    def _():
        m_scr[...] = jnp.full_like(m_scr, MASK_VALUE)
        l_scr[...] = jnp.zeros_like(l_scr)
        acc_scr[...] = jnp.zeros_like(acc_scr)

    s = lax.dot_general(q_ref[0], k_ref[0], (((1,), (1,)), ((), ())), preferred_element_type=F32)
    m_prev = m_scr[...]
    m_new = jnp.maximum(m_prev, jnp.max(s, axis=-1, keepdims=True))
    alpha = jnp.exp(m_prev - m_new)
    p = jnp.exp(s - m_new)
    l_scr[...] = alpha * l_scr[...] + jnp.sum(p, axis=-1, keepdims=True)
    acc_scr[...] = alpha * acc_scr[...] + jnp.dot(p.astype(BF16), v_ref[0], preferred_element_type=F32)
    m_scr[...] = m_new

    @pl.when(kj == pl.num_programs(3) - 1)
    def _():
        o_ref[0] = (acc_scr[...] / l_scr[...]).astype(o_ref.dtype)


def _flash(q, k, v, *, tq, tk):
    b, lq, _ = q.shape
    lk = k.shape[1]
    return pl.pallas_call(
        _flash_kernel,
        grid=(b, MLA_HEADS, lq // tq, lk // tk),
        in_specs=[pl.BlockSpec((1, tq, MLA_PAD), lambda bi, h, i, j: (bi, i, h)),
                  pl.BlockSpec((1, tk, MLA_PAD), lambda bi, h, i, j: (bi, j, h)),
                  pl.BlockSpec((1, tk, MLA_V), lambda bi, h, i, j: (bi, j, h))],
        out_specs=pl.BlockSpec((1, tq, MLA_V), lambda bi, h, i, j: (bi, i, h)),
        out_shape=jax.ShapeDtypeStruct((b, lq, MLA_HEADS * MLA_V), BF16),
        scratch_shapes=[pltpu.VMEM((tq, 1), F32), pltpu.VMEM((tq, 1), F32), pltpu.VMEM((tq, MLA_V), F32)],
        compiler_params=_cp("parallel", "parallel", "parallel", "arbitrary"),
        name="mla_flash",
    )(q, k, v)


def _top_values(s, k):
    vals, cur = [], s
    for _ in range(k):
        mx = jnp.max(cur, axis=-1, keepdims=True)
        vals.append(mx)
        cur = jnp.where(cur >= mx, MASK_VALUE, cur)
    return vals


def _peer_route_kernel(x_ref, g_ref, mod_ref, wq_ref, kt_ref, h_ref, s1_ref, e1_ref, th_ref, e0_ref):
    h2 = _hmod(x_ref[0], g_ref[...], mod_ref[0], 3).astype(BF16)
    h_ref[...] = h2
    qp = jnp.dot(h2, wq_ref[...], preferred_element_type=F32).astype(BF16)
    tm = qp.shape[0]
    k = PEER_TOPK
    lane = lax.broadcasted_iota(jnp.int32, (tm, k * k), 1)
    li, lk = lane // k, lane % k
    for h in range(PEER_HEADS):
        s0 = jnp.dot(qp[:, (2 * h) * PEER_HALF:(2 * h + 1) * PEER_HALF], kt_ref[2 * h],
                     preferred_element_type=F32)
        s1 = jnp.dot(qp[:, (2 * h + 1) * PEER_HALF:(2 * h + 2) * PEER_HALF], kt_ref[2 * h + 1],
                     preferred_element_type=F32)
        a = _top_values(s0, k)
        bb = _top_values(s1, k)
        ca = jnp.zeros((tm, k * k), F32)
        cb = jnp.zeros((tm, k * k), F32)
        for i in range(k):
            ca = jnp.where(li == i, a[i], ca)
            cb = jnp.where(lk == i, bb[i], cb)
        c = _top_values(ca + cb, k + 1)
        z = jnp.zeros((tm, 1), F32)
        for j in range(k):
            z = z + jnp.exp(c[j] - c[0])
        tau = 0.5 * (c[k - 1] + c[k])
        e0 = jnp.where(s0 >= a[k - 1], jnp.exp(s0 - a[0]), 0.0) / z
        e1 = jnp.where(s1 >= bb[k - 1], jnp.exp(s1 - bb[0]), 0.0)
        s1_ref[h] = s1.T
        e1_ref[h] = e1.T
        th_ref[h] = (tau - s0).T
        e0_ref[h] = e0.T


def _peer_route(x, g, mod, wq, kt, *, tm):
    b, t, d = x.shape
    nt = t // tm
    rt = lambda: pl.BlockSpec((PEER_HEADS, PEER_KEYS, tm), lambda bi, i: (0, 0, bi * nt + i))
    rshape = jax.ShapeDtypeStruct((PEER_HEADS, PEER_KEYS, b * t), F32)
    return pl.pallas_call(
        _peer_route_kernel,
        grid=(b, nt),
        in_specs=[pl.BlockSpec((1, tm, d), lambda bi, i: (bi, i, 0)),
                  pl.BlockSpec((1, d), lambda bi, i: (0, 0)),
                  pl.BlockSpec((1, 6, d), lambda bi, i: (bi, 0, 0)),
                  pl.BlockSpec(wq.shape, lambda bi, i: (0, 0)),
                  pl.BlockSpec(kt.shape, lambda bi, i: (0, 0, 0))],
        out_specs=[pl.BlockSpec((tm, d), lambda bi, i: (bi * nt + i, 0)), rt(), rt(), rt(), rt()],
        out_shape=[jax.ShapeDtypeStruct((b * t, d), BF16), rshape, rshape, rshape, rshape],
        compiler_params=_cp("parallel", "parallel"),
        name="peer_route",
    )(x, g.reshape(1, d), mod, wq, kt)


def _gelu_tanh(a):
    return 0.5 * a * (1.0 + jnp.tanh(math.sqrt(2.0 / math.pi) * (a + 0.044715 * (a * a * a))))


def _peer_expert_kernel(*refs, final):
    if final:
        (h_ref, u_ref, vt_ref, s1_ref, e1_ref, th_ref, e0_ref, x_ref, mod_ref, fn_ref, o_ref, acc_scr, w_scr) = refs
    else:
        (h_ref, u_ref, vt_ref, s1_ref, e1_ref, th_ref, e0_ref, x_ref, mod_ref, o_ref, acc_scr, w_scr) = refs
    j = pl.program_id(2)

    @pl.when(j == 0)
    def _():
        acc_scr[...] = jnp.zeros_like(acc_scr)

    at = lax.dot_general(u_ref[...], h_ref[...], (((1,), (1,)), ((), ())), preferred_element_type=F32)
    n_i1 = u_ref.shape[0] // PEER_KEYS
    for il in range(n_i1):
        row = j * n_i1 + il
        gate = None
        for h in range(PEER_HEADS):
            th = th_ref[h, pl.ds(row, 1), :]
            e0 = e0_ref[h, pl.ds(row, 1), :]
            term = jnp.where(s1_ref[h] >= th, e0 * e1_ref[h], 0.0)
            gate = term if gate is None else gate + term
        a = at[il * PEER_KEYS:(il + 1) * PEER_KEYS, :]
        w_scr[il * PEER_KEYS:(il + 1) * PEER_KEYS, :] = (gate * _gelu_tanh(a)).astype(BF16)
    acc_scr[...] += jnp.dot(vt_ref[...], w_scr[...], preferred_element_type=F32)

    @pl.when(j == pl.num_programs(2) - 1)
    def _():
        y = x_ref[0] + mod_ref[0, 5:6, :] * acc_scr[...].T
        if final:
            y = _rms_rows(y, fn_ref[...])
        o_ref[0] = y


def _peer_expert(h2, route, u, vt, x, mod, final_gain, *, tt, ec):
    b, t, d = x.shape
    nt = t // tt
    ne = u.shape[0]
    s1, e1, th, e0 = route
    once = dict(pipeline_mode=pl.Buffered(1))
    rt = lambda: pl.BlockSpec((PEER_HEADS, PEER_KEYS, tt), lambda bi, i, j: (0, 0, bi * nt + i), **once)
    in_specs = [pl.BlockSpec((tt, d), lambda bi, i, j: (bi * nt + i, 0), **once),
                pl.BlockSpec((ec, d), lambda bi, i, j: (j, 0)),
                pl.BlockSpec((d, ec), lambda bi, i, j: (0, j)),
                rt(), rt(), rt(), rt(),
                pl.BlockSpec((1, tt, d), lambda bi, i, j: (bi, i, 0), **once),
                pl.BlockSpec((1, 6, d), lambda bi, i, j: (bi, 0, 0))]
    args = [h2, u, vt, s1, e1, th, e0, x, mod]
    if final_gain is not None:
        in_specs.append(pl.BlockSpec((1, d), lambda bi, i, j: (0, 0)))
        args.append(final_gain.reshape(1, d))
    return pl.pallas_call(
        functools.partial(_peer_expert_kernel, final=final_gain is not None),
        grid=(b, nt, ne // ec),
        in_specs=in_specs,
        out_specs=pl.BlockSpec((1, tt, d), lambda bi, i, j: (bi, i, 0)),
        out_shape=jax.ShapeDtypeStruct((b, t, d), F32),
        scratch_shapes=[pltpu.VMEM((d, tt), F32), pltpu.VMEM((ec, tt), BF16)],
        compiler_params=_cp("parallel", "parallel", "arbitrary"),
        name="peer_expert",
    )(*args)


def _peer(x, g, mod, wq, kt, u, vt, final_gain, *, tm, tt, ec):
    out = _peer_route(x, g, mod, wq, kt, tm=tm)
    return _peer_expert(out[0], out[1:], u, vt, x, mod, final_gain, tt=tt, ec=ec)


def _lat_tile(t, want):
    while t % want:
        want //= 2
    return want


def kernel(x, c, ctx, c_ctx, w_mod, b_mod, norm1, norm2, even_w_in, even_ret_gn, even_w_out, odd_w_in, odd_sink, odd_mla_q_norm, odd_mla_kv_norm, odd_mla_w_uq, odd_mla_w_ukv, odd_w_out, peer_w_q, peer_sub_keys, peer_u, peer_v, final_norm):
    b, s_len, d = x.shape
    n_ctx = ctx.shape[1]
    depth = w_mod.shape[0]
    assert depth == 2 and even_w_in.shape[0] == 1 and odd_w_in.shape[0] == 1
    tm_l = _lat_tile(s_len, 1024)
    tm_c = n_ctx

    rows = 8 * ((b + 1 + 7) // 8)
    cond = jnp.zeros((rows, d), F32).at[:b].set(c).at[b].set(c_ctx)
    mods = _adaln(cond, w_mod, b_mod).reshape(depth, rows, 6, d)
    mod_l = [mods[l, :b] for l in range(depth)]
    mod_c = [jnp.broadcast_to(mods[l, b:b + 1], (b, 6, d)) for l in range(depth)]

    x_lat, x_ctx = x, ctx

    def peer_params(layer):
        kt = jnp.swapaxes(peer_sub_keys[layer].reshape(2 * PEER_HEADS, PEER_KEYS, PEER_HALF), 1, 2).astype(BF16)
        return (peer_w_q[layer].astype(BF16), kt, peer_u[layer].astype(BF16), peer_v[layer].T.astype(BF16))

    w_in = even_w_in[0].astype(BF16)
    w_out = even_w_out[0].astype(BF16)
    p_l = _proj(x_lat, norm1[0], mod_l[0], w_in, si=0, tm=tm_l, tn=512, name="even_proj_lat")
    p_c = _proj(x_ctx, norm1[0], mod_c[0], w_in, si=0, tm=tm_c, tn=512, name="even_proj_ctx")
    fm_l = _fourier_lat(p_l[..., :FOURIER_WIDTH])
    fm_c = _fourier_small(p_c[..., :FOURIER_WIDTH])
    tabs_l = _rope_tables(s_len, RET_DIM, LANES)
    tabs_c = _rope_tables(n_ctx, RET_DIM, LANES, identity=True)
    zero = jnp.zeros((b, RET_HEADS, RET_DIM, RET_DIM), F32)
    gn = even_ret_gn[0]
    rf_c, st_f = _retention(p_c, tabs_c, gn, zero, None, reverse=False, gate_blk=4)
    r_c, st_b = _retention(p_c, tabs_c, gn, zero, rf_c, reverse=True, gate_blk=5)
    rf_l, _ = _retention(p_l, tabs_l, gn, st_f, None, reverse=False, gate_blk=4)
    r_l, _ = _retention(p_l, tabs_l, gn, st_b, rf_l, reverse=True, gate_blk=5)
    x_lat = _outproj(fm_l, r_l, w_out, x_lat, mod_l[0], tm=tm_l, tn=512, name="even_out_lat")
    x_ctx = _outproj(fm_c, r_c, w_out, x_ctx, mod_c[0], tm=tm_c, tn=512, name="even_out_ctx")
    pp = peer_params(0)
    tt_l = _lat_tile(s_len, 512)
    x_lat = _peer(x_lat, norm2[0], mod_l[0], *pp, None, tm=256, tt=tt_l, ec=1024)
    x_ctx = _peer(x_ctx, norm2[0], mod_c[0], *pp, None, tm=256, tt=n_ctx, ec=1024)

    wi = odd_w_in[0]
    o_sq, o_sk, o_sv = 0, SWA_Q_HEADS * SWA_HEAD_DIM, (SWA_Q_HEADS + SWA_KV_HEADS) * SWA_HEAD_DIM
    o_cq = o_sv + SWA_KV_HEADS * SWA_HEAD_DIM
    o_ckv = o_cq + MLA_Q_RANK
    o_kr = o_ckv + MLA_KV_RANK
    w_in = jnp.concatenate([wi[:, o_sq:o_sk], wi[:, o_cq:o_ckv], wi[:, o_ckv:o_kr], wi[:, o_sk:o_sv], wi[:, o_sv:o_cq],
                            wi[:, o_kr:], jnp.zeros((d, LANES - MLA_ROPE), wi.dtype)], axis=1).astype(BF16)
    n_odd = w_in.shape[1]
    q_blk, cq_blk, ckv_blk = 0, (o_sk - o_sq) // MLA_Q_RANK, (o_sk - o_sq + MLA_Q_RANK) // MLA_KV_RANK
    k_blk = (o_sk - o_sq + MLA_Q_RANK + MLA_KV_RANK) // LANES
    v_blk, kr_blk = k_blk + 1, k_blk + 2
    tm_o = _lat_tile(s_len, 512)
    p_l = _proj(x_lat, norm1[1], mod_l[1], w_in, si=0, tm=tm_o, tn=n_odd, name="odd_proj_lat")
    p_c = _proj(x_ctx, norm1[1], mod_c[1], w_in, si=0, tm=tm_c, tn=n_odd, name="odd_proj_ctx")
    tabs_swa = _rope_tables(s_len, SWA_HEAD_DIM, LANES)
    swa = _swa(p_l, p_c, tabs_swa, odd_sink[0], q_blk=q_blk, k_blk=k_blk, v_blk=v_blk)
    wuq = odd_mla_w_uq[0].reshape(MLA_Q_RANK, MLA_HEADS, MLA_QK_DIM)
    wuq = jnp.pad(wuq, ((0, 0), (0, 0), (0, MLA_PAD - MLA_QK_DIM))).reshape(MLA_Q_RANK, MLA_HEADS * MLA_PAD)
    wukv = odd_mla_w_ukv[0].reshape(MLA_KV_RANK, MLA_HEADS, MLA_NOPE + MLA_V)
    wukv = jnp.concatenate([wukv[:, :, :MLA_NOPE].reshape(MLA_KV_RANK, -1),
                            wukv[:, :, MLA_NOPE:].reshape(MLA_KV_RANK, -1)], axis=1)
    tabs_mla = _pad_rope_tables(_rope_tables(s_len, MLA_ROPE, MLA_ROPE), LANES)
    tabs_id = _rope_tables(n_ctx, MLA_ROPE, LANES, identity=True)
    mq = _mla_q(p_l, odd_mla_q_norm[0], wuq.astype(BF16), tabs_mla, x_blk=cq_blk, tm=tm_o)
    k_l, v_l = _mla_kv(p_l, odd_mla_kv_norm[0], wukv.astype(BF16), tabs_mla, x_blk=ckv_blk, kr_blk=kr_blk, tm=tm_o)
    k_c, v_c = _mla_kv(p_c, odd_mla_kv_norm[0], wukv.astype(BF16), tabs_id, x_blk=ckv_blk, kr_blk=kr_blk, tm=tm_c)
    k_all = jnp.concatenate([k_c, k_l], axis=1)
    v_all = jnp.concatenate([v_c, v_l], axis=1)
    lk = n_ctx + s_len
    tk = max(t for t in (1280, 1024, 640, 512, 256, 128) if lk % t == 0)
    mla = _flash(mq, k_all, v_all, tq=_lat_tile(s_len, 1024), tk=tk)
    x_lat = _outproj(swa, mla, odd_w_out[0].astype(BF16), x_lat, mod_l[1], tm=tm_l, tn=512, name="odd_out_lat")
    pp = peer_params(1)
    return _peer(x_lat, norm2[1], mod_l[1], *pp, final_norm, tm=256, tt=tt_l, ec=1024)
```

```python
import functools
import math

import numpy as np
import jax
import jax.numpy as jnp
from jax import lax
from jax.experimental import pallas as pl
from jax.experimental.pallas import tpu as pltpu

F32 = jnp.float32
BF16 = jnp.bfloat16

GRID_W = 64
NORM_EPS = 1e-6
ROPE_BASE = 10000.0
MASK_VALUE = -1e30
LANES = 128

FOURIER_GROUPS = 4
FOURIER_GROUP_DIM = 256
FOURIER_WIDTH = FOURIER_GROUPS * FOURIER_GROUP_DIM
RET_HEADS = 8
RET_DIM = 128
RET_CHUNK = 128
RET_WIDTH = RET_HEADS * RET_DIM

SWA_Q_HEADS = 16
SWA_KV_HEADS = 2
SWA_GROUP = SWA_Q_HEADS // SWA_KV_HEADS
SWA_HEAD_DIM = 64
SWA_WINDOW = 128
SWA_BLOCK = 128
MLA_HEADS = 8
MLA_Q_RANK = 512
MLA_KV_RANK = 256
MLA_NOPE = 128
MLA_ROPE = 64
MLA_V = 128
MLA_QK_DIM = MLA_NOPE + MLA_ROPE
MLA_PAD = 2 * LANES

PEER_HEADS = 8
PEER_KEYS = 128
PEER_TOPK = 16
PEER_HALF = 128

VMEM_LIMIT = 56 * 1024 * 1024


def _cp(*sem):
    return pltpu.CompilerParams(dimension_semantics=sem, vmem_limit_bytes=VMEM_LIMIT)


def _adaln_kernel(c_ref, w_ref, b_ref, o_ref):
    c = c_ref[...]
    s = c * jax.nn.sigmoid(c)
    o_ref[0] = jnp.dot(s, w_ref[0], preferred_element_type=F32) + b_ref[0]


def _adaln(cond, w_mod, b_mod):
    depth, d, n = w_mod.shape
    r = cond.shape[0]
    tn = 1024
    return pl.pallas_call(
        _adaln_kernel,
        grid=(depth, n // tn),
        in_specs=[pl.BlockSpec((r, d), lambda l, j: (0, 0)),
                  pl.BlockSpec((1, d, tn), lambda l, j: (l, 0, j)),
                  pl.BlockSpec((1, 1, tn), lambda l, j: (l, 0, j))],
        out_specs=pl.BlockSpec((1, r, tn), lambda l, j: (l, 0, j)),
        out_shape=jax.ShapeDtypeStruct((depth, r, n), F32),
        compiler_params=_cp("parallel", "parallel"),
        name="adaln",
    )(cond, w_mod, b_mod.reshape(depth, 1, n))


def _hmod(x, g, mod, si):
    ms = jnp.mean(x * x, axis=-1, keepdims=True)
    y = x * lax.rsqrt(ms + NORM_EPS) * g
    return y * (1.0 + mod[si + 1:si + 2, :]) + mod[si:si + 1, :]


def _proj_kernel(x_ref, g_ref, mod_ref, w_ref, o_ref, h_scr, *, si):
    @pl.when(pl.program_id(2) == 0)
    def _():
        h_scr[...] = _hmod(x_ref[0], g_ref[...], mod_ref[0], si).astype(BF16)

    o_ref[0] = jnp.dot(h_scr[...], w_ref[...], preferred_element_type=F32).astype(o_ref.dtype)


def _proj(x, g, mod, w, *, si, tm, tn, name):
    b, t, d = x.shape
    n = w.shape[1]
    return pl.pallas_call(
        functools.partial(_proj_kernel, si=si),
        grid=(b, t // tm, n // tn),
        in_specs=[pl.BlockSpec((1, tm, d), lambda bi, i, j: (bi, i, 0)),
                  pl.BlockSpec((1, d), lambda bi, i, j: (0, 0)),
                  pl.BlockSpec((1, 6, d), lambda bi, i, j: (bi, 0, 0)),
                  pl.BlockSpec((d, tn), lambda bi, i, j: (0, j))],
        out_specs=pl.BlockSpec((1, tm, tn), lambda bi, i, j: (bi, i, j)),
        out_shape=jax.ShapeDtypeStruct((b, t, n), BF16),
        scratch_shapes=[pltpu.VMEM((tm, d), BF16)],
        compiler_params=_cp("parallel", "parallel", "arbitrary"),
        name=name,
    )(x, g.reshape(1, d), mod, w)


def _outproj_kernel(a_ref, b_ref, w_ref, x_ref, mod_ref, o_ref):
    ka = a_ref.shape[2]
    acc = jnp.dot(a_ref[0], w_ref[:ka, :], preferred_element_type=F32)
    acc = acc + jnp.dot(b_ref[0], w_ref[ka:, :], preferred_element_type=F32)
    o_ref[0] = x_ref[0] + mod_ref[0, 2:3, :] * acc


def _outproj(a, b2, w, x, mod, *, tm, tn, name):
    b, t, d = x.shape
    ka, kb = a.shape[2], b2.shape[2]
    return pl.pallas_call(
        _outproj_kernel,
        grid=(b, t // tm, d // tn),
        in_specs=[pl.BlockSpec((1, tm, ka), lambda bi, i, j: (bi, i, 0)),
                  pl.BlockSpec((1, tm, kb), lambda bi, i, j: (bi, i, 0)),
                  pl.BlockSpec((ka + kb, tn), lambda bi, i, j: (0, j)),
                  pl.BlockSpec((1, tm, tn), lambda bi, i, j: (bi, i, j)),
                  pl.BlockSpec((1, 6, tn), lambda bi, i, j: (bi, 0, j))],
        out_specs=pl.BlockSpec((1, tm, tn), lambda bi, i, j: (bi, i, j)),
        out_shape=jax.ShapeDtypeStruct((b, t, d), F32),
        compiler_params=_cp("parallel", "parallel", "arbitrary"),
        name=name,
    )(a, b2, w, x, mod)


def _rope_tables(n_pos, d_rot, width, identity=False):
    q = d_rot // 4
    if identity:
        return (jnp.ones((n_pos, width), F32), jnp.zeros((n_pos, width), F32), jnp.zeros((n_pos, width), F32))
    d_axis = d_rot // 2
    inv = ROPE_BASE ** (-jnp.arange(0, d_axis, 2, dtype=F32) / d_axis)
    t = jnp.arange(n_pos)
    row = (t // GRID_W).astype(F32)
    col = (t % GRID_W).astype(F32)
    ar = row[:, None] * inv[None, :]
    ac = col[:, None] * inv[None, :]
    ang = jnp.concatenate([ar, ar, ac, ac], axis=-1)
    cos, sin = jnp.cos(ang), jnp.sin(ang)
    quarter = np.arange(d_rot) // q
    up = jnp.asarray((quarter % 2 == 1).astype(np.float32))
    dn = jnp.asarray((quarter % 2 == 0).astype(np.float32))
    sin_p, sin_m = sin * up, -sin * dn
    reps = width // d_rot
    return tuple(jnp.tile(a, (1, reps)) for a in (cos, sin_p, sin_m))


def _pad_rope_tables(tabs, width):
    cos, sp, sm = tabs
    pad = width - cos.shape[1]
    return (jnp.pad(cos, ((0, 0), (0, pad)), constant_values=1.0), jnp.pad(sp, ((0, 0), (0, pad))),
            jnp.pad(sm, ((0, 0), (0, pad))))


def _rope(t, cos, sp, sm, q):
    w = t.shape[-1]
    return t * cos + pltpu.roll(t, q, 1) * sp + pltpu.roll(t, w - q, 1) * sm


def _dft_mats(n):
    k = np.arange(n)
    ang = 2.0 * np.pi * ((k[:, None] * k[None, :]) % n) / n
    return np.cos(ang), np.sin(ang)


def _fourier_a_kernel(u_ref, fa_ref, tc_ref, ts_ref, o_ref):
    l1 = u_ref.shape[1]
    t1 = jnp.dot(fa_ref[...], u_ref[0], preferred_element_type=F32)
    tr, ti = t1[:l1], t1[l1:]
    reps = tr.shape[1] // LANES
    c = jnp.concatenate([tc_ref[0]] * reps, axis=1)
    s = jnp.concatenate([ts_ref[0]] * reps, axis=1)
    o_ref[0, 0] = (tr * c + ti * s).astype(o_ref.dtype)
    o_ref[0, 1] = (ti * c - tr * s).astype(o_ref.dtype)


def _fourier_b_kernel(t_ref, fb_ref, cs_ref, o_ref):
    l2 = t_ref.shape[2]
    t2 = jnp.concatenate([t_ref[0, 0], t_ref[0, 1]], axis=0)
    v = jnp.dot(fb_ref[...], t2, preferred_element_type=F32).astype(BF16)
    vr, vi = v[:l2], v[l2:]
    outs = []
    for g in range(FOURIER_GROUPS):
        sl = slice(g * FOURIER_GROUP_DIM, (g + 1) * FOURIER_GROUP_DIM)
        vg = jnp.concatenate([vr[:, sl], vi[:, sl]], axis=1)
        outs.append(jnp.dot(vg, cs_ref[...], preferred_element_type=F32))
    o_ref[0] = jnp.concatenate(outs, axis=1).astype(o_ref.dtype)


def _fourier_lat(u):
    b, l, w = u.shape
    l1 = int(round(math.sqrt(l)))
    l2 = l // l1
    assert l1 * l2 == l and l1 % 16 == 0 and l2 % 16 == 0
    ca, sa = _dft_mats(l1)
    fa = jnp.asarray(np.concatenate([ca, -sa], axis=0), BF16)
    kk = np.arange(l1)[None, :, None] * np.arange(l2)[:, None, None]
    ang = 2.0 * np.pi * (kk % l) / l
    tc = jnp.asarray(np.broadcast_to(np.cos(ang), (l2, l1, LANES)), F32)
    ts = jnp.asarray(np.broadcast_to(np.sin(ang), (l2, l1, LANES)), F32)
    t2 = pl.pallas_call(
        _fourier_a_kernel,
        grid=(b, l2),
        in_specs=[pl.BlockSpec((1, l1, w), lambda bi, j: (bi, 0, j)),
                  pl.BlockSpec((2 * l1, l1), lambda bi, j: (0, 0)),
                  pl.BlockSpec((1, l1, LANES), lambda bi, j: (j, 0, 0)),
                  pl.BlockSpec((1, l1, LANES), lambda bi, j: (j, 0, 0))],
        out_specs=pl.BlockSpec((1, 2, l1, w), lambda bi, j: (bi, 0, 0, j)),
        out_shape=jax.ShapeDtypeStruct((b, 2, l1, l2 * w), BF16),
        compiler_params=_cp("parallel", "parallel"),
        name="fourier_a",
    )(u.reshape(b, l1, l2 * w), fa, tc, ts)
    cb, sb = _dft_mats(l2)
    scale = 1.0 / math.sqrt(l * FOURIER_GROUP_DIM)
    fb = jnp.asarray(np.block([[cb, sb], [-sb, cb]]) * scale, BF16)
    cc, sc = _dft_mats(FOURIER_GROUP_DIM)
    cs = jnp.asarray(np.concatenate([cc, sc], axis=0), BF16)
    y = pl.pallas_call(
        _fourier_b_kernel,
        grid=(b, l1),
        in_specs=[pl.BlockSpec((1, 2, l2, w), lambda bi, j: (bi, 0, j, 0)),
                  pl.BlockSpec((2 * l2, 2 * l2), lambda bi, j: (0, 0)),
                  pl.BlockSpec((2 * FOURIER_GROUP_DIM, FOURIER_GROUP_DIM), lambda bi, j: (0, 0))],
        out_specs=pl.BlockSpec((1, l2, w), lambda bi, j: (bi, 0, j)),
        out_shape=jax.ShapeDtypeStruct((b, l2, l1 * w), BF16),
        compiler_params=_cp("parallel", "parallel"),
        name="fourier_b",
    )(t2.reshape(b, 2, l1 * l2, w), fb, cs)
    return y.reshape(b, l, w)


def _fourier_small_kernel(u_ref, cl_ref, sl_ref, cs_ref, o_ref):
    gd = FOURIER_GROUP_DIM
    outs = []
    for g in range(FOURIER_GROUPS):
        ug = u_ref[0][:, g * gd:(g + 1) * gd]
        pq = jnp.dot(ug, cs_ref[...], preferred_element_type=F32).astype(BF16)
        y = jnp.dot(cl_ref[...], pq[:, :gd], preferred_element_type=F32)
        y = y - jnp.dot(sl_ref[...], pq[:, gd:], preferred_element_type=F32)
        outs.append(y)
    o_ref[0] = jnp.concatenate(outs, axis=1).astype(o_ref.dtype)


def _fourier_small(u):
    b, l, w = u.shape
    cl, sl = _dft_mats(l)
    scale = 1.0 / math.sqrt(l * FOURIER_GROUP_DIM)
    cc, sc = _dft_mats(FOURIER_GROUP_DIM)
    cs = jnp.asarray(np.concatenate([cc, sc], axis=1), BF16)
    return pl.pallas_call(
        _fourier_small_kernel,
        grid=(b,),
        in_specs=[pl.BlockSpec((1, l, w), lambda bi: (bi, 0, 0)),
                  pl.BlockSpec((l, l), lambda bi: (0, 0)),
                  pl.BlockSpec((l, l), lambda bi: (0, 0)),
                  pl.BlockSpec((FOURIER_GROUP_DIM, 2 * FOURIER_GROUP_DIM), lambda bi: (0, 0))],
        out_specs=pl.BlockSpec((1, l, w), lambda bi: (bi, 0, 0)),
        out_shape=jax.ShapeDtypeStruct((b, l, w), BF16),
        compiler_params=_cp("parallel"),
        name="fourier_small",
    )(u, jnp.asarray(cl * scale, BF16), jnp.asarray(sl * scale, BF16), cs)


def _ret_consts(reverse):
    c = RET_CHUNK
    lg = np.log(1.0 - np.exp2(-5.0 - np.arange(RET_HEADS, dtype=np.float32))).astype(np.float32)
    pos = np.arange(c, dtype=np.float32)
    diff = pos[:, None] - pos[None, :]
    if reverse:
        diff = -diff
    decay = np.where(diff >= 0, np.exp(lg[:, None, None] * np.maximum(diff, 0.0)), 0.0)
    if reverse:
        zeta = np.exp(lg[None, :] * pos[:, None])
        xi = np.exp(lg[None, :] * (c - pos[:, None]))
    else:
        zeta = np.exp(lg[None, :] * (c - 1.0 - pos[:, None]))
        xi = np.exp(lg[None, :] * (pos[:, None] + 1.0))
    wide = lambda a: np.repeat(a, RET_DIM, axis=1)
    chunk_decay = [float(v) for v in np.exp(lg * c)]
    return (jnp.asarray(decay, F32), jnp.asarray(wide(zeta), F32), jnp.asarray(wide(xi), F32), chunk_decay)


def _ret_kernel(*refs, chunk_decay, has_prev):
    if has_prev:
        (q_ref, k_ref, v_ref, g_ref, cos_ref, sp_ref, sm_ref, dec_ref, zeta_ref, xi_ref, gn_ref, s0_ref, prev_ref,
         o_ref, sf_ref, st_scr) = refs
    else:
        (q_ref, k_ref, v_ref, g_ref, cos_ref, sp_ref, sm_ref, dec_ref, zeta_ref, xi_ref, gn_ref, s0_ref,
         o_ref, sf_ref, st_scr) = refs
        prev_ref = None
    c = pl.program_id(1)

    @pl.when(c == 0)
    def _():
        st_scr[...] = s0_ref[0]

    cos, sp, sm = cos_ref[...], sp_ref[...], sm_ref[...]
    outs = []
    for h in range(RET_HEADS):
        sl = slice(h * RET_DIM, (h + 1) * RET_DIM)
        q = _rope(q_ref[0][:, sl].astype(F32), cos, sp, sm, RET_DIM // 4)
        k = _rope(k_ref[0][:, sl].astype(F32), cos, sp, sm, RET_DIM // 4) * (RET_DIM ** -0.5)
        v = v_ref[0][:, sl].astype(F32)
        kt = k.T.astype(BF16)
        scores = jnp.dot(q.astype(BF16), kt, preferred_element_type=F32) * dec_ref[h]
        inner = jnp.dot(scores.astype(BF16), v.astype(BF16), preferred_element_type=F32)
        state = st_scr[h]
        cross = jnp.dot((q * xi_ref[:, sl]).astype(BF16), state.astype(BF16), preferred_element_type=F32)
        o = inner + cross
        st_scr[h] = state * chunk_decay[h] + jnp.dot(kt, (v * zeta_ref[:, sl]).astype(BF16),
                                                      preferred_element_type=F32)
        y = o * lax.rsqrt(jnp.mean(o * o, axis=-1, keepdims=True) + NORM_EPS) * gn_ref[:, sl]
        gate = g_ref[0][:, sl].astype(F32)
        outs.append(y * (gate * jax.nn.sigmoid(gate)))
    res = jnp.concatenate(outs, axis=1)
    if prev_ref is not None:
        res = res + prev_ref[0].astype(F32)
    o_ref[0] = res.astype(o_ref.dtype)

    @pl.when(c == pl.num_programs(1) - 1)
    def _():
        sf_ref[0] = st_scr[...]


def _retention(proj, tabs, gn, state0, prev, *, reverse, gate_blk):
    b, t, _ = proj.shape
    w = RET_WIDTH
    n = t // RET_CHUNK
    decay, zeta, xi, chunk_decay = _ret_consts(reverse)
    pos = (lambda ci: n - 1 - ci) if reverse else (lambda ci: ci)
    blk = lambda col: pl.BlockSpec((1, RET_CHUNK, w), lambda bi, ci: (bi, pos(ci), col))
    tab = pl.BlockSpec((RET_CHUNK, LANES), lambda bi, ci: (pos(ci), 0))
    const2 = pl.BlockSpec((RET_CHUNK, w), lambda bi, ci: (0, 0))
    st = pl.BlockSpec((1, RET_HEADS, RET_DIM, RET_DIM), lambda bi, ci: (bi, 0, 0, 0))
    in_specs = [blk(1), blk(2), blk(3), blk(gate_blk), tab, tab, tab,
                pl.BlockSpec((RET_HEADS, RET_CHUNK, RET_CHUNK), lambda bi, ci: (0, 0, 0)), const2, const2,
                pl.BlockSpec((1, w), lambda bi, ci: (0, 0)), st]
    args = [proj, proj, proj, proj, *tabs, decay, zeta, xi, gn.reshape(1, w), state0]
    if prev is not None:
        in_specs.append(pl.BlockSpec((1, RET_CHUNK, w), lambda bi, ci: (bi, pos(ci), 0)))
        args.append(prev)
    return pl.pallas_call(
        functools.partial(_ret_kernel, chunk_decay=chunk_decay, has_prev=prev is not None),
        grid=(b, n),
        in_specs=in_specs,
        out_specs=[pl.BlockSpec((1, RET_CHUNK, w), lambda bi, ci: (bi, pos(ci), 0)), st],
        out_shape=[jax.ShapeDtypeStruct((b, t, w), BF16),
                   jax.ShapeDtypeStruct((b, RET_HEADS, RET_DIM, RET_DIM), F32)],
        scratch_shapes=[pltpu.VMEM((RET_HEADS, RET_DIM, RET_DIM), F32)],
        compiler_params=_cp("parallel", "arbitrary"),
        name="retention_bwd" if reverse else "retention_fwd",
    )(*args)


def _swa_kernel(q_ref, kc_ref, vc_ref, kl_ref, km_ref, kr_ref, vl_ref, vm_ref, vr_ref,
                cq_ref, spq_ref, smq_ref, cl_ref, spl_ref, sml_ref, cm_ref, spm_ref, smm_ref,
                cr_ref, spr_ref, smr_ref, sink_ref, o_ref, *, seq_len):
    i = pl.program_id(1)
    qd = SWA_HEAD_DIM // 4
    hd = SWA_HEAD_DIM
    cq, spq, smq = cq_ref[...], spq_ref[...], smq_ref[...]
    rk = lambda kref, c, sp, sm: _rope(kref[0].astype(F32), c[...], sp[...], sm[...], qd)
    k_all = jnp.concatenate([kc_ref[0].astype(F32), rk(kl_ref, cl_ref, spl_ref, sml_ref),
                             rk(km_ref, cm_ref, spm_ref, smm_ref), rk(kr_ref, cr_ref, spr_ref, smr_ref)], axis=0)
    v_all = jnp.concatenate([vc_ref[0], vl_ref[0], vm_ref[0], vr_ref[0]], axis=0).astype(F32)
    vt = v_all.T
    n_ctx = kc_ref.shape[1]
    n_keys = k_all.shape[0]
    n_loc = 3 * SWA_BLOCK
    gw = SWA_GROUP * SWA_BLOCK
    cidx = lax.broadcasted_iota(jnp.int32, (n_loc, gw), 0)
    r = lax.broadcasted_iota(jnp.int32, (n_loc, gw), 1) % SWA_BLOCK
    kpos = (i - 1) * SWA_BLOCK + cidx
    valid = (jnp.abs(SWA_BLOCK + r - cidx) <= SWA_WINDOW) & (kpos >= 0) & (kpos < seq_len)
    klane = lax.broadcasted_iota(jnp.int32, (n_keys, LANES), 1)
    vrow = lax.broadcasted_iota(jnp.int32, (LANES, n_keys), 0)
    qt_blk = [(_rope(q_ref[0][:, blk * LANES:(blk + 1) * LANES].astype(F32), cq, spq, smq, qd) * (hd ** -0.5)).T
              for blk in range(SWA_Q_HEADS // 2)]
    zeros = jnp.zeros((hd, gw), F32)
    outs = []
    for kv in range(SWA_KV_HEADS):
        pieces = []
        for h in range(kv * SWA_GROUP, (kv + 1) * SWA_GROUP):
            pieces.append(qt_blk[h // 2][(h % 2) * hd:(h % 2 + 1) * hd, :])
        qg = jnp.concatenate(pieces, axis=1)
        qg = jnp.concatenate([qg, zeros] if kv == 0 else [zeros, qg], axis=0).astype(BF16)
        in_kv = (klane < hd) if kv == 0 else (klane >= hd)
        st = jnp.dot(jnp.where(in_kv, k_all, 0.0).astype(BF16), qg, preferred_element_type=F32)
        st = jnp.concatenate([st[:n_ctx], jnp.where(valid, st[n_ctx:], MASK_VALUE)], axis=0)
        sk = sink_ref[kv]
        m = jnp.maximum(jnp.max(st, axis=0, keepdims=True), sk)
        e = jnp.exp(st - m)
        p = e * (1.0 / (jnp.sum(e, axis=0, keepdims=True) + jnp.exp(sk - m)))
        v_kv = jnp.where((vrow < hd) if kv == 0 else (vrow >= hd), vt, 0.0).astype(BF16)
        ot = jnp.dot(v_kv, p.astype(BF16), preferred_element_type=F32)[kv * hd:(kv + 1) * hd, :]
        stacked = jnp.concatenate([ot[:, g * SWA_BLOCK:(g + 1) * SWA_BLOCK] for g in range(SWA_GROUP)], axis=0)
        outs.append(stacked.T)
    o_ref[0] = jnp.concatenate(outs, axis=1).astype(o_ref.dtype)


def _swa(proj_l, proj_c, tabs, sink, *, q_blk, k_blk, v_blk):
    b, l, _ = proj_l.shape
    n_ctx = proj_c.shape[1]
    nb = l // SWA_BLOCK
    qw = SWA_Q_HEADS * SWA_HEAD_DIM
    left = lambda i: jnp.maximum(i - 1, 0)
    right = lambda i: jnp.minimum(i + 1, nb - 1)
    kvspec = lambda col, f: pl.BlockSpec((1, SWA_BLOCK, LANES), lambda bi, i: (bi, f(i), col))
    ctxspec = lambda col: pl.BlockSpec((1, n_ctx, LANES), lambda bi, i: (bi, 0, col))
    tab = lambda f: pl.BlockSpec((SWA_BLOCK, LANES), lambda bi, i: (f(i), 0))
    ident = lambda i: i
    in_specs = [pl.BlockSpec((1, SWA_BLOCK, qw), lambda bi, i: (bi, i, q_blk)),
                ctxspec(k_blk), ctxspec(v_blk),
                kvspec(k_blk, left), kvspec(k_blk, ident), kvspec(k_blk, right),
                kvspec(v_blk, left), kvspec(v_blk, ident), kvspec(v_blk, right)]
    args = [proj_l, proj_c, proj_c, proj_l, proj_l, proj_l, proj_l, proj_l, proj_l]
    for f in (ident, left, ident, right):
        in_specs += [tab(f)] * 3
        args += list(tabs)
    gw = SWA_GROUP * SWA_BLOCK
    in_specs.append(pl.BlockSpec((SWA_KV_HEADS, 1, gw), lambda bi, i: (0, 0, 0)))
    args.append(jnp.repeat(sink.reshape(SWA_KV_HEADS, SWA_GROUP).astype(F32), SWA_BLOCK, axis=1)
                .reshape(SWA_KV_HEADS, 1, gw))
    return pl.pallas_call(
        functools.partial(_swa_kernel, seq_len=l),
        grid=(b, nb),
        in_specs=in_specs,
        out_specs=pl.BlockSpec((1, SWA_BLOCK, qw), lambda bi, i: (bi, i, 0)),
        out_shape=jax.ShapeDtypeStruct((b, l, qw), BF16),
        compiler_params=_cp("parallel", "parallel"),
        name="swa",
    )(*args)


def _rms_rows(x, g):
    return x * lax.rsqrt(jnp.mean(x * x, axis=-1, keepdims=True) + NORM_EPS) * g


NT_DIMS = (((1,), (1,)), ((), ()))


def _mla_q_kernel(x_ref, g_ref, wt_ref, cos_ref, sp_ref, sm_ref, o_ref):
    x = _rms_rows(x_ref[0].astype(F32), g_ref[...]).astype(BF16)
    mqt = lax.dot_general(wt_ref[...], x, NT_DIMS, preferred_element_type=F32)
    cos, sp, sm = cos_ref[...], sp_ref[...], sm_ref[...]
    scale = MLA_QK_DIM ** -0.5 * math.log2(math.e)
    q = MLA_ROPE // 4
    for h in range(MLA_HEADS):
        o_ref[0, h * MLA_PAD:h * MLA_PAD + LANES, :] = (mqt[h * MLA_PAD:h * MLA_PAD + LANES] * scale).astype(o_ref.dtype)
        r = mqt[h * MLA_PAD + LANES:(h + 1) * MLA_PAD]
        r = r * cos + pltpu.roll(r, q, 0) * sp + pltpu.roll(r, LANES - q, 0) * sm
        o_ref[0, h * MLA_PAD + LANES:(h + 1) * MLA_PAD, :] = (r * scale).astype(o_ref.dtype)


def _mla_q(proj, g, wt, tabs_t, *, x_blk, tm):
    b, t, _ = proj.shape
    n, kq = wt.shape
    tab = pl.BlockSpec((LANES, tm), lambda bi, i: (0, i))
    return pl.pallas_call(
        _mla_q_kernel,
        grid=(b, t // tm),
        in_specs=[pl.BlockSpec((1, tm, kq), lambda bi, i: (bi, i, x_blk)),
                  pl.BlockSpec((1, kq), lambda bi, i: (0, 0)),
                  pl.BlockSpec((n, kq), lambda bi, i: (0, 0)), tab, tab, tab],
        out_specs=pl.BlockSpec((1, n, tm), lambda bi, i: (bi, 0, i)),
        out_shape=jax.ShapeDtypeStruct((b, n, t), BF16),
        compiler_params=_cp("parallel", "parallel"),
        name="mla_q",
    )(proj, g.reshape(1, kq), wt, *tabs_t)


def _mla_kv_kernel(x_ref, kr_ref, g_ref, wk_ref, wvt_ref, cos_ref, sp_ref, sm_ref, k_ref, vt_ref):
    x = _rms_rows(x_ref[0].astype(F32), g_ref[...]).astype(BF16)
    kn = jnp.dot(x, wk_ref[...], preferred_element_type=F32)
    kr = _rope(kr_ref[0].astype(F32), cos_ref[...], sp_ref[...], sm_ref[...], MLA_ROPE // 4)
    outs = []
    for h in range(MLA_HEADS):
        outs.append(kn[:, h * MLA_NOPE:(h + 1) * MLA_NOPE])
        outs.append(kr)
    k_ref[0] = jnp.concatenate(outs, axis=1).astype(k_ref.dtype)
    vt_ref[0] = lax.dot_general(wvt_ref[...], x, NT_DIMS, preferred_element_type=F32).astype(vt_ref.dtype)


def _mla_kv(proj, g, wk, wvt, tabs, *, x_blk, kr_blk, tm):
    b, t, _ = proj.shape
    kk = wk.shape[0]
    tab = pl.BlockSpec((tm, LANES), lambda bi, i: (i, 0))
    return pl.pallas_call(
        _mla_kv_kernel,
        grid=(b, t // tm),
        in_specs=[pl.BlockSpec((1, tm, kk), lambda bi, i: (bi, i, x_blk)),
                  pl.BlockSpec((1, tm, LANES), lambda bi, i: (bi, i, kr_blk)),
                  pl.BlockSpec((1, kk), lambda bi, i: (0, 0)),
                  pl.BlockSpec(wk.shape, lambda bi, i: (0, 0)),
                  pl.BlockSpec(wvt.shape, lambda bi, i: (0, 0)), tab, tab, tab],
        out_specs=[pl.BlockSpec((1, tm, MLA_HEADS * MLA_PAD), lambda bi, i: (bi, i, 0)),
                   pl.BlockSpec((1, MLA_HEADS * MLA_V, tm), lambda bi, i: (bi, 0, i))],
        out_shape=[jax.ShapeDtypeStruct((b, t, MLA_HEADS * MLA_PAD), BF16),
                   jax.ShapeDtypeStruct((b, MLA_HEADS * MLA_V, t), BF16)],
        compiler_params=_cp("parallel", "parallel"),
        name="mla_kv",
    )(proj, proj, g.reshape(1, kk), wk, wvt, *tabs)


def _flash_kernel(qt_ref, k_ref, vt_ref, o_ref):
    qt = qt_ref[0]
    tq = qt.shape[1]
    nc, _, tk = vt_ref.shape[2:]
    m = jnp.full((1, tq), MASK_VALUE, F32)
    l = jnp.zeros((1, tq), F32)
    acc = jnp.zeros((MLA_V, tq), F32)
    for c in range(nc):
        st = jnp.dot(k_ref[0, c * tk:(c + 1) * tk, :], qt, preferred_element_type=F32)
        m_new = jnp.maximum(m, jnp.max(st, axis=0, keepdims=True))
        alpha = jnp.exp2(m - m_new)
        p = jnp.exp2(st - m_new)
        l = alpha * l + jnp.sum(p, axis=0, keepdims=True)
        acc = alpha * acc + jnp.dot(vt_ref[0, 0, c], p.astype(BF16), preferred_element_type=F32)
        m = m_new
    o_ref[0] = (acc * (1.0 / l)).T.astype(o_ref.dtype)


def _flash(qt, k, vt, *, tq):
    b, _, lq = qt.shape
    lk = k.shape[1]
    nc, _, tk = vt.shape[2:]
    return pl.pallas_call(
        _flash_kernel,
        grid=(b, MLA_HEADS, lq // tq),
        in_specs=[pl.BlockSpec((1, MLA_PAD, tq), lambda bi, h, i: (bi, h, i)),
                  pl.BlockSpec((1, lk, MLA_PAD), lambda bi, h, i: (bi, 0, h)),
                  pl.BlockSpec((1, 1, nc, MLA_V, tk), lambda bi, h, i: (bi, h, 0, 0, 0))],
        out_specs=pl.BlockSpec((1, tq, MLA_V), lambda bi, h, i: (bi, i, h)),
        out_shape=jax.ShapeDtypeStruct((b, lq, MLA_HEADS * MLA_V), BF16),
        compiler_params=_cp("parallel", "parallel", "arbitrary"),
        name="mla_flash",
    )(qt, k, vt)


def _sort_network(n):
    pairs, p = [], 1
    while p < n:
        k = p
        while k >= 1:
            for j in range(k % p, n - k, 2 * k):
                for i in range(min(k, n - j - k)):
                    if (i + j) // (2 * p) == (i + j + k) // (2 * p):
                        pairs.append((i + j, i + j + k))
            k //= 2
        p *= 2
    return pairs


SUBLANES = 8


def _stack_rows(rows):
    n = rows[0].shape[1]
    sub = lax.broadcasted_iota(jnp.int32, (SUBLANES, n), 0)
    out = jnp.broadcast_to(rows[0], (SUBLANES, n))
    for r in range(1, len(rows)):
        out = jnp.where(sub == r, rows[r], out)
    return out


def _column_top(st, k):
    groups = st.shape[0] // SUBLANES
    v = [st[SUBLANES * j:SUBLANES * (j + 1), :] for j in range(groups)]
    for i, j in _sort_network(groups):
        v[i], v[j] = jnp.maximum(v[i], v[j]), jnp.minimum(v[i], v[j])
    vals = []
    for t in range(k):
        mx = jnp.max(v[0], axis=0, keepdims=True)
        vals.append(mx)
        pop = v[0] >= mx
        for j in range(k - 1 - t):
            v[j] = jnp.where(pop, v[j + 1] if j + 1 < groups else MASK_VALUE, v[j])
    return vals


def _pair_top(a, b, k):
    lo = _stack_rows(a[:SUBLANES])
    v = [lo + b[j] for j in range(k)]
    hi = [_stack_rows(a[SUBLANES * g:SUBLANES * (g + 1)]) + b[0] for g in range(1, k // SUBLANES)]
    vals = []
    for t in range(k + 1):
        head = v[0]
        for hgrp in hi:
            head = jnp.maximum(head, hgrp)
        mx = jnp.max(head, axis=0, keepdims=True)
        vals.append(mx)
        pop = v[0] >= mx
        for j in range(min(k, k - t)):
            v[j] = jnp.where(pop, v[j + 1] if j + 1 < k else MASK_VALUE, v[j])
        hi = [jnp.where(hgrp >= mx, MASK_VALUE, hgrp) for hgrp in hi]
    return vals


def _peer_route_kernel(x_ref, g_ref, mod_ref, wq_ref, ks_ref, h_ref, s1_ref, e1_ref, th_ref, e0_ref):
    h2 = _hmod(x_ref[0], g_ref[...], mod_ref[0], 3).astype(BF16)
    h_ref[...] = h2
    qp = jnp.dot(h2, wq_ref[...], preferred_element_type=F32).astype(BF16)
    k = PEER_TOPK
    nt = (((1,), (1,)), ((), ()))
    for h in range(PEER_HEADS):
        s0 = lax.dot_general(ks_ref[2 * h], qp[:, (2 * h) * PEER_HALF:(2 * h + 1) * PEER_HALF], nt,
                             preferred_element_type=F32)
        s1 = lax.dot_general(ks_ref[2 * h + 1], qp[:, (2 * h + 1) * PEER_HALF:(2 * h + 2) * PEER_HALF], nt,
                             preferred_element_type=F32)
        a = _column_top(s0, k)
        bb = _column_top(s1, k)
        c = _pair_top(a, bb, k)
        z = jnp.zeros_like(c[0])
        for j in range(k):
            z = z + jnp.exp(c[j] - c[0])
        tau = 0.5 * (c[k - 1] + c[k])
        s1_ref[h] = s1
        e1_ref[h] = jnp.where(s1 >= bb[k - 1], jnp.exp(s1 - bb[0]), 0.0)
        th_ref[h] = tau - s0
        e0_ref[h] = jnp.where(s0 >= a[k - 1], jnp.exp(s0 - a[0]), 0.0) * (1.0 / z)


def _peer_route(x, g, mod, wq, kt, *, tm):
    b, t, d = x.shape
    nt = t // tm
    rt = lambda: pl.BlockSpec((PEER_HEADS, PEER_KEYS, tm), lambda bi, i: (0, 0, bi * nt + i))
    rshape = jax.ShapeDtypeStruct((PEER_HEADS, PEER_KEYS, b * t), F32)
    return pl.pallas_call(
        _peer_route_kernel,
        grid=(b, nt),
        in_specs=[pl.BlockSpec((1, tm, d), lambda bi, i: (bi, i, 0)),
                  pl.BlockSpec((1, d), lambda bi, i: (0, 0)),
                  pl.BlockSpec((1, 6, d), lambda bi, i: (bi, 0, 0)),
                  pl.BlockSpec(wq.shape, lambda bi, i: (0, 0)),
                  pl.BlockSpec(kt.shape, lambda bi, i: (0, 0, 0))],
        out_specs=[pl.BlockSpec((tm, d), lambda bi, i: (bi * nt + i, 0)), rt(), rt(), rt(), rt()],
        out_shape=[jax.ShapeDtypeStruct((b * t, d), BF16), rshape, rshape, rshape, rshape],
        compiler_params=_cp("parallel", "parallel"),
        name="peer_route",
    )(x, g.reshape(1, d), mod, wq, kt)


def _gelu_tanh(a):
    return 0.5 * a * (1.0 + jnp.tanh(math.sqrt(2.0 / math.pi) * (a + 0.044715 * (a * a * a))))


def _peer_expert_kernel(*refs, final):
    if final:
        (h_ref, u_ref, vt_ref, s1_ref, e1_ref, th_ref, e0_ref, x_ref, mod_ref, fn_ref, o_ref, acc_scr) = refs
    else:
        (h_ref, u_ref, vt_ref, s1_ref, e1_ref, th_ref, e0_ref, x_ref, mod_ref, o_ref, acc_scr) = refs
    j = pl.program_id(2)

    @pl.when(j == 0)
    def _():
        acc_scr[...] = jnp.zeros_like(acc_scr)

    at = lax.dot_general(u_ref[...], h_ref[...], NT_DIMS, preferred_element_type=F32)
    ws = []
    for il in range(th_ref.shape[1]):
        gate = None
        for h in range(PEER_HEADS):
            th = th_ref[h, il:il + 1, :]
            e0 = e0_ref[h, il:il + 1, :]
            term = jnp.where(s1_ref[h] >= th, e0 * e1_ref[h], 0.0)
            gate = term if gate is None else gate + term
        a = at[il * PEER_KEYS:(il + 1) * PEER_KEYS, :]
        ws.append((gate * _gelu_tanh(a)).astype(BF16))
    acc_scr[...] += jnp.dot(vt_ref[...], jnp.concatenate(ws, axis=0), preferred_element_type=F32)

    @pl.when(j == pl.num_programs(2) - 1)
    def _():
        y = x_ref[0] + mod_ref[0, 5:6, :] * acc_scr[...].T
        if final:
            y = _rms_rows(y, fn_ref[...])
        o_ref[0] = y


def _peer_expert(h2, route, u, vt, x, mod, final_gain, *, tt, ec):
    b, t, d = x.shape
    nt = t // tt
    ne = u.shape[0]
    s1, e1, th, e0 = route
    once = dict(pipeline_mode=pl.Buffered(1))
    rt = lambda: pl.BlockSpec((PEER_HEADS, PEER_KEYS, tt), lambda bi, i, j: (0, 0, bi * nt + i), **once)
    rj = lambda: pl.BlockSpec((PEER_HEADS, ec // PEER_KEYS, tt), lambda bi, i, j: (0, j, bi * nt + i))
    in_specs = [pl.BlockSpec((tt, d), lambda bi, i, j: (bi * nt + i, 0), **once),
                pl.BlockSpec((ec, d), lambda bi, i, j: (j, 0)),
                pl.BlockSpec((d, ec), lambda bi, i, j: (0, j)),
                rt(), rt(), rj(), rj(),
                pl.BlockSpec((1, tt, d), lambda bi, i, j: (bi, i, 0), **once),
                pl.BlockSpec((1, 6, d), lambda bi, i, j: (bi, 0, 0))]
    args = [h2, u, vt, s1, e1, th, e0, x, mod]
    if final_gain is not None:
        in_specs.append(pl.BlockSpec((1, d), lambda bi, i, j: (0, 0)))
        args.append(final_gain.reshape(1, d))
    return pl.pallas_call(
        functools.partial(_peer_expert_kernel, final=final_gain is not None),
        grid=(b, nt, ne // ec),
        in_specs=in_specs,
        out_specs=pl.BlockSpec((1, tt, d), lambda bi, i, j: (bi, i, 0)),
        out_shape=jax.ShapeDtypeStruct((b, t, d), F32),
        scratch_shapes=[pltpu.VMEM((d, tt), F32)],
        compiler_params=_cp("parallel", "parallel", "arbitrary"),
        name="peer_expert",
    )(*args)


def _peer(x, g, mod, wq, kt, u, vt, final_gain, *, tm, tt, ec):
    out = _peer_route(x, g, mod, wq, kt, tm=tm)
    return _peer_expert(out[0], out[1:], u, vt, x, mod, final_gain, tt=tt, ec=ec)


def _lat_tile(t, want):
    while t % want:
        want //= 2
    return want


def kernel(x, c, ctx, c_ctx, w_mod, b_mod, norm1, norm2, even_w_in, even_ret_gn, even_w_out, odd_w_in, odd_sink, odd_mla_q_norm, odd_mla_kv_norm, odd_mla_w_uq, odd_mla_w_ukv, odd_w_out, peer_w_q, peer_sub_keys, peer_u, peer_v, final_norm):
    b, s_len, d = x.shape
    n_ctx = ctx.shape[1]
    depth = w_mod.shape[0]
    assert depth == 2 and even_w_in.shape[0] == 1 and odd_w_in.shape[0] == 1
    tm_l = _lat_tile(s_len, 1024)
    tm_c = n_ctx

    rows = 8 * ((b + 1 + 7) // 8)
    cond = jnp.zeros((rows, d), F32).at[:b].set(c).at[b].set(c_ctx)
    mods = _adaln(cond, w_mod, b_mod).reshape(depth, rows, 6, d)
    mod_l = [mods[l, :b] for l in range(depth)]
    mod_c = [jnp.broadcast_to(mods[l, b:b + 1], (b, 6, d)) for l in range(depth)]

    x_lat, x_ctx = x, ctx

    def peer_params(layer):
        kt = peer_sub_keys[layer].reshape(2 * PEER_HEADS, PEER_KEYS, PEER_HALF).astype(BF16)
        return (peer_w_q[layer].astype(BF16), kt, peer_u[layer].astype(BF16), peer_v[layer].T.astype(BF16))

    w_in = even_w_in[0].astype(BF16)
    w_out = even_w_out[0].astype(BF16)
    p_l = _proj(x_lat, norm1[0], mod_l[0], w_in, si=0, tm=tm_l, tn=512, name="even_proj_lat")
    p_c = _proj(x_ctx, norm1[0], mod_c[0], w_in, si=0, tm=tm_c, tn=512, name="even_proj_ctx")
    fm_l = _fourier_lat(p_l[..., :FOURIER_WIDTH])
    fm_c = _fourier_small(p_c[..., :FOURIER_WIDTH])
    tabs_l = _rope_tables(s_len, RET_DIM, LANES)
    tabs_c = _rope_tables(n_ctx, RET_DIM, LANES, identity=True)
    zero = jnp.zeros((b, RET_HEADS, RET_DIM, RET_DIM), F32)
    gn = even_ret_gn[0]
    rf_c, st_f = _retention(p_c, tabs_c, gn, zero, None, reverse=False, gate_blk=4)
    r_c, st_b = _retention(p_c, tabs_c, gn, zero, rf_c, reverse=True, gate_blk=5)
    rf_l, _ = _retention(p_l, tabs_l, gn, st_f, None, reverse=False, gate_blk=4)
    r_l, _ = _retention(p_l, tabs_l, gn, st_b, rf_l, reverse=True, gate_blk=5)
    x_lat = _outproj(fm_l, r_l, w_out, x_lat, mod_l[0], tm=tm_l, tn=512, name="even_out_lat")
    x_ctx = _outproj(fm_c, r_c, w_out, x_ctx, mod_c[0], tm=tm_c, tn=512, name="even_out_ctx")
    pp = peer_params(0)
    tt_l = _lat_tile(s_len, 512)
    x_lat = _peer(x_lat, norm2[0], mod_l[0], *pp, None, tm=256, tt=tt_l, ec=1024)
    x_ctx = _peer(x_ctx, norm2[0], mod_c[0], *pp, None, tm=256, tt=n_ctx, ec=1024)

    wi = odd_w_in[0]
    o_sq, o_sk, o_sv = 0, SWA_Q_HEADS * SWA_HEAD_DIM, (SWA_Q_HEADS + SWA_KV_HEADS) * SWA_HEAD_DIM
    o_cq = o_sv + SWA_KV_HEADS * SWA_HEAD_DIM
    o_ckv = o_cq + MLA_Q_RANK
    o_kr = o_ckv + MLA_KV_RANK
    w_in = jnp.concatenate([wi[:, o_sq:o_sk], wi[:, o_cq:o_ckv], wi[:, o_ckv:o_kr], wi[:, o_sk:o_sv], wi[:, o_sv:o_cq],
                            wi[:, o_kr:], jnp.zeros((d, LANES - MLA_ROPE), wi.dtype)], axis=1).astype(BF16)
    n_odd = w_in.shape[1]
    q_blk, cq_blk, ckv_blk = 0, (o_sk - o_sq) // MLA_Q_RANK, (o_sk - o_sq + MLA_Q_RANK) // MLA_KV_RANK
    k_blk = (o_sk - o_sq + MLA_Q_RANK + MLA_KV_RANK) // LANES
    v_blk, kr_blk = k_blk + 1, k_blk + 2
    tm_o = _lat_tile(s_len, 512)
    p_l = _proj(x_lat, norm1[1], mod_l[1], w_in, si=0, tm=tm_o, tn=n_odd, name="odd_proj_lat")
    p_c = _proj(x_ctx, norm1[1], mod_c[1], w_in, si=0, tm=tm_c, tn=n_odd, name="odd_proj_ctx")
    tabs_swa = _rope_tables(s_len, SWA_HEAD_DIM, LANES)
    swa = _swa(p_l, p_c, tabs_swa, odd_sink[0], q_blk=q_blk, k_blk=k_blk, v_blk=v_blk)
    wuq = odd_mla_w_uq[0].reshape(MLA_Q_RANK, MLA_HEADS, MLA_QK_DIM)
    wuq = jnp.pad(wuq, ((0, 0), (0, 0), (0, MLA_PAD - MLA_QK_DIM))).reshape(MLA_Q_RANK, MLA_HEADS * MLA_PAD)
    wukv = odd_mla_w_ukv[0].reshape(MLA_KV_RANK, MLA_HEADS, MLA_NOPE + MLA_V)
    wk = wukv[:, :, :MLA_NOPE].reshape(MLA_KV_RANK, -1).astype(BF16)
    wvt = wukv[:, :, MLA_NOPE:].reshape(MLA_KV_RANK, -1).T.astype(BF16)
    tabs_mla = _pad_rope_tables(_rope_tables(s_len, MLA_ROPE, MLA_ROPE), LANES)
    tabs_id = _rope_tables(n_ctx, MLA_ROPE, LANES, identity=True)
    qt = _mla_q(p_l, odd_mla_q_norm[0], wuq.T.astype(BF16), tuple(a.T for a in tabs_mla), x_blk=cq_blk, tm=tm_o)
    k_l, vt_l = _mla_kv(p_l, odd_mla_kv_norm[0], wk, wvt, tabs_mla, x_blk=ckv_blk, kr_blk=kr_blk, tm=tm_o)
    k_c, vt_c = _mla_kv(p_c, odd_mla_kv_norm[0], wk, wvt, tabs_id, x_blk=ckv_blk, kr_blk=kr_blk, tm=tm_c)
    k_all = jnp.concatenate([k_c, k_l], axis=1)
    lk = n_ctx + s_len
    tk = max(t for t in (1280, 1024, 768, 512, 256, 128) if lk % t == 0)
    vt_all = jnp.concatenate([vt_c, vt_l], axis=2).reshape(b, MLA_HEADS, MLA_V, lk // tk, tk)
    vt_all = jnp.transpose(vt_all, (0, 1, 3, 2, 4))
    mla = _flash(qt, k_all, vt_all, tq=_lat_tile(s_len, 512))
    x_lat = _outproj(swa, mla, odd_w_out[0].astype(BF16), x_lat, mod_l[1], tm=tm_l, tn=512, name="odd_out_lat")
    pp = peer_params(1)
    return _peer(x_lat, norm2[1], mod_l[1], *pp, final_norm, tm=256, tt=tt_l, ec=1024)
```

```python
import functools
import math

import numpy as np
import jax
import jax.numpy as jnp
from jax import lax
from jax.experimental import pallas as pl
from jax.experimental.pallas import tpu as pltpu

F32 = jnp.float32
BF16 = jnp.bfloat16

GRID_W = 64
NORM_EPS = 1e-6
ROPE_BASE = 10000.0
MASK_VALUE = -1e30
LANES = 128

FOURIER_GROUPS = 4
FOURIER_GROUP_DIM = 256
FOURIER_WIDTH = FOURIER_GROUPS * FOURIER_GROUP_DIM
RET_HEADS = 8
RET_DIM = 128
RET_CHUNK = 128
RET_WIDTH = RET_HEADS * RET_DIM

SWA_Q_HEADS = 16
SWA_KV_HEADS = 2
SWA_GROUP = SWA_Q_HEADS // SWA_KV_HEADS
SWA_HEAD_DIM = 64
SWA_WINDOW = 128
SWA_BLOCK = 128
MLA_HEADS = 8
MLA_Q_RANK = 512
MLA_KV_RANK = 256
MLA_NOPE = 128
MLA_ROPE = 64
MLA_V = 128
MLA_QK_DIM = MLA_NOPE + MLA_ROPE
MLA_PAD = 2 * LANES

PEER_HEADS = 8
PEER_KEYS = 128
PEER_TOPK = 16
PEER_HALF = 128

VMEM_LIMIT = 56 * 1024 * 1024


def _cp(*sem):
    return pltpu.CompilerParams(dimension_semantics=sem, vmem_limit_bytes=VMEM_LIMIT)


def _adaln_kernel(c_ref, w_ref, b_ref, o_ref):
    c = c_ref[...]
    s = c * jax.nn.sigmoid(c)
    o_ref[0] = jnp.dot(s, w_ref[0], preferred_element_type=F32) + b_ref[0]


def _adaln(cond, w_mod, b_mod):
    depth, d, n = w_mod.shape
    r = cond.shape[0]
    tn = 1024
    return pl.pallas_call(
        _adaln_kernel,
        grid=(depth, n // tn),
        in_specs=[pl.BlockSpec((r, d), lambda l, j: (0, 0)),
                  pl.BlockSpec((1, d, tn), lambda l, j: (l, 0, j)),
                  pl.BlockSpec((1, 1, tn), lambda l, j: (l, 0, j))],
        out_specs=pl.BlockSpec((1, r, tn), lambda l, j: (l, 0, j)),
        out_shape=jax.ShapeDtypeStruct((depth, r, n), F32),
        compiler_params=_cp("parallel", "parallel"),
        name="adaln",
    )(cond, w_mod, b_mod.reshape(depth, 1, n))


def _hmod(x, g, mod, si):
    ms = jnp.mean(x * x, axis=-1, keepdims=True)
    y = x * lax.rsqrt(ms + NORM_EPS) * g
    return y * (1.0 + mod[si + 1:si + 2, :]) + mod[si:si + 1, :]


def _proj_kernel(x_ref, g_ref, mod_ref, w_ref, o_ref, h_scr, *, si):
    @pl.when(pl.program_id(2) == 0)
    def _():
        h_scr[...] = _hmod(x_ref[0], g_ref[...], mod_ref[0], si).astype(BF16)

    o_ref[0] = jnp.dot(h_scr[...], w_ref[...], preferred_element_type=F32).astype(o_ref.dtype)


def _proj(x, g, mod, w, *, si, tm, tn, name):
    b, t, d = x.shape
    n = w.shape[1]
    return pl.pallas_call(
        functools.partial(_proj_kernel, si=si),
        grid=(b, t // tm, n // tn),
        in_specs=[pl.BlockSpec((1, tm, d), lambda bi, i, j: (bi, i, 0)),
                  pl.BlockSpec((1, d), lambda bi, i, j: (0, 0)),
                  pl.BlockSpec((1, 6, d), lambda bi, i, j: (bi, 0, 0)),
                  pl.BlockSpec((d, tn), lambda bi, i, j: (0, j))],
        out_specs=pl.BlockSpec((1, tm, tn), lambda bi, i, j: (bi, i, j)),
        out_shape=jax.ShapeDtypeStruct((b, t, n), BF16),
        scratch_shapes=[pltpu.VMEM((tm, d), BF16)],
        compiler_params=_cp("parallel", "parallel", "arbitrary"),
        name=name,
    )(x, g.reshape(1, d), mod, w)


def _outproj_kernel(a_ref, b_ref, w_ref, x_ref, mod_ref, o_ref):
    ka = a_ref.shape[2]
    acc = jnp.dot(a_ref[0], w_ref[:ka, :], preferred_element_type=F32)
    acc = acc + jnp.dot(b_ref[0], w_ref[ka:, :], preferred_element_type=F32)
    o_ref[0] = x_ref[0] + mod_ref[0, 2:3, :] * acc


def _outproj(a, b2, w, x, mod, *, tm, tn, name):
    b, t, d = x.shape
    ka, kb = a.shape[2], b2.shape[2]
    return pl.pallas_call(
        _outproj_kernel,
        grid=(b, t // tm, d // tn),
        in_specs=[pl.BlockSpec((1, tm, ka), lambda bi, i, j: (bi, i, 0)),
                  pl.BlockSpec((1, tm, kb), lambda bi, i, j: (bi, i, 0)),
                  pl.BlockSpec((ka + kb, tn), lambda bi, i, j: (0, j)),
                  pl.BlockSpec((1, tm, tn), lambda bi, i, j: (bi, i, j)),
                  pl.BlockSpec((1, 6, tn), lambda bi, i, j: (bi, 0, j))],
        out_specs=pl.BlockSpec((1, tm, tn), lambda bi, i, j: (bi, i, j)),
        out_shape=jax.ShapeDtypeStruct((b, t, d), F32),
        compiler_params=_cp("parallel", "parallel", "arbitrary"),
        name=name,
    )(a, b2, w, x, mod)


def _rope_tables(n_pos, d_rot, width, identity=False):
    q = d_rot // 4
    if identity:
        return (jnp.ones((n_pos, width), F32), jnp.zeros((n_pos, width), F32), jnp.zeros((n_pos, width), F32))
    d_axis = d_rot // 2
    inv = ROPE_BASE ** (-jnp.arange(0, d_axis, 2, dtype=F32) / d_axis)
    t = jnp.arange(n_pos)
    row = (t // GRID_W).astype(F32)
    col = (t % GRID_W).astype(F32)
    ar = row[:, None] * inv[None, :]
    ac = col[:, None] * inv[None, :]
    ang = jnp.concatenate([ar, ar, ac, ac], axis=-1)
    cos, sin = jnp.cos(ang), jnp.sin(ang)
    quarter = np.arange(d_rot) // q
    up = jnp.asarray((quarter % 2 == 1).astype(np.float32))
    dn = jnp.asarray((quarter % 2 == 0).astype(np.float32))
    sin_p, sin_m = sin * up, -sin * dn
    reps = width // d_rot
    return tuple(jnp.tile(a, (1, reps)) for a in (cos, sin_p, sin_m))


def _pad_rope_tables(tabs, width):
    cos, sp, sm = tabs
    pad = width - cos.shape[1]
    return (jnp.pad(cos, ((0, 0), (0, pad)), constant_values=1.0), jnp.pad(sp, ((0, 0), (0, pad))),
            jnp.pad(sm, ((0, 0), (0, pad))))


def _rope(t, cos, sp, sm, q):
    w = t.shape[-1]
    return t * cos + pltpu.roll(t, q, 1) * sp + pltpu.roll(t, w - q, 1) * sm


def _dft_mats(n):
    k = np.arange(n)
    ang = 2.0 * np.pi * ((k[:, None] * k[None, :]) % n) / n
    return np.cos(ang), np.sin(ang)


def _fourier_a_kernel(u_ref, fa_ref, tc_ref, ts_ref, o_ref):
    l1 = u_ref.shape[1]
    t1 = jnp.dot(fa_ref[...], u_ref[0], preferred_element_type=F32)
    tr, ti = t1[:l1], t1[l1:]
    reps = tr.shape[1] // LANES
    c = jnp.concatenate([tc_ref[0]] * reps, axis=1)
    s = jnp.concatenate([ts_ref[0]] * reps, axis=1)
    o_ref[0, 0] = (tr * c + ti * s).astype(o_ref.dtype)
    o_ref[0, 1] = (ti * c - tr * s).astype(o_ref.dtype)


def _fourier_b_kernel(t_ref, fb_ref, cs_ref, o_ref):
    l2 = t_ref.shape[2]
    t2 = jnp.concatenate([t_ref[0, 0], t_ref[0, 1]], axis=0)
    v = jnp.dot(fb_ref[...], t2, preferred_element_type=F32).astype(BF16)
    vr, vi = v[:l2], v[l2:]
    outs = []
    for g in range(FOURIER_GROUPS):
        sl = slice(g * FOURIER_GROUP_DIM, (g + 1) * FOURIER_GROUP_DIM)
        vg = jnp.concatenate([vr[:, sl], vi[:, sl]], axis=1)
        outs.append(jnp.dot(vg, cs_ref[...], preferred_element_type=F32))
    o_ref[0] = jnp.concatenate(outs, axis=1).astype(o_ref.dtype)


def _fourier_lat(u):
    b, l, w = u.shape
    l1 = int(round(math.sqrt(l)))
    l2 = l // l1
    assert l1 * l2 == l and l1 % 16 == 0 and l2 % 16 == 0
    ca, sa = _dft_mats(l1)
    fa = jnp.asarray(np.concatenate([ca, -sa], axis=0), BF16)
    kk = np.arange(l1)[None, :, None] * np.arange(l2)[:, None, None]
    ang = 2.0 * np.pi * (kk % l) / l
    tc = jnp.asarray(np.broadcast_to(np.cos(ang), (l2, l1, LANES)), F32)
    ts = jnp.asarray(np.broadcast_to(np.sin(ang), (l2, l1, LANES)), F32)
    t2 = pl.pallas_call(
        _fourier_a_kernel,
        grid=(b, l2),
        in_specs=[pl.BlockSpec((1, l1, w), lambda bi, j: (bi, 0, j)),
                  pl.BlockSpec((2 * l1, l1), lambda bi, j: (0, 0)),
                  pl.BlockSpec((1, l1, LANES), lambda bi, j: (j, 0, 0)),
                  pl.BlockSpec((1, l1, LANES), lambda bi, j: (j, 0, 0))],
        out_specs=pl.BlockSpec((1, 2, l1, w), lambda bi, j: (bi, 0, 0, j)),
        out_shape=jax.ShapeDtypeStruct((b, 2, l1, l2 * w), BF16),
        compiler_params=_cp("parallel", "parallel"),
        name="fourier_a",
    )(u.reshape(b, l1, l2 * w), fa, tc, ts)
    cb, sb = _dft_mats(l2)
    scale = 1.0 / math.sqrt(l * FOURIER_GROUP_DIM)
    fb = jnp.asarray(np.block([[cb, sb], [-sb, cb]]) * scale, BF16)
    cc, sc = _dft_mats(FOURIER_GROUP_DIM)
    cs = jnp.asarray(np.concatenate([cc, sc], axis=0), BF16)
    y = pl.pallas_call(
        _fourier_b_kernel,
        grid=(b, l1),
        in_specs=[pl.BlockSpec((1, 2, l2, w), lambda bi, j: (bi, 0, j, 0)),
                  pl.BlockSpec((2 * l2, 2 * l2), lambda bi, j: (0, 0)),
                  pl.BlockSpec((2 * FOURIER_GROUP_DIM, FOURIER_GROUP_DIM), lambda bi, j: (0, 0))],
        out_specs=pl.BlockSpec((1, l2, w), lambda bi, j: (bi, 0, j)),
        out_shape=jax.ShapeDtypeStruct((b, l2, l1 * w), BF16),
        compiler_params=_cp("parallel", "parallel"),
        name="fourier_b",
    )(t2.reshape(b, 2, l1 * l2, w), fb, cs)
    return y.reshape(b, l, w)


def _fourier_small_kernel(u_ref, cl_ref, sl_ref, cs_ref, o_ref):
    gd = FOURIER_GROUP_DIM
    outs = []
    for g in range(FOURIER_GROUPS):
        ug = u_ref[0][:, g * gd:(g + 1) * gd]
        pq = jnp.dot(ug, cs_ref[...], preferred_element_type=F32).astype(BF16)
        y = jnp.dot(cl_ref[...], pq[:, :gd], preferred_element_type=F32)
        y = y - jnp.dot(sl_ref[...], pq[:, gd:], preferred_element_type=F32)
        outs.append(y)
    o_ref[0] = jnp.concatenate(outs, axis=1).astype(o_ref.dtype)


def _fourier_small(u):
    b, l, w = u.shape
    cl, sl = _dft_mats(l)
    scale = 1.0 / math.sqrt(l * FOURIER_GROUP_DIM)
    cc, sc = _dft_mats(FOURIER_GROUP_DIM)
    cs = jnp.asarray(np.concatenate([cc, sc], axis=1), BF16)
    return pl.pallas_call(
        _fourier_small_kernel,
        grid=(b,),
        in_specs=[pl.BlockSpec((1, l, w), lambda bi: (bi, 0, 0)),
                  pl.BlockSpec((l, l), lambda bi: (0, 0)),
                  pl.BlockSpec((l, l), lambda bi: (0, 0)),
                  pl.BlockSpec((FOURIER_GROUP_DIM, 2 * FOURIER_GROUP_DIM), lambda bi: (0, 0))],
        out_specs=pl.BlockSpec((1, l, w), lambda bi: (bi, 0, 0)),
        out_shape=jax.ShapeDtypeStruct((b, l, w), BF16),
        compiler_params=_cp("parallel"),
        name="fourier_small",
    )(u, jnp.asarray(cl * scale, BF16), jnp.asarray(sl * scale, BF16), cs)


def _ret_consts(reverse):
    c = RET_CHUNK
    lg = np.log(1.0 - np.exp2(-5.0 - np.arange(RET_HEADS, dtype=np.float32))).astype(np.float32)
    pos = np.arange(c, dtype=np.float32)
    diff = pos[:, None] - pos[None, :]
    if reverse:
        diff = -diff
    decay = np.where(diff >= 0, np.exp(lg[:, None, None] * np.maximum(diff, 0.0)), 0.0)
    if reverse:
        zeta = np.exp(lg[None, :] * pos[:, None])
        xi = np.exp(lg[None, :] * (c - pos[:, None]))
    else:
        zeta = np.exp(lg[None, :] * (c - 1.0 - pos[:, None]))
        xi = np.exp(lg[None, :] * (pos[:, None] + 1.0))
    wide = lambda a: np.repeat(a, RET_DIM, axis=1)
    chunk_decay = [float(v) for v in np.exp(lg * c)]
    return (jnp.asarray(decay, F32), jnp.asarray(wide(zeta), F32), jnp.asarray(wide(xi), F32), chunk_decay)


def _ret_kernel(*refs, chunk_decay, has_prev):
    if has_prev:
        (q_ref, k_ref, v_ref, g_ref, cos_ref, sp_ref, sm_ref, dec_ref, zeta_ref, xi_ref, gn_ref, s0_ref, prev_ref,
         o_ref, sf_ref, st_scr) = refs
    else:
        (q_ref, k_ref, v_ref, g_ref, cos_ref, sp_ref, sm_ref, dec_ref, zeta_ref, xi_ref, gn_ref, s0_ref,
         o_ref, sf_ref, st_scr) = refs
        prev_ref = None
    c = pl.program_id(1)

    @pl.when(c == 0)
    def _():
        st_scr[...] = s0_ref[0]

    cos, sp, sm = cos_ref[...], sp_ref[...], sm_ref[...]
    outs = []
    for h in range(RET_HEADS):
        sl = slice(h * RET_DIM, (h + 1) * RET_DIM)
        q = _rope(q_ref[0][:, sl].astype(F32), cos, sp, sm, RET_DIM // 4)
        k = _rope(k_ref[0][:, sl].astype(F32), cos, sp, sm, RET_DIM // 4) * (RET_DIM ** -0.5)
        v = v_ref[0][:, sl].astype(F32)
        kt = k.T.astype(BF16)
        scores = jnp.dot(q.astype(BF16), kt, preferred_element_type=F32) * dec_ref[h]
        inner = jnp.dot(scores.astype(BF16), v.astype(BF16), preferred_element_type=F32)
        state = st_scr[h]
        cross = jnp.dot((q * xi_ref[:, sl]).astype(BF16), state.astype(BF16), preferred_element_type=F32)
        o = inner + cross
        st_scr[h] = state * chunk_decay[h] + jnp.dot(kt, (v * zeta_ref[:, sl]).astype(BF16),
                                                      preferred_element_type=F32)
        y = o * lax.rsqrt(jnp.mean(o * o, axis=-1, keepdims=True) + NORM_EPS) * gn_ref[:, sl]
        gate = g_ref[0][:, sl].astype(F32)
        outs.append(y * (gate * jax.nn.sigmoid(gate)))
    res = jnp.concatenate(outs, axis=1)
    if prev_ref is not None:
        res = res + prev_ref[0].astype(F32)
    o_ref[0] = res.astype(o_ref.dtype)

    @pl.when(c == pl.num_programs(1) - 1)
    def _():
        sf_ref[0] = st_scr[...]


def _retention(proj, tabs, gn, state0, prev, *, reverse, gate_blk):
    b, t, _ = proj.shape
    w = RET_WIDTH
    n = t // RET_CHUNK
    decay, zeta, xi, chunk_decay = _ret_consts(reverse)
    pos = (lambda ci: n - 1 - ci) if reverse else (lambda ci: ci)
    blk = lambda col: pl.BlockSpec((1, RET_CHUNK, w), lambda bi, ci: (bi, pos(ci), col))
    tab = pl.BlockSpec((RET_CHUNK, LANES), lambda bi, ci: (pos(ci), 0))
    const2 = pl.BlockSpec((RET_CHUNK, w), lambda bi, ci: (0, 0))
    st = pl.BlockSpec((1, RET_HEADS, RET_DIM, RET_DIM), lambda bi, ci: (bi, 0, 0, 0))
    in_specs = [blk(1), blk(2), blk(3), blk(gate_blk), tab, tab, tab,
                pl.BlockSpec((RET_HEADS, RET_CHUNK, RET_CHUNK), lambda bi, ci: (0, 0, 0)), const2, const2,
                pl.BlockSpec((1, w), lambda bi, ci: (0, 0)), st]
    args = [proj, proj, proj, proj, *tabs, decay, zeta, xi, gn.reshape(1, w), state0]
    if prev is not None:
        in_specs.append(pl.BlockSpec((1, RET_CHUNK, w), lambda bi, ci: (bi, pos(ci), 0)))
        args.append(prev)
    return pl.pallas_call(
        functools.partial(_ret_kernel, chunk_decay=chunk_decay, has_prev=prev is not None),
        grid=(b, n),
        in_specs=in_specs,
        out_specs=[pl.BlockSpec((1, RET_CHUNK, w), lambda bi, ci: (bi, pos(ci), 0)), st],
        out_shape=[jax.ShapeDtypeStruct((b, t, w), BF16),
                   jax.ShapeDtypeStruct((b, RET_HEADS, RET_DIM, RET_DIM), F32)],
        scratch_shapes=[pltpu.VMEM((RET_HEADS, RET_DIM, RET_DIM), F32)],
        compiler_params=_cp("parallel", "arbitrary"),
        name="retention_bwd" if reverse else "retention_fwd",
    )(*args)


def _swa_kernel(q_ref, kc_ref, vc_ref, kl_ref, km_ref, kr_ref, vl_ref, vm_ref, vr_ref,
                cq_ref, spq_ref, smq_ref, cl_ref, spl_ref, sml_ref, cm_ref, spm_ref, smm_ref,
                cr_ref, spr_ref, smr_ref, sink_ref, o_ref, *, seq_len):
    i = pl.program_id(1)
    qd = SWA_HEAD_DIM // 4
    hd = SWA_HEAD_DIM
    cq, spq, smq = cq_ref[...], spq_ref[...], smq_ref[...]
    rk = lambda kref, c, sp, sm: _rope(kref[0].astype(F32), c[...], sp[...], sm[...], qd)
    k_all = jnp.concatenate([kc_ref[0].astype(F32), rk(kl_ref, cl_ref, spl_ref, sml_ref),
                             rk(km_ref, cm_ref, spm_ref, smm_ref), rk(kr_ref, cr_ref, spr_ref, smr_ref)], axis=0)
    v_all = jnp.concatenate([vc_ref[0], vl_ref[0], vm_ref[0], vr_ref[0]], axis=0).astype(F32)
    vt = v_all.T
    n_ctx = kc_ref.shape[1]
    n_keys = k_all.shape[0]
    n_loc = 3 * SWA_BLOCK
    gw = SWA_GROUP * SWA_BLOCK
    cidx = lax.broadcasted_iota(jnp.int32, (n_loc, gw), 0)
    r = lax.broadcasted_iota(jnp.int32, (n_loc, gw), 1) % SWA_BLOCK
    kpos = (i - 1) * SWA_BLOCK + cidx
    valid = (jnp.abs(SWA_BLOCK + r - cidx) <= SWA_WINDOW) & (kpos >= 0) & (kpos < seq_len)
    klane = lax.broadcasted_iota(jnp.int32, (n_keys, LANES), 1)
    vrow = lax.broadcasted_iota(jnp.int32, (LANES, n_keys), 0)
    qt_blk = [(_rope(q_ref[0][:, blk * LANES:(blk + 1) * LANES].astype(F32), cq, spq, smq, qd) * (hd ** -0.5)).T
              for blk in range(SWA_Q_HEADS // 2)]
    zeros = jnp.zeros((hd, gw), F32)
    outs = []
    for kv in range(SWA_KV_HEADS):
        pieces = []
        for h in range(kv * SWA_GROUP, (kv + 1) * SWA_GROUP):
            pieces.append(qt_blk[h // 2][(h % 2) * hd:(h % 2 + 1) * hd, :])
        qg = jnp.concatenate(pieces, axis=1)
        qg = jnp.concatenate([qg, zeros] if kv == 0 else [zeros, qg], axis=0).astype(BF16)
        in_kv = (klane < hd) if kv == 0 else (klane >= hd)
        st = jnp.dot(jnp.where(in_kv, k_all, 0.0).astype(BF16), qg, preferred_element_type=F32)
        st = jnp.concatenate([st[:n_ctx], jnp.where(valid, st[n_ctx:], MASK_VALUE)], axis=0)
        sk = sink_ref[kv]
        m = jnp.maximum(jnp.max(st, axis=0, keepdims=True), sk)
        e = jnp.exp(st - m)
        p = e * (1.0 / (jnp.sum(e, axis=0, keepdims=True) + jnp.exp(sk - m)))
        v_kv = jnp.where((vrow < hd) if kv == 0 else (vrow >= hd), vt, 0.0).astype(BF16)
        ot = jnp.dot(v_kv, p.astype(BF16), preferred_element_type=F32)[kv * hd:(kv + 1) * hd, :]
        stacked = jnp.concatenate([ot[:, g * SWA_BLOCK:(g + 1) * SWA_BLOCK] for g in range(SWA_GROUP)], axis=0)
        outs.append(stacked.T)
    o_ref[0] = jnp.concatenate(outs, axis=1).astype(o_ref.dtype)


def _swa(proj_l, proj_c, tabs, sink, *, q_blk, k_blk, v_blk):
    b, l, _ = proj_l.shape
    n_ctx = proj_c.shape[1]
    nb = l // SWA_BLOCK
    qw = SWA_Q_HEADS * SWA_HEAD_DIM
    left = lambda i: jnp.maximum(i - 1, 0)
    right = lambda i: jnp.minimum(i + 1, nb - 1)
    kvspec = lambda col, f: pl.BlockSpec((1, SWA_BLOCK, LANES), lambda bi, i: (bi, f(i), col))
    ctxspec = lambda col: pl.BlockSpec((1, n_ctx, LANES), lambda bi, i: (bi, 0, col))
    tab = lambda f: pl.BlockSpec((SWA_BLOCK, LANES), lambda bi, i: (f(i), 0))
    ident = lambda i: i
    in_specs = [pl.BlockSpec((1, SWA_BLOCK, qw), lambda bi, i: (bi, i, q_blk)),
                ctxspec(k_blk), ctxspec(v_blk),
                kvspec(k_blk, left), kvspec(k_blk, ident), kvspec(k_blk, right),
                kvspec(v_blk, left), kvspec(v_blk, ident), kvspec(v_blk, right)]
    args = [proj_l, proj_c, proj_c, proj_l, proj_l, proj_l, proj_l, proj_l, proj_l]
    for f in (ident, left, ident, right):
        in_specs += [tab(f)] * 3
        args += list(tabs)
    gw = SWA_GROUP * SWA_BLOCK
    in_specs.append(pl.BlockSpec((SWA_KV_HEADS, 1, gw), lambda bi, i: (0, 0, 0)))
    args.append(jnp.repeat(sink.reshape(SWA_KV_HEADS, SWA_GROUP).astype(F32), SWA_BLOCK, axis=1)
                .reshape(SWA_KV_HEADS, 1, gw))
    return pl.pallas_call(
        functools.partial(_swa_kernel, seq_len=l),
        grid=(b, nb),
        in_specs=in_specs,
        out_specs=pl.BlockSpec((1, SWA_BLOCK, qw), lambda bi, i: (bi, i, 0)),
        out_shape=jax.ShapeDtypeStruct((b, l, qw), BF16),
        compiler_params=_cp("parallel", "parallel"),
        name="swa",
    )(*args)


def _rms_rows(x, g):
    return x * lax.rsqrt(jnp.mean(x * x, axis=-1, keepdims=True) + NORM_EPS) * g


NT_DIMS = (((1,), (1,)), ((), ()))


def _mla_q_kernel(x_ref, g_ref, wt_ref, cos_ref, sp_ref, sm_ref, o_ref):
    x = _rms_rows(x_ref[0].astype(F32), g_ref[...]).astype(BF16)
    mqt = lax.dot_general(wt_ref[...], x, NT_DIMS, preferred_element_type=F32)
    cos, sp, sm = cos_ref[...], sp_ref[...], sm_ref[...]
    scale = MLA_QK_DIM ** -0.5 * math.log2(math.e)
    q = MLA_ROPE // 4
    for h in range(MLA_HEADS):
        o_ref[0, h * MLA_PAD:h * MLA_PAD + LANES, :] = (mqt[h * MLA_PAD:h * MLA_PAD + LANES] * scale).astype(o_ref.dtype)
        r = mqt[h * MLA_PAD + LANES:(h + 1) * MLA_PAD]
        r = r * cos + pltpu.roll(r, q, 0) * sp + pltpu.roll(r, LANES - q, 0) * sm
        o_ref[0, h * MLA_PAD + LANES:(h + 1) * MLA_PAD, :] = (r * scale).astype(o_ref.dtype)


def _mla_q(proj, g, wt, tabs_t, *, x_blk, tm):
    b, t, _ = proj.shape
    n, kq = wt.shape
    tab = pl.BlockSpec((LANES, tm), lambda bi, i: (0, i))
    return pl.pallas_call(
        _mla_q_kernel,
        grid=(b, t // tm),
        in_specs=[pl.BlockSpec((1, tm, kq), lambda bi, i: (bi, i, x_blk)),
                  pl.BlockSpec((1, kq), lambda bi, i: (0, 0)),
                  pl.BlockSpec((n, kq), lambda bi, i: (0, 0)), tab, tab, tab],
        out_specs=pl.BlockSpec((1, n, tm), lambda bi, i: (bi, 0, i)),
        out_shape=jax.ShapeDtypeStruct((b, n, t), BF16),
        compiler_params=_cp("parallel", "parallel"),
        name="mla_q",
    )(proj, g.reshape(1, kq), wt, *tabs_t)


def _mla_kv_kernel(x_ref, kr_ref, g_ref, wk_ref, wvt_ref, cos_ref, sp_ref, sm_ref, k_ref, vt_ref):
    x = _rms_rows(x_ref[0].astype(F32), g_ref[...]).astype(BF16)
    kn = jnp.dot(x, wk_ref[...], preferred_element_type=F32)
    kr = _rope(kr_ref[0].astype(F32), cos_ref[...], sp_ref[...], sm_ref[...], MLA_ROPE // 4)
    outs = []
    for h in range(MLA_HEADS):
        outs.append(kn[:, h * MLA_NOPE:(h + 1) * MLA_NOPE])
        outs.append(kr)
    k_ref[0] = jnp.concatenate(outs, axis=1).astype(k_ref.dtype)
    vt_ref[0] = lax.dot_general(wvt_ref[...], x, NT_DIMS, preferred_element_type=F32).astype(vt_ref.dtype)


def _mla_kv(proj, g, wk, wvt, tabs, *, x_blk, kr_blk, tm):
    b, t, _ = proj.shape
    kk = wk.shape[0]
    tab = pl.BlockSpec((tm, LANES), lambda bi, i: (i, 0))
    return pl.pallas_call(
        _mla_kv_kernel,
        grid=(b, t // tm),
        in_specs=[pl.BlockSpec((1, tm, kk), lambda bi, i: (bi, i, x_blk)),
                  pl.BlockSpec((1, tm, LANES), lambda bi, i: (bi, i, kr_blk)),
                  pl.BlockSpec((1, kk), lambda bi, i: (0, 0)),
                  pl.BlockSpec(wk.shape, lambda bi, i: (0, 0)),
                  pl.BlockSpec(wvt.shape, lambda bi, i: (0, 0)), tab, tab, tab],
        out_specs=[pl.BlockSpec((1, tm, MLA_HEADS * MLA_PAD), lambda bi, i: (bi, i, 0)),
                   pl.BlockSpec((1, MLA_HEADS * MLA_V, tm), lambda bi, i: (bi, 0, i))],
        out_shape=[jax.ShapeDtypeStruct((b, t, MLA_HEADS * MLA_PAD), BF16),
                   jax.ShapeDtypeStruct((b, MLA_HEADS * MLA_V, t), BF16)],
        compiler_params=_cp("parallel", "parallel"),
        name="mla_kv",
    )(proj, proj, g.reshape(1, kk), wk, wvt, *tabs)


def _flash_kernel(qt_ref, k_ref, vt_ref, o_ref):
    qt = qt_ref[0]
    tq = qt.shape[1]
    nc, _, tk = vt_ref.shape[2:]
    m = jnp.full((1, tq), MASK_VALUE, F32)
    l = jnp.zeros((1, tq), F32)
    acc = jnp.zeros((MLA_V, tq), F32)
    for c in range(nc):
        st = jnp.dot(k_ref[0, c * tk:(c + 1) * tk, :], qt, preferred_element_type=F32)
        m_new = jnp.maximum(m, jnp.max(st, axis=0, keepdims=True))
        alpha = jnp.exp2(m - m_new)
        p = jnp.exp2(st - m_new)
        l = alpha * l + jnp.sum(p, axis=0, keepdims=True)
        acc = alpha * acc + jnp.dot(vt_ref[0, 0, c], p.astype(BF16), preferred_element_type=F32)
        m = m_new
    o_ref[0] = (acc * (1.0 / l)).T.astype(o_ref.dtype)


def _flash(qt, k, vt, *, tq):
    b, _, lq = qt.shape
    lk = k.shape[1]
    nc, _, tk = vt.shape[2:]
    return pl.pallas_call(
        _flash_kernel,
        grid=(b, MLA_HEADS, lq // tq),
        in_specs=[pl.BlockSpec((1, MLA_PAD, tq), lambda bi, h, i: (bi, h, i)),
                  pl.BlockSpec((1, lk, MLA_PAD), lambda bi, h, i: (bi, 0, h)),
                  pl.BlockSpec((1, 1, nc, MLA_V, tk), lambda bi, h, i: (bi, h, 0, 0, 0))],
        out_specs=pl.BlockSpec((1, tq, MLA_V), lambda bi, h, i: (bi, i, h)),
        out_shape=jax.ShapeDtypeStruct((b, lq, MLA_HEADS * MLA_V), BF16),
        compiler_params=_cp("parallel", "parallel", "arbitrary"),
        name="mla_flash",
    )(qt, k, vt)


def _sort_network(n):
    pairs, p = [], 1
    while p < n:
        k = p
        while k >= 1:
            for j in range(k % p, n - k, 2 * k):
                for i in range(min(k, n - j - k)):
                    if (i + j) // (2 * p) == (i + j + k) // (2 * p):
                        pairs.append((i + j, i + j + k))
            k //= 2
        p *= 2
    return pairs


SUBLANES = 8


def _stack_rows(rows):
    n = rows[0].shape[1]
    sub = lax.broadcasted_iota(jnp.int32, (SUBLANES, n), 0)
    out = jnp.broadcast_to(rows[0], (SUBLANES, n))
    for r in range(1, len(rows)):
        out = jnp.where(sub == r, rows[r], out)
    return out


def _column_top(st, k):
    groups = st.shape[0] // SUBLANES
    v = [st[SUBLANES * j:SUBLANES * (j + 1), :] for j in range(groups)]
    for i, j in _sort_network(groups):
        v[i], v[j] = jnp.maximum(v[i], v[j]), jnp.minimum(v[i], v[j])
    vals = []
    for t in range(k):
        mx = jnp.max(v[0], axis=0, keepdims=True)
        vals.append(mx)
        pop = v[0] >= mx
        for j in range(k - 1 - t):
            v[j] = jnp.where(pop, v[j + 1] if j + 1 < groups else MASK_VALUE, v[j])
    return vals


def _pair_top(a, b, k):
    lo = _stack_rows(a[:SUBLANES])
    v = [lo + b[j] for j in range(k)]
    hi = [_stack_rows(a[SUBLANES * g:SUBLANES * (g + 1)]) + b[0] for g in range(1, k // SUBLANES)]
    vals = []
    for t in range(k + 1):
        head = v[0]
        for hgrp in hi:
            head = jnp.maximum(head, hgrp)
        mx = jnp.max(head, axis=0, keepdims=True)
        vals.append(mx)
        pop = v[0] >= mx
        for j in range(min(k, k - t)):
            v[j] = jnp.where(pop, v[j + 1] if j + 1 < k else MASK_VALUE, v[j])
        hi = [jnp.where(hgrp >= mx, MASK_VALUE, hgrp) for hgrp in hi]
    return vals


def _peer_route_kernel(x_ref, g_ref, mod_ref, wq_ref, ks_ref, h_ref, r1_ref, e1_ref, cut_ref, e0_ref):
    h2 = _hmod(x_ref[0], g_ref[...], mod_ref[0], 3).astype(BF16)
    h_ref[...] = h2
    qp = jnp.dot(h2, wq_ref[...], preferred_element_type=F32).astype(BF16)
    k = PEER_TOPK
    for h in range(PEER_HEADS):
        s0 = lax.dot_general(ks_ref[2 * h], qp[:, (2 * h) * PEER_HALF:(2 * h + 1) * PEER_HALF], NT_DIMS,
                             preferred_element_type=F32)
        s1 = lax.dot_general(ks_ref[2 * h + 1], qp[:, (2 * h + 1) * PEER_HALF:(2 * h + 2) * PEER_HALF], NT_DIMS,
                             preferred_element_type=F32)
        a = _column_top(s0, k)
        bb = _column_top(s1, k)
        c = _pair_top(a, bb, k)
        z = jnp.zeros_like(c[0])
        for j in range(k):
            z = z + jnp.exp(c[j] - c[0])
        tau = 0.5 * (c[k - 1] + c[k])
        theta = tau - s0
        rank1 = jnp.full(s1.shape, float(k), F32)
        cut = jnp.zeros(s0.shape, F32)
        for j in range(k):
            rank1 = jnp.where(s1 >= bb[k - 1 - j], float(k - 1 - j), rank1)
            cut = jnp.where(bb[j] >= theta, float(j + 1), cut)
        r1_ref[h] = rank1.astype(r1_ref.dtype)
        e1_ref[h] = jnp.exp(s1 - bb[0]).astype(e1_ref.dtype)
        cut_ref[h] = jnp.where(s0 >= a[k - 1], cut, 0.0)
        e0_ref[h] = jnp.exp(s0 - a[0]) * (1.0 / z)


def _peer_route(x, g, mod, wq, kt, *, tm):
    b, t, d = x.shape
    nt = t // tm
    rt = lambda: pl.BlockSpec((PEER_HEADS, PEER_KEYS, tm), lambda bi, i: (0, 0, bi * nt + i))
    rshape = lambda dt: jax.ShapeDtypeStruct((PEER_HEADS, PEER_KEYS, b * t), dt)
    return pl.pallas_call(
        _peer_route_kernel,
        grid=(b, nt),
        in_specs=[pl.BlockSpec((1, tm, d), lambda bi, i: (bi, i, 0)),
                  pl.BlockSpec((1, d), lambda bi, i: (0, 0)),
                  pl.BlockSpec((1, 6, d), lambda bi, i: (bi, 0, 0)),
                  pl.BlockSpec(wq.shape, lambda bi, i: (0, 0)),
                  pl.BlockSpec(kt.shape, lambda bi, i: (0, 0, 0))],
        out_specs=[pl.BlockSpec((tm, d), lambda bi, i: (bi * nt + i, 0)), rt(), rt(), rt(), rt()],
        out_shape=[jax.ShapeDtypeStruct((b * t, d), BF16), rshape(BF16), rshape(BF16), rshape(F32), rshape(F32)],
        compiler_params=_cp("parallel", "parallel"),
        name="peer_route",
    )(x, g.reshape(1, d), mod, wq, kt)


def _gelu_tanh(a):
    return 0.5 * a * (1.0 + jnp.tanh(math.sqrt(2.0 / math.pi) * (a + 0.044715 * (a * a * a))))


def _peer_expert_kernel(*refs, final):
    if final:
        (h_ref, u0_ref, un_ref, vt_ref, r1_ref, e1_ref, cut_ref, e0_ref, x_ref, mod_ref, fn_ref, o_ref,
         acc_scr, at_scr) = refs
    else:
        (h_ref, u0_ref, un_ref, vt_ref, r1_ref, e1_ref, cut_ref, e0_ref, x_ref, mod_ref, o_ref,
         acc_scr, at_scr) = refs
    j = pl.program_id(2)

    def pre_activation(u_ref):
        return lax.dot_general(u_ref[...], h_ref[...], NT_DIMS, preferred_element_type=F32).astype(BF16)

    @pl.when(j == 0)
    def _():
        acc_scr[...] = jnp.zeros_like(acc_scr)
        at_scr[0] = pre_activation(u0_ref)

    def body(cur, nxt):
        at_scr[nxt] = pre_activation(un_ref)
        ws = []
        for il in range(cut_ref.shape[1]):
            gate = None
            for h in range(PEER_HEADS):
                cut = cut_ref[h, il:il + 1, :].astype(BF16)
                e0 = e0_ref[h, il:il + 1, :].astype(BF16)
                term = jnp.where(r1_ref[h] < cut, e0 * e1_ref[h], jnp.zeros((), BF16))
                gate = term if gate is None else gate + term
            ws.append(gate * _gelu_tanh(at_scr[cur, il * PEER_KEYS:(il + 1) * PEER_KEYS, :]))
        acc_scr[...] += jnp.dot(vt_ref[...], jnp.concatenate(ws, axis=0), preferred_element_type=F32)

    @pl.when(j % 2 == 0)
    def _():
        body(0, 1)

    @pl.when(j % 2 == 1)
    def _():
        body(1, 0)

    @pl.when(j == pl.num_programs(2) - 1)
    def _():
        y = x_ref[0] + mod_ref[0, 5:6, :] * acc_scr[...].T
        if final:
            y = _rms_rows(y, fn_ref[...])
        o_ref[0] = y


def _peer_expert(h2, route, u, vt, x, mod, final_gain, *, tt, ec):
    b, t, d = x.shape
    nt = t // tt
    ne = u.shape[0] // ec
    r1, e1, cut, e0 = route
    once = dict(pipeline_mode=pl.Buffered(1))
    rt = lambda: pl.BlockSpec((PEER_HEADS, PEER_KEYS, tt), lambda bi, i, j: (0, 0, bi * nt + i), **once)
    rj = lambda: pl.BlockSpec((PEER_HEADS, ec // PEER_KEYS, tt), lambda bi, i, j: (0, j, bi * nt + i))
    in_specs = [pl.BlockSpec((tt, d), lambda bi, i, j: (bi * nt + i, 0), **once),
                pl.BlockSpec((ec, d), lambda bi, i, j: (0, 0), **once),
                pl.BlockSpec((ec, d), lambda bi, i, j: (jnp.minimum(j + 1, ne - 1), 0)),
                pl.BlockSpec((d, ec), lambda bi, i, j: (0, j)),
                rt(), rt(), rj(), rj(),
                pl.BlockSpec((1, tt, d), lambda bi, i, j: (bi, i, 0), **once),
                pl.BlockSpec((1, 6, d), lambda bi, i, j: (bi, 0, 0))]
    args = [h2, u, u, vt, r1, e1, cut, e0, x, mod]
    if final_gain is not None:
        in_specs.append(pl.BlockSpec((1, d), lambda bi, i, j: (0, 0)))
        args.append(final_gain.reshape(1, d))
    return pl.pallas_call(
        functools.partial(_peer_expert_kernel, final=final_gain is not None),
        grid=(b, nt, ne),
        in_specs=in_specs,
        out_specs=pl.BlockSpec((1, tt, d), lambda bi, i, j: (bi, i, 0)),
        out_shape=jax.ShapeDtypeStruct((b, t, d), F32),
        scratch_shapes=[pltpu.VMEM((d, tt), F32), pltpu.VMEM((2, ec, tt), BF16)],
        compiler_params=_cp("parallel", "parallel", "arbitrary"),
        name="peer_expert",
    )(*args)


def _peer(x, g, mod, wq, kt, u, vt, final_gain, *, tm, tt, ec):
    out = _peer_route(x, g, mod, wq, kt, tm=tm)
    return _peer_expert(out[0], out[1:], u, vt, x, mod, final_gain, tt=tt, ec=ec)


def _lat_tile(t, want):
    while t % want:
        want //= 2
    return want


def kernel(x, c, ctx, c_ctx, w_mod, b_mod, norm1, norm2, even_w_in, even_ret_gn, even_w_out, odd_w_in, odd_sink, odd_mla_q_norm, odd_mla_kv_norm, odd_mla_w_uq, odd_mla_w_ukv, odd_w_out, peer_w_q, peer_sub_keys, peer_u, peer_v, final_norm):
    b, s_len, d = x.shape
    n_ctx = ctx.shape[1]
    depth = w_mod.shape[0]
    assert depth == 2 and even_w_in.shape[0] == 1 and odd_w_in.shape[0] == 1
    tm_l = _lat_tile(s_len, 1024)
    tm_c = n_ctx

    rows = 8 * ((b + 1 + 7) // 8)
    cond = jnp.zeros((rows, d), F32).at[:b].set(c).at[b].set(c_ctx)
    mods = _adaln(cond, w_mod, b_mod).reshape(depth, rows, 6, d)
    mod_l = [mods[l, :b] for l in range(depth)]
    mod_c = [jnp.broadcast_to(mods[l, b:b + 1], (b, 6, d)) for l in range(depth)]

    x_lat, x_ctx = x, ctx

    def peer_params(layer):
        kt = peer_sub_keys[layer].reshape(2 * PEER_HEADS, PEER_KEYS, PEER_HALF).astype(BF16)
        return (peer_w_q[layer].astype(BF16), kt, peer_u[layer].astype(BF16), peer_v[layer].T.astype(BF16))

    w_in = even_w_in[0].astype(BF16)
    w_out = even_w_out[0].astype(BF16)
    p_l = _proj(x_lat, norm1[0], mod_l[0], w_in, si=0, tm=tm_l, tn=512, name="even_proj_lat")
    p_c = _proj(x_ctx, norm1[0], mod_c[0], w_in, si=0, tm=tm_c, tn=512, name="even_proj_ctx")
    fm_l = _fourier_lat(p_l[..., :FOURIER_WIDTH])
    fm_c = _fourier_small(p_c[..., :FOURIER_WIDTH])
    tabs_l = _rope_tables(s_len, RET_DIM, LANES)
    tabs_c = _rope_tables(n_ctx, RET_DIM, LANES, identity=True)
    zero = jnp.zeros((b, RET_HEADS, RET_DIM, RET_DIM), F32)
    gn = even_ret_gn[0]
    rf_c, st_f = _retention(p_c, tabs_c, gn, zero, None, reverse=False, gate_blk=4)
    r_c, st_b = _retention(p_c, tabs_c, gn, zero, rf_c, reverse=True, gate_blk=5)
    rf_l, _ = _retention(p_l, tabs_l, gn, st_f, None, reverse=False, gate_blk=4)
    r_l, _ = _retention(p_l, tabs_l, gn, st_b, rf_l, reverse=True, gate_blk=5)
    x_lat = _outproj(fm_l, r_l, w_out, x_lat, mod_l[0], tm=tm_l, tn=512, name="even_out_lat")
    x_ctx = _outproj(fm_c, r_c, w_out, x_ctx, mod_c[0], tm=tm_c, tn=512, name="even_out_ctx")
    pp = peer_params(0)
    tt_l = _lat_tile(s_len, 512)
    x_lat = _peer(x_lat, norm2[0], mod_l[0], *pp, None, tm=256, tt=tt_l, ec=1024)
    x_ctx = _peer(x_ctx, norm2[0], mod_c[0], *pp, None, tm=256, tt=n_ctx, ec=1024)

    wi = odd_w_in[0]
    o_sq, o_sk, o_sv = 0, SWA_Q_HEADS * SWA_HEAD_DIM, (SWA_Q_HEADS + SWA_KV_HEADS) * SWA_HEAD_DIM
    o_cq = o_sv + SWA_KV_HEADS * SWA_HEAD_DIM
    o_ckv = o_cq + MLA_Q_RANK
    o_kr = o_ckv + MLA_KV_RANK
    w_in = jnp.concatenate([wi[:, o_sq:o_sk], wi[:, o_cq:o_ckv], wi[:, o_ckv:o_kr], wi[:, o_sk:o_sv], wi[:, o_sv:o_cq],
                            wi[:, o_kr:], jnp.zeros((d, LANES - MLA_ROPE), wi.dtype)], axis=1).astype(BF16)
    n_odd = w_in.shape[1]
    q_blk, cq_blk, ckv_blk = 0, (o_sk - o_sq) // MLA_Q_RANK, (o_sk - o_sq + MLA_Q_RANK) // MLA_KV_RANK
    k_blk = (o_sk - o_sq + MLA_Q_RANK + MLA_KV_RANK) // LANES
    v_blk, kr_blk = k_blk + 1, k_blk + 2
    tm_o = _lat_tile(s_len, 512)
    p_l = _proj(x_lat, norm1[1], mod_l[1], w_in, si=0, tm=tm_o, tn=n_odd, name="odd_proj_lat")
    p_c = _proj(x_ctx, norm1[1], mod_c[1], w_in, si=0, tm=tm_c, tn=n_odd, name="odd_proj_ctx")
    tabs_swa = _rope_tables(s_len, SWA_HEAD_DIM, LANES)
    swa = _swa(p_l, p_c, tabs_swa, odd_sink[0], q_blk=q_blk, k_blk=k_blk, v_blk=v_blk)
    wuq = odd_mla_w_uq[0].reshape(MLA_Q_RANK, MLA_HEADS, MLA_QK_DIM)
    wuq = jnp.pad(wuq, ((0, 0), (0, 0), (0, MLA_PAD - MLA_QK_DIM))).reshape(MLA_Q_RANK, MLA_HEADS * MLA_PAD)
    wukv = odd_mla_w_ukv[0].reshape(MLA_KV_RANK, MLA_HEADS, MLA_NOPE + MLA_V)
    wk = wukv[:, :, :MLA_NOPE].reshape(MLA_KV_RANK, -1).astype(BF16)
    wvt = wukv[:, :, MLA_NOPE:].reshape(MLA_KV_RANK, -1).T.astype(BF16)
    tabs_mla = _pad_rope_tables(_rope_tables(s_len, MLA_ROPE, MLA_ROPE), LANES)
    tabs_id = _rope_tables(n_ctx, MLA_ROPE, LANES, identity=True)
    qt = _mla_q(p_l, odd_mla_q_norm[0], wuq.T.astype(BF16), tuple(a.T for a in tabs_mla), x_blk=cq_blk, tm=tm_o)
    k_l, vt_l = _mla_kv(p_l, odd_mla_kv_norm[0], wk, wvt, tabs_mla, x_blk=ckv_blk, kr_blk=kr_blk, tm=tm_o)
    k_c, vt_c = _mla_kv(p_c, odd_mla_kv_norm[0], wk, wvt, tabs_id, x_blk=ckv_blk, kr_blk=kr_blk, tm=tm_c)
    k_all = jnp.concatenate([k_c, k_l], axis=1)
    lk = n_ctx + s_len
    tk = max(t for t in (1280, 1024, 768, 512, 256, 128) if lk % t == 0)
    vt_all = jnp.concatenate([vt_c, vt_l], axis=2).reshape(b, MLA_HEADS, MLA_V, lk // tk, tk)
    vt_all = jnp.transpose(vt_all, (0, 1, 3, 2, 4))
    mla = _flash(qt, k_all, vt_all, tq=_lat_tile(s_len, 512))
    x_lat = _outproj(swa, mla, odd_w_out[0].astype(BF16), x_lat, mod_l[1], tm=tm_l, tn=512, name="odd_out_lat")
    pp = peer_params(1)
    return _peer(x_lat, norm2[1], mod_l[1], *pp, final_norm, tm=256, tt=tt_l, ec=1024)
```

```python
import functools
import math

import numpy as np
import jax
import jax.numpy as jnp
from jax import lax
from jax.experimental import pallas as pl
from jax.experimental.pallas import tpu as pltpu

F32 = jnp.float32
BF16 = jnp.bfloat16

GRID_W = 64
NORM_EPS = 1e-6
ROPE_BASE = 10000.0
MASK_VALUE = -1e30
LANES = 128

FOURIER_GROUPS = 4
FOURIER_GROUP_DIM = 256
FOURIER_WIDTH = FOURIER_GROUPS * FOURIER_GROUP_DIM
RET_HEADS = 8
RET_DIM = 128
RET_CHUNK = 128
RET_WIDTH = RET_HEADS * RET_DIM

SWA_Q_HEADS = 16
SWA_KV_HEADS = 2
SWA_GROUP = SWA_Q_HEADS // SWA_KV_HEADS
SWA_HEAD_DIM = 64
SWA_WINDOW = 128
SWA_BLOCK = 128
MLA_HEADS = 8
MLA_Q_RANK = 512
MLA_KV_RANK = 256
MLA_NOPE = 128
MLA_ROPE = 64
MLA_V = 128
MLA_QK_DIM = MLA_NOPE + MLA_ROPE
MLA_PAD = 2 * LANES

PEER_HEADS = 8
PEER_KEYS = 128
PEER_TOPK = 16
PEER_HALF = 128
EXPERT_CHUNK = 1024

VMEM_LIMIT = 56 * 1024 * 1024


def _cp(*sem):
    return pltpu.CompilerParams(dimension_semantics=sem, vmem_limit_bytes=VMEM_LIMIT)


def _adaln_kernel(c_ref, w_ref, b_ref, o_ref):
    c = c_ref[...]
    s = c * jax.nn.sigmoid(c)
    o_ref[0] = jnp.dot(s, w_ref[0], preferred_element_type=F32) + b_ref[0]


def _adaln(cond, w_mod, b_mod):
    depth, d, n = w_mod.shape
    r = cond.shape[0]
    tn = 1024
    return pl.pallas_call(
        _adaln_kernel,
        grid=(depth, n // tn),
        in_specs=[pl.BlockSpec((r, d), lambda l, j: (0, 0)),
                  pl.BlockSpec((1, d, tn), lambda l, j: (l, 0, j)),
                  pl.BlockSpec((1, 1, tn), lambda l, j: (l, 0, j))],
        out_specs=pl.BlockSpec((1, r, tn), lambda l, j: (l, 0, j)),
        out_shape=jax.ShapeDtypeStruct((depth, r, n), F32),
        compiler_params=_cp("parallel", "parallel"),
        name="adaln",
    )(cond, w_mod, b_mod.reshape(depth, 1, n))


def _hmod(x, g, mod, si):
    ms = jnp.mean(x * x, axis=-1, keepdims=True)
    y = x * lax.rsqrt(ms + NORM_EPS) * g
    return y * (1.0 + mod[si + 1:si + 2, :]) + mod[si:si + 1, :]


def _proj_kernel(x_ref, g_ref, mod_ref, w_ref, o_ref, h_scr, *, si):
    @pl.when(pl.program_id(2) == 0)
    def _():
        h_scr[...] = _hmod(x_ref[0], g_ref[...], mod_ref[0], si).astype(BF16)

    o_ref[0] = jnp.dot(h_scr[...], w_ref[...], preferred_element_type=F32).astype(o_ref.dtype)


def _proj(x, g, mod, w, *, si, tm, tn, name):
    b, t, d = x.shape
    n = w.shape[1]
    return pl.pallas_call(
        functools.partial(_proj_kernel, si=si),
        grid=(b, t // tm, n // tn),
        in_specs=[pl.BlockSpec((1, tm, d), lambda bi, i, j: (bi, i, 0)),
                  pl.BlockSpec((1, d), lambda bi, i, j: (0, 0)),
                  pl.BlockSpec((1, 6, d), lambda bi, i, j: (bi, 0, 0)),
                  pl.BlockSpec((d, tn), lambda bi, i, j: (0, j))],
        out_specs=pl.BlockSpec((1, tm, tn), lambda bi, i, j: (bi, i, j)),
        out_shape=jax.ShapeDtypeStruct((b, t, n), BF16),
        scratch_shapes=[pltpu.VMEM((tm, d), BF16)],
        compiler_params=_cp("parallel", "parallel", "arbitrary"),
        name=name,
    )(x, g.reshape(1, d), mod, w)


def _outproj_kernel(a_ref, b_ref, w_ref, x_ref, mod_ref, o_ref):
    ka = a_ref.shape[2]
    acc = jnp.dot(a_ref[0], w_ref[:ka, :], preferred_element_type=F32)
    acc = acc + jnp.dot(b_ref[0], w_ref[ka:, :], preferred_element_type=F32)
    o_ref[0] = x_ref[0] + mod_ref[0, 2:3, :] * acc


def _outproj(a, b2, w, x, mod, *, tm, tn, name):
    b, t, d = x.shape
    ka, kb = a.shape[2], b2.shape[2]
    return pl.pallas_call(
        _outproj_kernel,
        grid=(b, t // tm, d // tn),
        in_specs=[pl.BlockSpec((1, tm, ka), lambda bi, i, j: (bi, i, 0)),
                  pl.BlockSpec((1, tm, kb), lambda bi, i, j: (bi, i, 0)),
                  pl.BlockSpec((ka + kb, tn), lambda bi, i, j: (0, j)),
                  pl.BlockSpec((1, tm, tn), lambda bi, i, j: (bi, i, j)),
                  pl.BlockSpec((1, 6, tn), lambda bi, i, j: (bi, 0, j))],
        out_specs=pl.BlockSpec((1, tm, tn), lambda bi, i, j: (bi, i, j)),
        out_shape=jax.ShapeDtypeStruct((b, t, d), F32),
        compiler_params=_cp("parallel", "parallel", "arbitrary"),
        name=name,
    )(a, b2, w, x, mod)


def _rope_tables(n_pos, d_rot, width, identity=False):
    q = d_rot // 4
    if identity:
        return (jnp.ones((n_pos, width), F32), jnp.zeros((n_pos, width), F32), jnp.zeros((n_pos, width), F32))
    d_axis = d_rot // 2
    inv = ROPE_BASE ** (-jnp.arange(0, d_axis, 2, dtype=F32) / d_axis)
    t = jnp.arange(n_pos)
    row = (t // GRID_W).astype(F32)
    col = (t % GRID_W).astype(F32)
    ar = row[:, None] * inv[None, :]
    ac = col[:, None] * inv[None, :]
    ang = jnp.concatenate([ar, ar, ac, ac], axis=-1)
    cos, sin = jnp.cos(ang), jnp.sin(ang)
    quarter = np.arange(d_rot) // q
    up = jnp.asarray((quarter % 2 == 1).astype(np.float32))
    dn = jnp.asarray((quarter % 2 == 0).astype(np.float32))
    sin_p, sin_m = sin * up, -sin * dn
    reps = width // d_rot
    return tuple(jnp.tile(a, (1, reps)) for a in (cos, sin_p, sin_m))


def _pad_rope_tables(tabs, width):
    cos, sp, sm = tabs
    pad = width - cos.shape[1]
    return (jnp.pad(cos, ((0, 0), (0, pad)), constant_values=1.0), jnp.pad(sp, ((0, 0), (0, pad))),
            jnp.pad(sm, ((0, 0), (0, pad))))


def _rope(t, cos, sp, sm, q):
    w = t.shape[-1]
    return t * cos + pltpu.roll(t, q, 1) * sp + pltpu.roll(t, w - q, 1) * sm


def _dft_mats(n):
    k = np.arange(n)
    ang = 2.0 * np.pi * ((k[:, None] * k[None, :]) % n) / n
    return np.cos(ang), np.sin(ang)


def _fourier_a_kernel(u_ref, fa_ref, tc_ref, ts_ref, o_ref):
    l1 = u_ref.shape[1]
    t1 = jnp.dot(fa_ref[...], u_ref[0], preferred_element_type=F32)
    tr, ti = t1[:l1], t1[l1:]
    reps = tr.shape[1] // LANES
    c = jnp.concatenate([tc_ref[0]] * reps, axis=1)
    s = jnp.concatenate([ts_ref[0]] * reps, axis=1)
    o_ref[0, 0] = (tr * c + ti * s).astype(o_ref.dtype)
    o_ref[0, 1] = (ti * c - tr * s).astype(o_ref.dtype)


def _fourier_b_kernel(t_ref, fb_ref, cs_ref, o_ref):
    l2 = t_ref.shape[2]
    t2 = jnp.concatenate([t_ref[0, 0], t_ref[0, 1]], axis=0)
    v = jnp.dot(fb_ref[...], t2, preferred_element_type=F32).astype(BF16)
    vr, vi = v[:l2], v[l2:]
    outs = []
    for g in range(FOURIER_GROUPS):
        sl = slice(g * FOURIER_GROUP_DIM, (g + 1) * FOURIER_GROUP_DIM)
        vg = jnp.concatenate([vr[:, sl], vi[:, sl]], axis=1)
        outs.append(jnp.dot(vg, cs_ref[...], preferred_element_type=F32))
    o_ref[0] = jnp.concatenate(outs, axis=1).astype(o_ref.dtype)


def _fourier_lat(u):
    b, l, w = u.shape
    l1 = int(round(math.sqrt(l)))
    l2 = l // l1
    assert l1 * l2 == l and l1 % 16 == 0 and l2 % 16 == 0
    ca, sa = _dft_mats(l1)
    fa = jnp.asarray(np.concatenate([ca, -sa], axis=0), BF16)
    kk = np.arange(l1)[None, :, None] * np.arange(l2)[:, None, None]
    ang = 2.0 * np.pi * (kk % l) / l
    tc = jnp.asarray(np.broadcast_to(np.cos(ang), (l2, l1, LANES)), F32)
    ts = jnp.asarray(np.broadcast_to(np.sin(ang), (l2, l1, LANES)), F32)
    t2 = pl.pallas_call(
        _fourier_a_kernel,
        grid=(b, l2),
        in_specs=[pl.BlockSpec((1, l1, w), lambda bi, j: (bi, 0, j)),
                  pl.BlockSpec((2 * l1, l1), lambda bi, j: (0, 0)),
                  pl.BlockSpec((1, l1, LANES), lambda bi, j: (j, 0, 0)),
                  pl.BlockSpec((1, l1, LANES), lambda bi, j: (j, 0, 0))],
        out_specs=pl.BlockSpec((1, 2, l1, w), lambda bi, j: (bi, 0, 0, j)),
        out_shape=jax.ShapeDtypeStruct((b, 2, l1, l2 * w), BF16),
        compiler_params=_cp("parallel", "parallel"),
        name="fourier_a",
    )(u.reshape(b, l1, l2 * w), fa, tc, ts)
    cb, sb = _dft_mats(l2)
    scale = 1.0 / math.sqrt(l * FOURIER_GROUP_DIM)
    fb = jnp.asarray(np.block([[cb, sb], [-sb, cb]]) * scale, BF16)
    cc, sc = _dft_mats(FOURIER_GROUP_DIM)
    cs = jnp.asarray(np.concatenate([cc, sc], axis=0), BF16)
    y = pl.pallas_call(
        _fourier_b_kernel,
        grid=(b, l1),
        in_specs=[pl.BlockSpec((1, 2, l2, w), lambda bi, j: (bi, 0, j, 0)),
                  pl.BlockSpec((2 * l2, 2 * l2), lambda bi, j: (0, 0)),
                  pl.BlockSpec((2 * FOURIER_GROUP_DIM, FOURIER_GROUP_DIM), lambda bi, j: (0, 0))],
        out_specs=pl.BlockSpec((1, l2, w), lambda bi, j: (bi, 0, j)),
        out_shape=jax.ShapeDtypeStruct((b, l2, l1 * w), BF16),
        compiler_params=_cp("parallel", "parallel"),
        name="fourier_b",
    )(t2.reshape(b, 2, l1 * l2, w), fb, cs)
    return y.reshape(b, l, w)


def _fourier_small_kernel(u_ref, cl_ref, sl_ref, cs_ref, o_ref):
    gd = FOURIER_GROUP_DIM
    outs = []
    for g in range(FOURIER_GROUPS):
        ug = u_ref[0][:, g * gd:(g + 1) * gd]
        pq = jnp.dot(ug, cs_ref[...], preferred_element_type=F32).astype(BF16)
        y = jnp.dot(cl_ref[...], pq[:, :gd], preferred_element_type=F32)
        y = y - jnp.dot(sl_ref[...], pq[:, gd:], preferred_element_type=F32)
        outs.append(y)
    o_ref[0] = jnp.concatenate(outs, axis=1).astype(o_ref.dtype)


def _fourier_small(u):
    b, l, w = u.shape
    cl, sl = _dft_mats(l)
    scale = 1.0 / math.sqrt(l * FOURIER_GROUP_DIM)
    cc, sc = _dft_mats(FOURIER_GROUP_DIM)
    cs = jnp.asarray(np.concatenate([cc, sc], axis=1), BF16)
    return pl.pallas_call(
        _fourier_small_kernel,
        grid=(b,),
        in_specs=[pl.BlockSpec((1, l, w), lambda bi: (bi, 0, 0)),
                  pl.BlockSpec((l, l), lambda bi: (0, 0)),
                  pl.BlockSpec((l, l), lambda bi: (0, 0)),
                  pl.BlockSpec((FOURIER_GROUP_DIM, 2 * FOURIER_GROUP_DIM), lambda bi: (0, 0))],
        out_specs=pl.BlockSpec((1, l, w), lambda bi: (bi, 0, 0)),
        out_shape=jax.ShapeDtypeStruct((b, l, w), BF16),
        compiler_params=_cp("parallel"),
        name="fourier_small",
    )(u, jnp.asarray(cl * scale, BF16), jnp.asarray(sl * scale, BF16), cs)


def _ret_consts(reverse):
    c = RET_CHUNK
    lg = np.log(1.0 - np.exp2(-5.0 - np.arange(RET_HEADS, dtype=np.float32))).astype(np.float32)
    pos = np.arange(c, dtype=np.float32)
    diff = pos[:, None] - pos[None, :]
    if reverse:
        diff = -diff
    decay = np.where(diff >= 0, np.exp(lg[:, None, None] * np.maximum(diff, 0.0)), 0.0)
    if reverse:
        zeta = np.exp(lg[None, :] * pos[:, None])
        xi = np.exp(lg[None, :] * (c - pos[:, None]))
    else:
        zeta = np.exp(lg[None, :] * (c - 1.0 - pos[:, None]))
        xi = np.exp(lg[None, :] * (pos[:, None] + 1.0))
    wide = lambda a: np.repeat(a, RET_DIM, axis=1)
    chunk_decay = [float(v) for v in np.exp(lg * c)]
    return (jnp.asarray(decay, F32), jnp.asarray(wide(zeta), F32), jnp.asarray(wide(xi), F32), chunk_decay)


def _ret_kernel(*refs, chunk_decay, has_prev):
    if has_prev:
        (q_ref, k_ref, v_ref, g_ref, cos_ref, sp_ref, sm_ref, dec_ref, zeta_ref, xi_ref, gn_ref, s0_ref, prev_ref,
         o_ref, sf_ref, st_scr) = refs
    else:
        (q_ref, k_ref, v_ref, g_ref, cos_ref, sp_ref, sm_ref, dec_ref, zeta_ref, xi_ref, gn_ref, s0_ref,
         o_ref, sf_ref, st_scr) = refs
        prev_ref = None
    c = pl.program_id(1)

    @pl.when(c == 0)
    def _():
        st_scr[...] = s0_ref[0]

    cos, sp, sm = cos_ref[...], sp_ref[...], sm_ref[...]
    outs = []
    for h in range(RET_HEADS):
        sl = slice(h * RET_DIM, (h + 1) * RET_DIM)
        q = _rope(q_ref[0][:, sl].astype(F32), cos, sp, sm, RET_DIM // 4)
        k = _rope(k_ref[0][:, sl].astype(F32), cos, sp, sm, RET_DIM // 4) * (RET_DIM ** -0.5)
        v = v_ref[0][:, sl].astype(F32)
        kt = k.T.astype(BF16)
        scores = jnp.dot(q.astype(BF16), kt, preferred_element_type=F32) * dec_ref[h]
        inner = jnp.dot(scores.astype(BF16), v.astype(BF16), preferred_element_type=F32)
        state = st_scr[h]
        cross = jnp.dot((q * xi_ref[:, sl]).astype(BF16), state.astype(BF16), preferred_element_type=F32)
        o = inner + cross
        st_scr[h] = state * chunk_decay[h] + jnp.dot(kt, (v * zeta_ref[:, sl]).astype(BF16),
                                                      preferred_element_type=F32)
        y = o * lax.rsqrt(jnp.mean(o * o, axis=-1, keepdims=True) + NORM_EPS) * gn_ref[:, sl]
        gate = g_ref[0][:, sl].astype(F32)
        outs.append(y * (gate * jax.nn.sigmoid(gate)))
    res = jnp.concatenate(outs, axis=1)
    if prev_ref is not None:
        res = res + prev_ref[0].astype(F32)
    o_ref[0] = res.astype(o_ref.dtype)

    @pl.when(c == pl.num_programs(1) - 1)
    def _():
        sf_ref[0] = st_scr[...]


def _retention(proj, tabs, gn, state0, prev, *, reverse, gate_blk):
    b, t, _ = proj.shape
    w = RET_WIDTH
    n = t // RET_CHUNK
    decay, zeta, xi, chunk_decay = _ret_consts(reverse)
    pos = (lambda ci: n - 1 - ci) if reverse else (lambda ci: ci)
    blk = lambda col: pl.BlockSpec((1, RET_CHUNK, w), lambda bi, ci: (bi, pos(ci), col))
    tab = pl.BlockSpec((RET_CHUNK, LANES), lambda bi, ci: (pos(ci), 0))
    const2 = pl.BlockSpec((RET_CHUNK, w), lambda bi, ci: (0, 0))
    st = pl.BlockSpec((1, RET_HEADS, RET_DIM, RET_DIM), lambda bi, ci: (bi, 0, 0, 0))
    in_specs = [blk(1), blk(2), blk(3), blk(gate_blk), tab, tab, tab,
                pl.BlockSpec((RET_HEADS, RET_CHUNK, RET_CHUNK), lambda bi, ci: (0, 0, 0)), const2, const2,
                pl.BlockSpec((1, w), lambda bi, ci: (0, 0)), st]
    args = [proj, proj, proj, proj, *tabs, decay, zeta, xi, gn.reshape(1, w), state0]
    if prev is not None:
        in_specs.append(pl.BlockSpec((1, RET_CHUNK, w), lambda bi, ci: (bi, pos(ci), 0)))
        args.append(prev)
    return pl.pallas_call(
        functools.partial(_ret_kernel, chunk_decay=chunk_decay, has_prev=prev is not None),
        grid=(b, n),
        in_specs=in_specs,
        out_specs=[pl.BlockSpec((1, RET_CHUNK, w), lambda bi, ci: (bi, pos(ci), 0)), st],
        out_shape=[jax.ShapeDtypeStruct((b, t, w), BF16),
                   jax.ShapeDtypeStruct((b, RET_HEADS, RET_DIM, RET_DIM), F32)],
        scratch_shapes=[pltpu.VMEM((RET_HEADS, RET_DIM, RET_DIM), F32)],
        compiler_params=_cp("parallel", "arbitrary"),
        name="retention_bwd" if reverse else "retention_fwd",
    )(*args)


def _swa_kernel(q_ref, kc_ref, vc_ref, kl_ref, km_ref, kr_ref, vl_ref, vm_ref, vr_ref,
                cq_ref, spq_ref, smq_ref, cl_ref, spl_ref, sml_ref, cm_ref, spm_ref, smm_ref,
                cr_ref, spr_ref, smr_ref, sink_ref, o_ref, *, seq_len):
    i = pl.program_id(1)
    qd = SWA_HEAD_DIM // 4
    hd = SWA_HEAD_DIM
    cq, spq, smq = cq_ref[...], spq_ref[...], smq_ref[...]
    rk = lambda kref, c, sp, sm: _rope(kref[0].astype(F32), c[...], sp[...], sm[...], qd)
    k_all = jnp.concatenate([kc_ref[0].astype(F32), rk(kl_ref, cl_ref, spl_ref, sml_ref),
                             rk(km_ref, cm_ref, spm_ref, smm_ref), rk(kr_ref, cr_ref, spr_ref, smr_ref)], axis=0)
    v_all = jnp.concatenate([vc_ref[0], vl_ref[0], vm_ref[0], vr_ref[0]], axis=0).astype(F32)
    vt = v_all.T
    n_ctx = kc_ref.shape[1]
    n_keys = k_all.shape[0]
    n_loc = 3 * SWA_BLOCK
    gw = SWA_GROUP * SWA_BLOCK
    cidx = lax.broadcasted_iota(jnp.int32, (n_loc, gw), 0)
    r = lax.broadcasted_iota(jnp.int32, (n_loc, gw), 1) % SWA_BLOCK
    kpos = (i - 1) * SWA_BLOCK + cidx
    valid = (jnp.abs(SWA_BLOCK + r - cidx) <= SWA_WINDOW) & (kpos >= 0) & (kpos < seq_len)
    klane = lax.broadcasted_iota(jnp.int32, (n_keys, LANES), 1)
    vrow = lax.broadcasted_iota(jnp.int32, (LANES, n_keys), 0)
    qt_blk = [(_rope(q_ref[0][:, blk * LANES:(blk + 1) * LANES].astype(F32), cq, spq, smq, qd) * (hd ** -0.5)).T
              for blk in range(SWA_Q_HEADS // 2)]
    zeros = jnp.zeros((hd, gw), F32)
    outs = []
    for kv in range(SWA_KV_HEADS):
        pieces = []
        for h in range(kv * SWA_GROUP, (kv + 1) * SWA_GROUP):
            pieces.append(qt_blk[h // 2][(h % 2) * hd:(h % 2 + 1) * hd, :])
        qg = jnp.concatenate(pieces, axis=1)
        qg = jnp.concatenate([qg, zeros] if kv == 0 else [zeros, qg], axis=0).astype(BF16)
        in_kv = (klane < hd) if kv == 0 else (klane >= hd)
        st = jnp.dot(jnp.where(in_kv, k_all, 0.0).astype(BF16), qg, preferred_element_type=F32)
        st = jnp.concatenate([st[:n_ctx], jnp.where(valid, st[n_ctx:], MASK_VALUE)], axis=0)
        sk = sink_ref[kv]
        m = jnp.maximum(jnp.max(st, axis=0, keepdims=True), sk)
        e = jnp.exp(st - m)
        p = e * (1.0 / (jnp.sum(e, axis=0, keepdims=True) + jnp.exp(sk - m)))
        v_kv = jnp.where((vrow < hd) if kv == 0 else (vrow >= hd), vt, 0.0).astype(BF16)
        ot = jnp.dot(v_kv, p.astype(BF16), preferred_element_type=F32)[kv * hd:(kv + 1) * hd, :]
        stacked = jnp.concatenate([ot[:, g * SWA_BLOCK:(g + 1) * SWA_BLOCK] for g in range(SWA_GROUP)], axis=0)
        outs.append(stacked.T)
    o_ref[0] = jnp.concatenate(outs, axis=1).astype(o_ref.dtype)


def _swa(proj_l, proj_c, tabs, sink, *, q_blk, k_blk, v_blk):
    b, l, _ = proj_l.shape
    n_ctx = proj_c.shape[1]
    nb = l // SWA_BLOCK
    qw = SWA_Q_HEADS * SWA_HEAD_DIM
    left = lambda i: jnp.maximum(i - 1, 0)
    right = lambda i: jnp.minimum(i + 1, nb - 1)
    kvspec = lambda col, f: pl.BlockSpec((1, SWA_BLOCK, LANES), lambda bi, i: (bi, f(i), col))
    ctxspec = lambda col: pl.BlockSpec((1, n_ctx, LANES), lambda bi, i: (bi, 0, col))
    tab = lambda f: pl.BlockSpec((SWA_BLOCK, LANES), lambda bi, i: (f(i), 0))
    ident = lambda i: i
    in_specs = [pl.BlockSpec((1, SWA_BLOCK, qw), lambda bi, i: (bi, i, q_blk)),
                ctxspec(k_blk), ctxspec(v_blk),
                kvspec(k_blk, left), kvspec(k_blk, ident), kvspec(k_blk, right),
                kvspec(v_blk, left), kvspec(v_blk, ident), kvspec(v_blk, right)]
    args = [proj_l, proj_c, proj_c, proj_l, proj_l, proj_l, proj_l, proj_l, proj_l]
    for f in (ident, left, ident, right):
        in_specs += [tab(f)] * 3
        args += list(tabs)
    gw = SWA_GROUP * SWA_BLOCK
    in_specs.append(pl.BlockSpec((SWA_KV_HEADS, 1, gw), lambda bi, i: (0, 0, 0)))
    args.append(jnp.repeat(sink.reshape(SWA_KV_HEADS, SWA_GROUP).astype(F32), SWA_BLOCK, axis=1)
                .reshape(SWA_KV_HEADS, 1, gw))
    return pl.pallas_call(
        functools.partial(_swa_kernel, seq_len=l),
        grid=(b, nb),
        in_specs=in_specs,
        out_specs=pl.BlockSpec((1, SWA_BLOCK, qw), lambda bi, i: (bi, i, 0)),
        out_shape=jax.ShapeDtypeStruct((b, l, qw), BF16),
        compiler_params=_cp("parallel", "parallel"),
        name="swa",
    )(*args)


def _rms_rows(x, g):
    return x * lax.rsqrt(jnp.mean(x * x, axis=-1, keepdims=True) + NORM_EPS) * g


NT_DIMS = (((1,), (1,)), ((), ()))


def _mla_q_kernel(x_ref, g_ref, wt_ref, cos_ref, sp_ref, sm_ref, o_ref):
    x = _rms_rows(x_ref[0].astype(F32), g_ref[...]).astype(BF16)
    mqt = lax.dot_general(wt_ref[...], x, NT_DIMS, preferred_element_type=F32)
    cos, sp, sm = cos_ref[...], sp_ref[...], sm_ref[...]
    scale = MLA_QK_DIM ** -0.5 * math.log2(math.e)
    q = MLA_ROPE // 4
    for h in range(MLA_HEADS):
        o_ref[0, h * MLA_PAD:h * MLA_PAD + LANES, :] = (mqt[h * MLA_PAD:h * MLA_PAD + LANES] * scale).astype(o_ref.dtype)
        r = mqt[h * MLA_PAD + LANES:(h + 1) * MLA_PAD]
        r = r * cos + pltpu.roll(r, q, 0) * sp + pltpu.roll(r, LANES - q, 0) * sm
        o_ref[0, h * MLA_PAD + LANES:(h + 1) * MLA_PAD, :] = (r * scale).astype(o_ref.dtype)


def _mla_q(proj, g, wt, tabs_t, *, x_blk, tm):
    b, t, _ = proj.shape
    n, kq = wt.shape
    tab = pl.BlockSpec((LANES, tm), lambda bi, i: (0, i))
    return pl.pallas_call(
        _mla_q_kernel,
        grid=(b, t // tm),
        in_specs=[pl.BlockSpec((1, tm, kq), lambda bi, i: (bi, i, x_blk)),
                  pl.BlockSpec((1, kq), lambda bi, i: (0, 0)),
                  pl.BlockSpec((n, kq), lambda bi, i: (0, 0)), tab, tab, tab],
        out_specs=pl.BlockSpec((1, n, tm), lambda bi, i: (bi, 0, i)),
        out_shape=jax.ShapeDtypeStruct((b, n, t), BF16),
        compiler_params=_cp("parallel", "parallel"),
        name="mla_q",
    )(proj, g.reshape(1, kq), wt, *tabs_t)


def _mla_kv_kernel(x_ref, kr_ref, g_ref, wk_ref, wvt_ref, cos_ref, sp_ref, sm_ref, k_ref, vt_ref):
    x = _rms_rows(x_ref[0].astype(F32), g_ref[...]).astype(BF16)
    kn = jnp.dot(x, wk_ref[...], preferred_element_type=F32)
    kr = _rope(kr_ref[0].astype(F32), cos_ref[...], sp_ref[...], sm_ref[...], MLA_ROPE // 4)
    for h in range(MLA_HEADS):
        k_ref[0, h] = jnp.concatenate([kn[:, h * MLA_NOPE:(h + 1) * MLA_NOPE], kr], axis=1).astype(k_ref.dtype)
    vt_ref[0] = lax.dot_general(wvt_ref[...], x, NT_DIMS, preferred_element_type=F32).astype(vt_ref.dtype)


def _mla_kv(proj, g, wk, wvt, tabs, *, x_blk, kr_blk, tm):
    b, t, _ = proj.shape
    kk = wk.shape[0]
    tab = pl.BlockSpec((tm, LANES), lambda bi, i: (i, 0))
    return pl.pallas_call(
        _mla_kv_kernel,
        grid=(b, t // tm),
        in_specs=[pl.BlockSpec((1, tm, kk), lambda bi, i: (bi, i, x_blk)),
                  pl.BlockSpec((1, tm, LANES), lambda bi, i: (bi, i, kr_blk)),
                  pl.BlockSpec((1, kk), lambda bi, i: (0, 0)),
                  pl.BlockSpec(wk.shape, lambda bi, i: (0, 0)),
                  pl.BlockSpec(wvt.shape, lambda bi, i: (0, 0)), tab, tab, tab],
        out_specs=[pl.BlockSpec((1, MLA_HEADS, tm, MLA_PAD), lambda bi, i: (bi, 0, i, 0)),
                   pl.BlockSpec((1, MLA_HEADS * MLA_V, tm), lambda bi, i: (bi, 0, i))],
        out_shape=[jax.ShapeDtypeStruct((b, MLA_HEADS, t, MLA_PAD), BF16),
                   jax.ShapeDtypeStruct((b, MLA_HEADS * MLA_V, t), BF16)],
        compiler_params=_cp("parallel", "parallel"),
        name="mla_kv",
    )(proj, proj, g.reshape(1, kk), wk, wvt, *tabs)


def _flash_kernel(qt_ref, k_ref, vt_ref, o_ref, s_scr, p_scr):
    qt = qt_ref[0]
    tq = qt.shape[1]
    nc, _, tk = vt_ref.shape[2:]
    m = jnp.full((1, tq), MASK_VALUE, F32)
    l = jnp.zeros((1, tq), F32)
    acc = jnp.zeros((MLA_V, tq), F32)
    s_scr[0] = jnp.dot(k_ref[0, 0, 0:tk, :], qt, preferred_element_type=F32)
    for c in range(nc + 1):
        if c + 1 < nc:
            s_scr[(c + 1) % 2] = jnp.dot(k_ref[0, 0, (c + 1) * tk:(c + 2) * tk, :], qt, preferred_element_type=F32)
        if c >= 1:
            acc = acc + jnp.dot(vt_ref[0, 0, c - 1], p_scr[(c - 1) % 2], preferred_element_type=F32)
        if c < nc:
            st = s_scr[c % 2]
            m_new = jnp.maximum(m, jnp.max(st, axis=0, keepdims=True))
            alpha = jnp.exp2(m - m_new)
            p = jnp.exp2(st - m_new)
            l = alpha * l + jnp.sum(p, axis=0, keepdims=True)
            p_scr[c % 2] = p.astype(BF16)
            acc = alpha * acc
            m = m_new
    o_ref[0] = (acc * (1.0 / l)).T.astype(o_ref.dtype)


def _flash(qt, k, vt, *, tq):
    b, _, lq = qt.shape
    lk = k.shape[2]
    nc, _, tk = vt.shape[2:]
    return pl.pallas_call(
        _flash_kernel,
        grid=(b, MLA_HEADS, lq // tq),
        in_specs=[pl.BlockSpec((1, MLA_PAD, tq), lambda bi, h, i: (bi, h, i)),
                  pl.BlockSpec((1, 1, lk, MLA_PAD), lambda bi, h, i: (bi, h, 0, 0)),
                  pl.BlockSpec((1, 1, nc, MLA_V, tk), lambda bi, h, i: (bi, h, 0, 0, 0))],
        out_specs=pl.BlockSpec((1, tq, MLA_V), lambda bi, h, i: (bi, i, h)),
        out_shape=jax.ShapeDtypeStruct((b, lq, MLA_HEADS * MLA_V), BF16),
        scratch_shapes=[pltpu.VMEM((2, tk, tq), F32), pltpu.VMEM((2, tk, tq), BF16)],
        compiler_params=_cp("parallel", "parallel", "arbitrary"),
        name="mla_flash",
    )(qt, k, vt)


def _sort_network(n):
    pairs, p = [], 1
    while p < n:
        k = p
        while k >= 1:
            for j in range(k % p, n - k, 2 * k):
                for i in range(min(k, n - j - k)):
                    if (i + j) // (2 * p) == (i + j + k) // (2 * p):
                        pairs.append((i + j, i + j + k))
            k //= 2
        p *= 2
    return pairs


SUBLANES = 8


def _stack_rows(rows):
    n = rows[0].shape[1]
    sub = lax.broadcasted_iota(jnp.int32, (SUBLANES, n), 0)
    out = jnp.broadcast_to(rows[0], (SUBLANES, n))
    for r in range(1, len(rows)):
        out = jnp.where(sub == r, rows[r], out)
    return out


def _column_top(st, k):
    groups = st.shape[0] // SUBLANES
    v = [st[SUBLANES * j:SUBLANES * (j + 1), :] for j in range(groups)]
    for i, j in _sort_network(groups):
        v[i], v[j] = jnp.maximum(v[i], v[j]), jnp.minimum(v[i], v[j])
    vals = []
    for t in range(k):
        mx = jnp.max(v[0], axis=0, keepdims=True)
        vals.append(mx)
        pop = v[0] >= mx
        for j in range(k - 1 - t):
            v[j] = jnp.where(pop, v[j + 1] if j + 1 < groups else MASK_VALUE, v[j])
    return vals


def _pair_top(a, b, k):
    lo = _stack_rows(a[:SUBLANES])
    v = [lo + b[j] for j in range(k)]
    hi = [_stack_rows(a[SUBLANES * g:SUBLANES * (g + 1)]) + b[0] for g in range(1, k // SUBLANES)]
    vals = []
    for t in range(k + 1):
        head = v[0]
        for hgrp in hi:
            head = jnp.maximum(head, hgrp)
        mx = jnp.max(head, axis=0, keepdims=True)
        vals.append(mx)
        pop = v[0] >= mx
        for j in range(min(k, k - t)):
            v[j] = jnp.where(pop, v[j + 1] if j + 1 < k else MASK_VALUE, v[j])
        hi = [jnp.where(hgrp >= mx, MASK_VALUE, hgrp) for hgrp in hi]
    return vals


def _peer_route_kernel(x_ref, g_ref, mod_ref, wq_ref, ks_ref, h_ref, r1_ref, e1_ref, cut_ref, e0_ref):
    h2 = _hmod(x_ref[0], g_ref[...], mod_ref[0], 3).astype(BF16)
    h_ref[...] = h2
    qp = jnp.dot(h2, wq_ref[...], preferred_element_type=F32).astype(BF16)
    k = PEER_TOPK
    for h in range(PEER_HEADS):
        s0 = lax.dot_general(ks_ref[2 * h], qp[:, (2 * h) * PEER_HALF:(2 * h + 1) * PEER_HALF], NT_DIMS,
                             preferred_element_type=F32)
        s1 = lax.dot_general(ks_ref[2 * h + 1], qp[:, (2 * h + 1) * PEER_HALF:(2 * h + 2) * PEER_HALF], NT_DIMS,
                             preferred_element_type=F32)
        a = _column_top(s0, k)
        bb = _column_top(s1, k)
        c = _pair_top(a, bb, k)
        z = jnp.zeros_like(c[0])
        for j in range(k):
            z = z + jnp.exp(c[j] - c[0])
        tau = 0.5 * (c[k - 1] + c[k])
        theta = tau - s0
        rank1 = jnp.full(s1.shape, float(k), F32)
        cut = jnp.zeros(s0.shape, F32)
        for j in range(k):
            rank1 = jnp.where(s1 >= bb[k - 1 - j], float(k - 1 - j), rank1)
            cut = jnp.where(bb[j] >= theta, float(j + 1), cut)
        r1_ref[h] = rank1.astype(r1_ref.dtype)
        e1_ref[h] = jnp.exp(s1 - bb[0]).astype(e1_ref.dtype)
        cut_ref[h] = jnp.where(s0 >= a[k - 1], cut, 0.0)
        e0_ref[h] = jnp.exp(s0 - a[0]) * (1.0 / z)


def _peer_route(x, g, mod, wq, kt, *, tm):
    b, t, d = x.shape
    nt = t // tm
    rt = lambda: pl.BlockSpec((PEER_HEADS, PEER_KEYS, tm), lambda bi, i: (0, 0, bi * nt + i))
    rshape = lambda dt: jax.ShapeDtypeStruct((PEER_HEADS, PEER_KEYS, b * t), dt)
    return pl.pallas_call(
        _peer_route_kernel,
        grid=(b, nt),
        in_specs=[pl.BlockSpec((1, tm, d), lambda bi, i: (bi, i, 0)),
                  pl.BlockSpec((1, d), lambda bi, i: (0, 0)),
                  pl.BlockSpec((1, 6, d), lambda bi, i: (bi, 0, 0)),
                  pl.BlockSpec(wq.shape, lambda bi, i: (0, 0)),
                  pl.BlockSpec(kt.shape, lambda bi, i: (0, 0, 0))],
        out_specs=[pl.BlockSpec((tm, d), lambda bi, i: (bi * nt + i, 0)), rt(), rt(), rt(), rt()],
        out_shape=[jax.ShapeDtypeStruct((b * t, d), BF16), rshape(BF16), rshape(BF16), rshape(F32), rshape(F32)],
        compiler_params=_cp("parallel", "parallel"),
        name="peer_route",
    )(x, g.reshape(1, d), mod, wq, kt)


def _gelu_tanh(a):
    return 0.5 * a * (1.0 + jnp.tanh(math.sqrt(2.0 / math.pi) * (a + 0.044715 * (a * a * a))))


def _peer_expert_kernel(*refs, final):
    if final:
        (h_ref, u0_ref, un_ref, vt_ref, r1_ref, e1_ref, cut_ref, e0_ref, x_ref, mod_ref, fn_ref, o_ref,
         acc_scr, at_scr) = refs
    else:
        (h_ref, u0_ref, un_ref, vt_ref, r1_ref, e1_ref, cut_ref, e0_ref, x_ref, mod_ref, o_ref,
         acc_scr, at_scr) = refs
    j = pl.program_id(2)

    def pre_activation(u_ref):
        return lax.dot_general(u_ref[...], h_ref[...], NT_DIMS, preferred_element_type=F32).astype(BF16)

    @pl.when(j == 0)
    def _():
        acc_scr[...] = jnp.zeros_like(acc_scr)
        at_scr[0] = pre_activation(u0_ref)

    def body(cur, nxt):
        at_scr[nxt] = pre_activation(un_ref)
        ws = []
        for il in range(cut_ref.shape[1]):
            gate = None
            for h in range(PEER_HEADS):
                cut = cut_ref[h, il:il + 1, :].astype(BF16)
                e0 = e0_ref[h, il:il + 1, :].astype(BF16)
                term = jnp.where(r1_ref[h] < cut, e0 * e1_ref[h], jnp.zeros((), BF16))
                gate = term if gate is None else gate + term
            ws.append(gate * _gelu_tanh(at_scr[cur, il * PEER_KEYS:(il + 1) * PEER_KEYS, :]))
        acc_scr[...] += jnp.dot(vt_ref[0], jnp.concatenate(ws, axis=0), preferred_element_type=F32)

    @pl.when(j % 2 == 0)
    def _():
        body(0, 1)

    @pl.when(j % 2 == 1)
    def _():
        body(1, 0)

    @pl.when(j == pl.num_programs(2) - 1)
    def _():
        y = x_ref[0] + mod_ref[0, 5:6, :] * acc_scr[...].T
        if final:
            y = _rms_rows(y, fn_ref[...])
        o_ref[0] = y


def _peer_expert(h2, route, u, vt, x, mod, final_gain, *, tt, ec):
    b, t, d = x.shape
    nt = t // tt
    ne = u.shape[0] // ec
    r1, e1, cut, e0 = route
    once = dict(pipeline_mode=pl.Buffered(1))
    rt = lambda: pl.BlockSpec((PEER_HEADS, PEER_KEYS, tt), lambda bi, i, j: (0, 0, bi * nt + i), **once)
    rj = lambda: pl.BlockSpec((PEER_HEADS, ec // PEER_KEYS, tt), lambda bi, i, j: (0, j, bi * nt + i))
    in_specs = [pl.BlockSpec((tt, d), lambda bi, i, j: (bi * nt + i, 0), **once),
                pl.BlockSpec((ec, d), lambda bi, i, j: (0, 0), **once),
                pl.BlockSpec((ec, d), lambda bi, i, j: (jnp.minimum(j + 1, ne - 1), 0)),
                pl.BlockSpec((1, d, ec), lambda bi, i, j: (j, 0, 0)),
                rt(), rt(), rj(), rj(),
                pl.BlockSpec((1, tt, d), lambda bi, i, j: (bi, i, 0), **once),
                pl.BlockSpec((1, 6, d), lambda bi, i, j: (bi, 0, 0))]
    args = [h2, u, u, vt, r1, e1, cut, e0, x, mod]
    if final_gain is not None:
        in_specs.append(pl.BlockSpec((1, d), lambda bi, i, j: (0, 0)))
        args.append(final_gain.reshape(1, d))
    return pl.pallas_call(
        functools.partial(_peer_expert_kernel, final=final_gain is not None),
        grid=(b, nt, ne),
        in_specs=in_specs,
        out_specs=pl.BlockSpec((1, tt, d), lambda bi, i, j: (bi, i, 0)),
        out_shape=jax.ShapeDtypeStruct((b, t, d), F32),
        scratch_shapes=[pltpu.VMEM((d, tt), F32), pltpu.VMEM((2, ec, tt), BF16)],
        compiler_params=_cp("parallel", "parallel", "arbitrary"),
        name="peer_expert",
    )(*args)


def _peer(x, g, mod, wq, kt, u, vt, final_gain, *, tm, tt, ec):
    out = _peer_route(x, g, mod, wq, kt, tm=tm)
    return _peer_expert(out[0], out[1:], u, vt, x, mod, final_gain, tt=tt, ec=ec)


def _lat_tile(t, want):
    while t % want:
        want //= 2
    return want


def kernel(x, c, ctx, c_ctx, w_mod, b_mod, norm1, norm2, even_w_in, even_ret_gn, even_w_out, odd_w_in, odd_sink, odd_mla_q_norm, odd_mla_kv_norm, odd_mla_w_uq, odd_mla_w_ukv, odd_w_out, peer_w_q, peer_sub_keys, peer_u, peer_v, final_norm):
    b, s_len, d = x.shape
    n_ctx = ctx.shape[1]
    depth = w_mod.shape[0]
    assert depth == 2 and even_w_in.shape[0] == 1 and odd_w_in.shape[0] == 1
    tm_l = _lat_tile(s_len, 1024)
    tm_c = n_ctx

    rows = 8 * ((b + 1 + 7) // 8)
    cond = jnp.zeros((rows, d), F32).at[:b].set(c).at[b].set(c_ctx)
    mods = _adaln(cond, w_mod, b_mod).reshape(depth, rows, 6, d)
    mod_l = [mods[l, :b] for l in range(depth)]
    mod_c = [jnp.broadcast_to(mods[l, b:b + 1], (b, 6, d)) for l in range(depth)]

    x_lat, x_ctx = x, ctx

    def peer_params(layer):
        kt = peer_sub_keys[layer].reshape(2 * PEER_HEADS, PEER_KEYS, PEER_HALF).astype(BF16)
        n_exp = peer_v.shape[1]
        vt = jnp.transpose(peer_v[layer].astype(BF16).reshape(n_exp // EXPERT_CHUNK, EXPERT_CHUNK, d), (0, 2, 1))
        return (peer_w_q[layer].astype(BF16), kt, peer_u[layer].astype(BF16), vt)

    w_in = even_w_in[0].astype(BF16)
    w_out = even_w_out[0].astype(BF16)
    p_l = _proj(x_lat, norm1[0], mod_l[0], w_in, si=0, tm=tm_l, tn=512, name="even_proj_lat")
    p_c = _proj(x_ctx, norm1[0], mod_c[0], w_in, si=0, tm=tm_c, tn=512, name="even_proj_ctx")
    fm_l = _fourier_lat(p_l[..., :FOURIER_WIDTH])
    fm_c = _fourier_small(p_c[..., :FOURIER_WIDTH])
    tabs_l = _rope_tables(s_len, RET_DIM, LANES)
    tabs_c = _rope_tables(n_ctx, RET_DIM, LANES, identity=True)
    zero = jnp.zeros((b, RET_HEADS, RET_DIM, RET_DIM), F32)
    gn = even_ret_gn[0]
    rf_c, st_f = _retention(p_c, tabs_c, gn, zero, None, reverse=False, gate_blk=4)
    r_c, st_b = _retention(p_c, tabs_c, gn, zero, rf_c, reverse=True, gate_blk=5)
    rf_l, _ = _retention(p_l, tabs_l, gn, st_f, None, reverse=False, gate_blk=4)
    r_l, _ = _retention(p_l, tabs_l, gn, st_b, rf_l, reverse=True, gate_blk=5)
    x_lat = _outproj(fm_l, r_l, w_out, x_lat, mod_l[0], tm=tm_l, tn=512, name="even_out_lat")
    x_ctx = _outproj(fm_c, r_c, w_out, x_ctx, mod_c[0], tm=tm_c, tn=512, name="even_out_ctx")
    pp = peer_params(0)
    tt_l = _lat_tile(s_len, 512)
    x_lat = _peer(x_lat, norm2[0], mod_l[0], *pp, None, tm=256, tt=tt_l, ec=EXPERT_CHUNK)
    x_ctx = _peer(x_ctx, norm2[0], mod_c[0], *pp, None, tm=256, tt=n_ctx, ec=EXPERT_CHUNK)

    wi = odd_w_in[0]
    o_sq, o_sk, o_sv = 0, SWA_Q_HEADS * SWA_HEAD_DIM, (SWA_Q_HEADS + SWA_KV_HEADS) * SWA_HEAD_DIM
    o_cq = o_sv + SWA_KV_HEADS * SWA_HEAD_DIM
    o_ckv = o_cq + MLA_Q_RANK
    o_kr = o_ckv + MLA_KV_RANK
    w_in = jnp.concatenate([wi[:, o_sq:o_sk], wi[:, o_cq:o_ckv], wi[:, o_ckv:o_kr], wi[:, o_sk:o_sv], wi[:, o_sv:o_cq],
                            wi[:, o_kr:], jnp.zeros((d, LANES - MLA_ROPE), wi.dtype)], axis=1).astype(BF16)
    n_odd = w_in.shape[1]
    q_blk, cq_blk, ckv_blk = 0, (o_sk - o_sq) // MLA_Q_RANK, (o_sk - o_sq + MLA_Q_RANK) // MLA_KV_RANK
    k_blk = (o_sk - o_sq + MLA_Q_RANK + MLA_KV_RANK) // LANES
    v_blk, kr_blk = k_blk + 1, k_blk + 2
    tm_o = _lat_tile(s_len, 512)
    p_l = _proj(x_lat, norm1[1], mod_l[1], w_in, si=0, tm=tm_o, tn=n_odd, name="odd_proj_lat")
    p_c = _proj(x_ctx, norm1[1], mod_c[1], w_in, si=0, tm=tm_c, tn=n_odd, name="odd_proj_ctx")
    tabs_swa = _rope_tables(s_len, SWA_HEAD_DIM, LANES)
    swa = _swa(p_l, p_c, tabs_swa, odd_sink[0], q_blk=q_blk, k_blk=k_blk, v_blk=v_blk)
    wuq = odd_mla_w_uq[0].reshape(MLA_Q_RANK, MLA_HEADS, MLA_QK_DIM)
    wuq = jnp.pad(wuq, ((0, 0), (0, 0), (0, MLA_PAD - MLA_QK_DIM))).reshape(MLA_Q_RANK, MLA_HEADS * MLA_PAD)
    wukv = odd_mla_w_ukv[0].reshape(MLA_KV_RANK, MLA_HEADS, MLA_NOPE + MLA_V)
    wk = wukv[:, :, :MLA_NOPE].reshape(MLA_KV_RANK, -1).astype(BF16)
    wvt = wukv[:, :, MLA_NOPE:].reshape(MLA_KV_RANK, -1).T.astype(BF16)
    tabs_mla = _pad_rope_tables(_rope_tables(s_len, MLA_ROPE, MLA_ROPE), LANES)
    tabs_id = _rope_tables(n_ctx, MLA_ROPE, LANES, identity=True)
    qt = _mla_q(p_l, odd_mla_q_norm[0], wuq.T.astype(BF16), tuple(a.T for a in tabs_mla), x_blk=cq_blk, tm=tm_o)
    k_l, vt_l = _mla_kv(p_l, odd_mla_kv_norm[0], wk, wvt, tabs_mla, x_blk=ckv_blk, kr_blk=kr_blk, tm=tm_o)
    k_c, vt_c = _mla_kv(p_c, odd_mla_kv_norm[0], wk, wvt, tabs_id, x_blk=ckv_blk, kr_blk=kr_blk, tm=tm_c)
    k_all = jnp.concatenate([k_c, k_l], axis=2)
    lk = n_ctx + s_len
    tk = max(t for t in (1280, 1024, 768, 512, 256, 128) if lk % t == 0)
    vt_all = jnp.concatenate([vt_c, vt_l], axis=2).reshape(b, MLA_HEADS, MLA_V, lk // tk, tk)
    vt_all = jnp.transpose(vt_all, (0, 1, 3, 2, 4))
    mla = _flash(qt, k_all, vt_all, tq=_lat_tile(s_len, 512))
    x_lat = _outproj(swa, mla, odd_w_out[0].astype(BF16), x_lat, mod_l[1], tm=tm_l, tn=512, name="odd_out_lat")
    pp = peer_params(1)
    return _peer(x_lat, norm2[1], mod_l[1], *pp, final_norm, tm=256, tt=tt_l, ec=EXPERT_CHUNK)
```

```python
import functools
import math

import numpy as np
import jax
import jax.numpy as jnp
from jax import lax
from jax.experimental import pallas as pl
from jax.experimental.pallas import tpu as pltpu

F32 = jnp.float32
BF16 = jnp.bfloat16

GRID_W = 64
NORM_EPS = 1e-6
ROPE_BASE = 10000.0
MASK_VALUE = -1e30
LANES = 128

FOURIER_GROUPS = 4
FOURIER_GROUP_DIM = 256
FOURIER_WIDTH = FOURIER_GROUPS * FOURIER_GROUP_DIM
RET_HEADS = 8
RET_DIM = 128
RET_CHUNK = 128
RET_WIDTH = RET_HEADS * RET_DIM

SWA_Q_HEADS = 16
SWA_KV_HEADS = 2
SWA_GROUP = SWA_Q_HEADS // SWA_KV_HEADS
SWA_HEAD_DIM = 64
SWA_WINDOW = 128
SWA_BLOCK = 128
MLA_HEADS = 8
MLA_Q_RANK = 512
MLA_KV_RANK = 256
MLA_NOPE = 128
MLA_ROPE = 64
MLA_V = 128
MLA_QK_DIM = MLA_NOPE + MLA_ROPE
MLA_PAD = 2 * LANES

PEER_HEADS = 8
PEER_KEYS = 128
PEER_TOPK = 16
PEER_HALF = 128
EXPERT_CHUNK = 1024

VMEM_LIMIT = 56 * 1024 * 1024


def _cp(*sem):
    return pltpu.CompilerParams(dimension_semantics=sem, vmem_limit_bytes=VMEM_LIMIT)


def _adaln_kernel(c_ref, w_ref, b_ref, o_ref):
    c = c_ref[...]
    s = c * jax.nn.sigmoid(c)
    o_ref[0] = jnp.dot(s, w_ref[0], preferred_element_type=F32) + b_ref[0]


def _adaln(cond, w_mod, b_mod):
    depth, d, n = w_mod.shape
    r = cond.shape[0]
    tn = 1024
    return pl.pallas_call(
        _adaln_kernel,
        grid=(depth, n // tn),
        in_specs=[pl.BlockSpec((r, d), lambda l, j: (0, 0)),
                  pl.BlockSpec((1, d, tn), lambda l, j: (l, 0, j)),
                  pl.BlockSpec((1, 1, tn), lambda l, j: (l, 0, j))],
        out_specs=pl.BlockSpec((1, r, tn), lambda l, j: (l, 0, j)),
        out_shape=jax.ShapeDtypeStruct((depth, r, n), F32),
        compiler_params=_cp("parallel", "parallel"),
        name="adaln",
    )(cond, w_mod, b_mod.reshape(depth, 1, n))


def _hmod(x, g, mod, si):
    ms = jnp.mean(x * x, axis=-1, keepdims=True)
    y = x * lax.rsqrt(ms + NORM_EPS) * g
    return y * (1.0 + mod[si + 1:si + 2, :]) + mod[si:si + 1, :]


def _proj_kernel(x_ref, g_ref, mod_ref, w_ref, o_ref, h_scr, *, si):
    @pl.when(pl.program_id(2) == 0)
    def _():
        h_scr[...] = _hmod(x_ref[0], g_ref[...], mod_ref[0], si).astype(BF16)

    o_ref[0] = jnp.dot(h_scr[...], w_ref[...], preferred_element_type=F32).astype(o_ref.dtype)


def _proj(x, g, mod, w, *, si, tm, tn, name):
    b, t, d = x.shape
    n = w.shape[1]
    return pl.pallas_call(
        functools.partial(_proj_kernel, si=si),
        grid=(b, t // tm, n // tn),
        in_specs=[pl.BlockSpec((1, tm, d), lambda bi, i, j: (bi, i, 0)),
                  pl.BlockSpec((1, d), lambda bi, i, j: (0, 0)),
                  pl.BlockSpec((1, 6, d), lambda bi, i, j: (bi, 0, 0)),
                  pl.BlockSpec((d, tn), lambda bi, i, j: (0, j))],
        out_specs=pl.BlockSpec((1, tm, tn), lambda bi, i, j: (bi, i, j)),
        out_shape=jax.ShapeDtypeStruct((b, t, n), BF16),
        scratch_shapes=[pltpu.VMEM((tm, d), BF16)],
        compiler_params=_cp("parallel", "parallel", "arbitrary"),
        name=name,
    )(x, g.reshape(1, d), mod, w)


def _outproj_kernel(a_ref, b_ref, w_ref, x_ref, mod_ref, o_ref):
    ka = a_ref.shape[2]
    acc = jnp.dot(a_ref[0], w_ref[:ka, :], preferred_element_type=F32)
    acc = acc + jnp.dot(b_ref[0], w_ref[ka:, :], preferred_element_type=F32)
    o_ref[0] = x_ref[0] + mod_ref[0, 2:3, :] * acc


def _outproj(a, b2, w, x, mod, *, tm, tn, name):
    b, t, d = x.shape
    ka, kb = a.shape[2], b2.shape[2]
    return pl.pallas_call(
        _outproj_kernel,
        grid=(b, t // tm, d // tn),
        in_specs=[pl.BlockSpec((1, tm, ka), lambda bi, i, j: (bi, i, 0)),
                  pl.BlockSpec((1, tm, kb), lambda bi, i, j: (bi, i, 0)),
                  pl.BlockSpec((ka + kb, tn), lambda bi, i, j: (0, j)),
                  pl.BlockSpec((1, tm, tn), lambda bi, i, j: (bi, i, j)),
                  pl.BlockSpec((1, 6, tn), lambda bi, i, j: (bi, 0, j))],
        out_specs=pl.BlockSpec((1, tm, tn), lambda bi, i, j: (bi, i, j)),
        out_shape=jax.ShapeDtypeStruct((b, t, d), F32),
        compiler_params=_cp("parallel", "parallel", "arbitrary"),
        name=name,
    )(a, b2, w, x, mod)


def _rope_tables(n_pos, d_rot, width, identity=False):
    q = d_rot // 4
    if identity:
        return (jnp.ones((n_pos, width), F32), jnp.zeros((n_pos, width), F32), jnp.zeros((n_pos, width), F32))
    d_axis = d_rot // 2
    inv = ROPE_BASE ** (-jnp.arange(0, d_axis, 2, dtype=F32) / d_axis)
    t = jnp.arange(n_pos)
    row = (t // GRID_W).astype(F32)
    col = (t % GRID_W).astype(F32)
    ar = row[:, None] * inv[None, :]
    ac = col[:, None] * inv[None, :]
    ang = jnp.concatenate([ar, ar, ac, ac], axis=-1)
    cos, sin = jnp.cos(ang), jnp.sin(ang)
    quarter = np.arange(d_rot) // q
    up = jnp.asarray((quarter % 2 == 1).astype(np.float32))
    dn = jnp.asarray((quarter % 2 == 0).astype(np.float32))
    sin_p, sin_m = sin * up, -sin * dn
    reps = width // d_rot
    return tuple(jnp.tile(a, (1, reps)) for a in (cos, sin_p, sin_m))


def _pad_rope_tables(tabs, width):
    cos, sp, sm = tabs
    pad = width - cos.shape[1]
    return (jnp.pad(cos, ((0, 0), (0, pad)), constant_values=1.0), jnp.pad(sp, ((0, 0), (0, pad))),
            jnp.pad(sm, ((0, 0), (0, pad))))


def _rope(t, cos, sp, sm, q):
    w = t.shape[-1]
    return t * cos + pltpu.roll(t, q, 1) * sp + pltpu.roll(t, w - q, 1) * sm


def _dft_mats(n):
    k = np.arange(n)
    ang = 2.0 * np.pi * ((k[:, None] * k[None, :]) % n) / n
    return np.cos(ang), np.sin(ang)


def _fourier_a_kernel(u_ref, fa_ref, tc_ref, ts_ref, o_ref):
    l1 = u_ref.shape[1]
    t1 = jnp.dot(fa_ref[...], u_ref[0], preferred_element_type=F32)
    tr, ti = t1[:l1], t1[l1:]
    reps = tr.shape[1] // LANES
    c = jnp.concatenate([tc_ref[0]] * reps, axis=1)
    s = jnp.concatenate([ts_ref[0]] * reps, axis=1)
    o_ref[0, 0] = (tr * c + ti * s).astype(o_ref.dtype)
    o_ref[0, 1] = (ti * c - tr * s).astype(o_ref.dtype)


def _fourier_b_kernel(t_ref, fb_ref, cs_ref, o_ref):
    l2 = t_ref.shape[2]
    t2 = jnp.concatenate([t_ref[0, 0], t_ref[0, 1]], axis=0)
    v = jnp.dot(fb_ref[...], t2, preferred_element_type=F32).astype(BF16)
    vr, vi = v[:l2], v[l2:]
    outs = []
    for g in range(FOURIER_GROUPS):
        sl = slice(g * FOURIER_GROUP_DIM, (g + 1) * FOURIER_GROUP_DIM)
        vg = jnp.concatenate([vr[:, sl], vi[:, sl]], axis=1)
        outs.append(jnp.dot(vg, cs_ref[...], preferred_element_type=F32))
    o_ref[0] = jnp.concatenate(outs, axis=1).astype(o_ref.dtype)


def _fourier_lat(u):
    b, l, w = u.shape
    l1 = int(round(math.sqrt(l)))
    l2 = l // l1
    assert l1 * l2 == l and l1 % 16 == 0 and l2 % 16 == 0
    ca, sa = _dft_mats(l1)
    fa = jnp.asarray(np.concatenate([ca, -sa], axis=0), BF16)
    kk = np.arange(l1)[None, :, None] * np.arange(l2)[:, None, None]
    ang = 2.0 * np.pi * (kk % l) / l
    tc = jnp.asarray(np.broadcast_to(np.cos(ang), (l2, l1, LANES)), F32)
    ts = jnp.asarray(np.broadcast_to(np.sin(ang), (l2, l1, LANES)), F32)
    t2 = pl.pallas_call(
        _fourier_a_kernel,
        grid=(b, l2),
        in_specs=[pl.BlockSpec((1, l1, w), lambda bi, j: (bi, 0, j)),
                  pl.BlockSpec((2 * l1, l1), lambda bi, j: (0, 0)),
                  pl.BlockSpec((1, l1, LANES), lambda bi, j: (j, 0, 0)),
                  pl.BlockSpec((1, l1, LANES), lambda bi, j: (j, 0, 0))],
        out_specs=pl.BlockSpec((1, 2, l1, w), lambda bi, j: (bi, 0, 0, j)),
        out_shape=jax.ShapeDtypeStruct((b, 2, l1, l2 * w), BF16),
        compiler_params=_cp("parallel", "parallel"),
        name="fourier_a",
    )(u.reshape(b, l1, l2 * w), fa, tc, ts)
    cb, sb = _dft_mats(l2)
    scale = 1.0 / math.sqrt(l * FOURIER_GROUP_DIM)
    fb = jnp.asarray(np.block([[cb, sb], [-sb, cb]]) * scale, BF16)
    cc, sc = _dft_mats(FOURIER_GROUP_DIM)
    cs = jnp.asarray(np.concatenate([cc, sc], axis=0), BF16)
    y = pl.pallas_call(
        _fourier_b_kernel,
        grid=(b, l1),
        in_specs=[pl.BlockSpec((1, 2, l2, w), lambda bi, j: (bi, 0, j, 0)),
                  pl.BlockSpec((2 * l2, 2 * l2), lambda bi, j: (0, 0)),
                  pl.BlockSpec((2 * FOURIER_GROUP_DIM, FOURIER_GROUP_DIM), lambda bi, j: (0, 0))],
        out_specs=pl.BlockSpec((1, l2, w), lambda bi, j: (bi, 0, j)),
        out_shape=jax.ShapeDtypeStruct((b, l2, l1 * w), BF16),
        compiler_params=_cp("parallel", "parallel"),
        name="fourier_b",
    )(t2.reshape(b, 2, l1 * l2, w), fb, cs)
    return y.reshape(b, l, w)


def _fourier_small_kernel(u_ref, cl_ref, sl_ref, cs_ref, o_ref):
    gd = FOURIER_GROUP_DIM
    outs = []
    for g in range(FOURIER_GROUPS):
        ug = u_ref[0][:, g * gd:(g + 1) * gd]
        pq = jnp.dot(ug, cs_ref[...], preferred_element_type=F32).astype(BF16)
        y = jnp.dot(cl_ref[...], pq[:, :gd], preferred_element_type=F32)
        y = y - jnp.dot(sl_ref[...], pq[:, gd:], preferred_element_type=F32)
        outs.append(y)
    o_ref[0] = jnp.concatenate(outs, axis=1).astype(o_ref.dtype)


def _fourier_small(u):
    b, l, w = u.shape
    cl, sl = _dft_mats(l)
    scale = 1.0 / math.sqrt(l * FOURIER_GROUP_DIM)
    cc, sc = _dft_mats(FOURIER_GROUP_DIM)
    cs = jnp.asarray(np.concatenate([cc, sc], axis=1), BF16)
    return pl.pallas_call(
        _fourier_small_kernel,
        grid=(b,),
        in_specs=[pl.BlockSpec((1, l, w), lambda bi: (bi, 0, 0)),
                  pl.BlockSpec((l, l), lambda bi: (0, 0)),
                  pl.BlockSpec((l, l), lambda bi: (0, 0)),
                  pl.BlockSpec((FOURIER_GROUP_DIM, 2 * FOURIER_GROUP_DIM), lambda bi: (0, 0))],
        out_specs=pl.BlockSpec((1, l, w), lambda bi: (bi, 0, 0)),
        out_shape=jax.ShapeDtypeStruct((b, l, w), BF16),
        compiler_params=_cp("parallel"),
        name="fourier_small",
    )(u, jnp.asarray(cl * scale, BF16), jnp.asarray(sl * scale, BF16), cs)


def _ret_consts(reverse):
    c = RET_CHUNK
    lg = np.log(1.0 - np.exp2(-5.0 - np.arange(RET_HEADS, dtype=np.float32))).astype(np.float32)
    pos = np.arange(c, dtype=np.float32)
    diff = pos[:, None] - pos[None, :]
    if reverse:
        diff = -diff
    decay = np.where(diff >= 0, np.exp(lg[:, None, None] * np.maximum(diff, 0.0)), 0.0)
    if reverse:
        zeta = np.exp(lg[None, :] * pos[:, None])
        xi = np.exp(lg[None, :] * (c - pos[:, None]))
    else:
        zeta = np.exp(lg[None, :] * (c - 1.0 - pos[:, None]))
        xi = np.exp(lg[None, :] * (pos[:, None] + 1.0))
    wide = lambda a: np.repeat(a, RET_DIM, axis=1)
    chunk_decay = [float(v) for v in np.exp(lg * c)]
    return (jnp.asarray(decay, F32), jnp.asarray(wide(zeta), F32), jnp.asarray(wide(xi), F32), chunk_decay)


def _ret_kernel(*refs, chunk_decay, has_prev):
    if has_prev:
        (q_ref, k_ref, v_ref, g_ref, cos_ref, sp_ref, sm_ref, dec_ref, zeta_ref, xi_ref, gn_ref, s0_ref, prev_ref,
         o_ref, sf_ref, st_scr) = refs
    else:
        (q_ref, k_ref, v_ref, g_ref, cos_ref, sp_ref, sm_ref, dec_ref, zeta_ref, xi_ref, gn_ref, s0_ref,
         o_ref, sf_ref, st_scr) = refs
        prev_ref = None
    c = pl.program_id(1)

    @pl.when(c == 0)
    def _():
        st_scr[...] = s0_ref[0]

    cos, sp, sm = cos_ref[...], sp_ref[...], sm_ref[...]
    outs = []
    for h in range(RET_HEADS):
        sl = slice(h * RET_DIM, (h + 1) * RET_DIM)
        q = _rope(q_ref[0][:, sl].astype(F32), cos, sp, sm, RET_DIM // 4)
        k = _rope(k_ref[0][:, sl].astype(F32), cos, sp, sm, RET_DIM // 4) * (RET_DIM ** -0.5)
        v = v_ref[0][:, sl].astype(F32)
        kt = k.T.astype(BF16)
        scores = jnp.dot(q.astype(BF16), kt, preferred_element_type=F32) * dec_ref[h]
        inner = jnp.dot(scores.astype(BF16), v.astype(BF16), preferred_element_type=F32)
        state = st_scr[h]
        cross = jnp.dot((q * xi_ref[:, sl]).astype(BF16), state.astype(BF16), preferred_element_type=F32)
        o = inner + cross
        st_scr[h] = state * chunk_decay[h] + jnp.dot(kt, (v * zeta_ref[:, sl]).astype(BF16),
                                                      preferred_element_type=F32)
        y = o * lax.rsqrt(jnp.mean(o * o, axis=-1, keepdims=True) + NORM_EPS) * gn_ref[:, sl]
        gate = g_ref[0][:, sl].astype(F32)
        outs.append(y * (gate * jax.nn.sigmoid(gate)))
    res = jnp.concatenate(outs, axis=1)
    if prev_ref is not None:
        res = res + prev_ref[0].astype(F32)
    o_ref[0] = res.astype(o_ref.dtype)

    @pl.when(c == pl.num_programs(1) - 1)
    def _():
        sf_ref[0] = st_scr[...]


def _retention(proj, tabs, gn, state0, prev, *, reverse, gate_blk):
    b, t, _ = proj.shape
    w = RET_WIDTH
    n = t // RET_CHUNK
    decay, zeta, xi, chunk_decay = _ret_consts(reverse)
    pos = (lambda ci: n - 1 - ci) if reverse else (lambda ci: ci)
    blk = lambda col: pl.BlockSpec((1, RET_CHUNK, w), lambda bi, ci: (bi, pos(ci), col))
    tab = pl.BlockSpec((RET_CHUNK, LANES), lambda bi, ci: (pos(ci), 0))
    const2 = pl.BlockSpec((RET_CHUNK, w), lambda bi, ci: (0, 0))
    st = pl.BlockSpec((1, RET_HEADS, RET_DIM, RET_DIM), lambda bi, ci: (bi, 0, 0, 0))
    in_specs = [blk(1), blk(2), blk(3), blk(gate_blk), tab, tab, tab,
                pl.BlockSpec((RET_HEADS, RET_CHUNK, RET_CHUNK), lambda bi, ci: (0, 0, 0)), const2, const2,
                pl.BlockSpec((1, w), lambda bi, ci: (0, 0)), st]
    args = [proj, proj, proj, proj, *tabs, decay, zeta, xi, gn.reshape(1, w), state0]
    if prev is not None:
        in_specs.append(pl.BlockSpec((1, RET_CHUNK, w), lambda bi, ci: (bi, pos(ci), 0)))
        args.append(prev)
    return pl.pallas_call(
        functools.partial(_ret_kernel, chunk_decay=chunk_decay, has_prev=prev is not None),
        grid=(b, n),
        in_specs=in_specs,
        out_specs=[pl.BlockSpec((1, RET_CHUNK, w), lambda bi, ci: (bi, pos(ci), 0)), st],
        out_shape=[jax.ShapeDtypeStruct((b, t, w), BF16),
                   jax.ShapeDtypeStruct((b, RET_HEADS, RET_DIM, RET_DIM), F32)],
        scratch_shapes=[pltpu.VMEM((RET_HEADS, RET_DIM, RET_DIM), F32)],
        compiler_params=_cp("parallel", "arbitrary"),
        name="retention_bwd" if reverse else "retention_fwd",
    )(*args)


def _swa_kernel(q_ref, kc_ref, vc_ref, kl_ref, km_ref, kr_ref, vl_ref, vm_ref, vr_ref,
                cq_ref, spq_ref, smq_ref, cl_ref, spl_ref, sml_ref, cm_ref, spm_ref, smm_ref,
                cr_ref, spr_ref, smr_ref, sink_ref, o_ref, *, seq_len):
    i = pl.program_id(1)
    qd = SWA_HEAD_DIM // 4
    hd = SWA_HEAD_DIM
    cq, spq, smq = cq_ref[...], spq_ref[...], smq_ref[...]
    rk = lambda kref, c, sp, sm: _rope(kref[0].astype(F32), c[...], sp[...], sm[...], qd)
    k_all = jnp.concatenate([kc_ref[0].astype(F32), rk(kl_ref, cl_ref, spl_ref, sml_ref),
                             rk(km_ref, cm_ref, spm_ref, smm_ref), rk(kr_ref, cr_ref, spr_ref, smr_ref)], axis=0)
    v_all = jnp.concatenate([vc_ref[0], vl_ref[0], vm_ref[0], vr_ref[0]], axis=0).astype(F32)
    vt = v_all.T
    n_ctx = kc_ref.shape[1]
    n_keys = k_all.shape[0]
    n_loc = 3 * SWA_BLOCK
    gw = SWA_GROUP * SWA_BLOCK
    cidx = lax.broadcasted_iota(jnp.int32, (n_loc, gw), 0)
    r = lax.broadcasted_iota(jnp.int32, (n_loc, gw), 1) % SWA_BLOCK
    kpos = (i - 1) * SWA_BLOCK + cidx
    valid = (jnp.abs(SWA_BLOCK + r - cidx) <= SWA_WINDOW) & (kpos >= 0) & (kpos < seq_len)
    klane = lax.broadcasted_iota(jnp.int32, (n_keys, LANES), 1)
    vrow = lax.broadcasted_iota(jnp.int32, (LANES, n_keys), 0)
    qt_blk = [(_rope(q_ref[0][:, blk * LANES:(blk + 1) * LANES].astype(F32), cq, spq, smq, qd) * (hd ** -0.5)).T
              for blk in range(SWA_Q_HEADS // 2)]
    zeros = jnp.zeros((hd, gw), F32)
    outs = []
    for kv in range(SWA_KV_HEADS):
        pieces = []
        for h in range(kv * SWA_GROUP, (kv + 1) * SWA_GROUP):
            pieces.append(qt_blk[h // 2][(h % 2) * hd:(h % 2 + 1) * hd, :])
        qg = jnp.concatenate(pieces, axis=1)
        qg = jnp.concatenate([qg, zeros] if kv == 0 else [zeros, qg], axis=0).astype(BF16)
        in_kv = (klane < hd) if kv == 0 else (klane >= hd)
        st = jnp.dot(jnp.where(in_kv, k_all, 0.0).astype(BF16), qg, preferred_element_type=F32)
        st = jnp.concatenate([st[:n_ctx], jnp.where(valid, st[n_ctx:], MASK_VALUE)], axis=0)
        sk = sink_ref[kv]
        m = jnp.maximum(jnp.max(st, axis=0, keepdims=True), sk)
        e = jnp.exp(st - m)
        p = e * (1.0 / (jnp.sum(e, axis=0, keepdims=True) + jnp.exp(sk - m)))
        v_kv = jnp.where((vrow < hd) if kv == 0 else (vrow >= hd), vt, 0.0).astype(BF16)
        ot = jnp.dot(v_kv, p.astype(BF16), preferred_element_type=F32)[kv * hd:(kv + 1) * hd, :]
        stacked = jnp.concatenate([ot[:, g * SWA_BLOCK:(g + 1) * SWA_BLOCK] for g in range(SWA_GROUP)], axis=0)
        outs.append(stacked.T)
    o_ref[0] = jnp.concatenate(outs, axis=1).astype(o_ref.dtype)


def _swa(proj_l, proj_c, tabs, sink, *, q_blk, k_blk, v_blk):
    b, l, _ = proj_l.shape
    n_ctx = proj_c.shape[1]
    nb = l // SWA_BLOCK
    qw = SWA_Q_HEADS * SWA_HEAD_DIM
    left = lambda i: jnp.maximum(i - 1, 0)
    right = lambda i: jnp.minimum(i + 1, nb - 1)
    kvspec = lambda col, f: pl.BlockSpec((1, SWA_BLOCK, LANES), lambda bi, i: (bi, f(i), col))
    ctxspec = lambda col: pl.BlockSpec((1, n_ctx, LANES), lambda bi, i: (bi, 0, col))
    tab = lambda f: pl.BlockSpec((SWA_BLOCK, LANES), lambda bi, i: (f(i), 0))
    ident = lambda i: i
    in_specs = [pl.BlockSpec((1, SWA_BLOCK, qw), lambda bi, i: (bi, i, q_blk)),
                ctxspec(k_blk), ctxspec(v_blk),
                kvspec(k_blk, left), kvspec(k_blk, ident), kvspec(k_blk, right),
                kvspec(v_blk, left), kvspec(v_blk, ident), kvspec(v_blk, right)]
    args = [proj_l, proj_c, proj_c, proj_l, proj_l, proj_l, proj_l, proj_l, proj_l]
    for f in (ident, left, ident, right):
        in_specs += [tab(f)] * 3
        args += list(tabs)
    gw = SWA_GROUP * SWA_BLOCK
    in_specs.append(pl.BlockSpec((SWA_KV_HEADS, 1, gw), lambda bi, i: (0, 0, 0)))
    args.append(jnp.repeat(sink.reshape(SWA_KV_HEADS, SWA_GROUP).astype(F32), SWA_BLOCK, axis=1)
                .reshape(SWA_KV_HEADS, 1, gw))
    return pl.pallas_call(
        functools.partial(_swa_kernel, seq_len=l),
        grid=(b, nb),
        in_specs=in_specs,
        out_specs=pl.BlockSpec((1, SWA_BLOCK, qw), lambda bi, i: (bi, i, 0)),
        out_shape=jax.ShapeDtypeStruct((b, l, qw), BF16),
        compiler_params=_cp("parallel", "parallel"),
        name="swa",
    )(*args)


def _rms_rows(x, g):
    return x * lax.rsqrt(jnp.mean(x * x, axis=-1, keepdims=True) + NORM_EPS) * g


NT_DIMS = (((1,), (1,)), ((), ()))


def _mla_q_kernel(x_ref, g_ref, wt_ref, cos_ref, sp_ref, sm_ref, o_ref):
    x = _rms_rows(x_ref[0].astype(F32), g_ref[...]).astype(BF16)
    mqt = lax.dot_general(wt_ref[...], x, NT_DIMS, preferred_element_type=F32)
    cos, sp, sm = cos_ref[...], sp_ref[...], sm_ref[...]
    scale = MLA_QK_DIM ** -0.5 * math.log2(math.e)
    q = MLA_ROPE // 4
    for h in range(MLA_HEADS):
        o_ref[0, h * MLA_PAD:h * MLA_PAD + LANES, :] = (mqt[h * MLA_PAD:h * MLA_PAD + LANES] * scale).astype(o_ref.dtype)
        r = mqt[h * MLA_PAD + LANES:(h + 1) * MLA_PAD]
        r = r * cos + pltpu.roll(r, q, 0) * sp + pltpu.roll(r, LANES - q, 0) * sm
        o_ref[0, h * MLA_PAD + LANES:(h + 1) * MLA_PAD, :] = (r * scale).astype(o_ref.dtype)


def _mla_q(proj, g, wt, tabs_t, *, x_blk, tm):
    b, t, _ = proj.shape
    n, kq = wt.shape
    tab = pl.BlockSpec((LANES, tm), lambda bi, i: (0, i))
    return pl.pallas_call(
        _mla_q_kernel,
        grid=(b, t // tm),
        in_specs=[pl.BlockSpec((1, tm, kq), lambda bi, i: (bi, i, x_blk)),
                  pl.BlockSpec((1, kq), lambda bi, i: (0, 0)),
                  pl.BlockSpec((n, kq), lambda bi, i: (0, 0)), tab, tab, tab],
        out_specs=pl.BlockSpec((1, n, tm), lambda bi, i: (bi, 0, i)),
        out_shape=jax.ShapeDtypeStruct((b, n, t), BF16),
        compiler_params=_cp("parallel", "parallel"),
        name="mla_q",
    )(proj, g.reshape(1, kq), wt, *tabs_t)


def _mla_kv_kernel(x_ref, kr_ref, g_ref, wk_ref, wvt_ref, cos_ref, sp_ref, sm_ref, k_ref, vt_ref):
    x = _rms_rows(x_ref[0].astype(F32), g_ref[...]).astype(BF16)
    kn = jnp.dot(x, wk_ref[...], preferred_element_type=F32)
    kr = _rope(kr_ref[0].astype(F32), cos_ref[...], sp_ref[...], sm_ref[...], MLA_ROPE // 4)
    for h in range(MLA_HEADS):
        k_ref[0, h] = jnp.concatenate([kn[:, h * MLA_NOPE:(h + 1) * MLA_NOPE], kr], axis=1).astype(k_ref.dtype)
    vt_ref[0] = lax.dot_general(wvt_ref[...], x, NT_DIMS, preferred_element_type=F32).astype(vt_ref.dtype)


def _mla_kv(proj, g, wk, wvt, tabs, *, x_blk, kr_blk, tm):
    b, t, _ = proj.shape
    kk = wk.shape[0]
    tab = pl.BlockSpec((tm, LANES), lambda bi, i: (i, 0))
    return pl.pallas_call(
        _mla_kv_kernel,
        grid=(b, t // tm),
        in_specs=[pl.BlockSpec((1, tm, kk), lambda bi, i: (bi, i, x_blk)),
                  pl.BlockSpec((1, tm, LANES), lambda bi, i: (bi, i, kr_blk)),
                  pl.BlockSpec((1, kk), lambda bi, i: (0, 0)),
                  pl.BlockSpec(wk.shape, lambda bi, i: (0, 0)),
                  pl.BlockSpec(wvt.shape, lambda bi, i: (0, 0)), tab, tab, tab],
        out_specs=[pl.BlockSpec((1, MLA_HEADS, tm, MLA_PAD), lambda bi, i: (bi, 0, i, 0)),
                   pl.BlockSpec((1, MLA_HEADS * MLA_V, tm), lambda bi, i: (bi, 0, i))],
        out_shape=[jax.ShapeDtypeStruct((b, MLA_HEADS, t, MLA_PAD), BF16),
                   jax.ShapeDtypeStruct((b, MLA_HEADS * MLA_V, t), BF16)],
        compiler_params=_cp("parallel", "parallel"),
        name="mla_kv",
    )(proj, proj, g.reshape(1, kk), wk, wvt, *tabs)


def _flash_kernel(qt_ref, k_ref, vt_ref, o_ref, s_scr, p_scr):
    qt = qt_ref[0]
    tq = qt.shape[1]
    nc, _, tk = vt_ref.shape[2:]
    m = jnp.full((1, tq), MASK_VALUE, F32)
    l = jnp.zeros((1, tq), F32)
    acc = jnp.zeros((MLA_V, tq), F32)
    s_scr[0] = jnp.dot(k_ref[0, 0, 0:tk, :], qt, preferred_element_type=F32)
    for c in range(nc + 1):
        if c + 1 < nc:
            s_scr[(c + 1) % 2] = jnp.dot(k_ref[0, 0, (c + 1) * tk:(c + 2) * tk, :], qt, preferred_element_type=F32)
        if c >= 1:
            acc = acc + jnp.dot(vt_ref[0, 0, c - 1], p_scr[(c - 1) % 2], preferred_element_type=F32)
        if c < nc:
            st = s_scr[c % 2]
            m_new = jnp.maximum(m, jnp.max(st, axis=0, keepdims=True))
            alpha = jnp.exp2(m - m_new)
            p = jnp.exp2(st - m_new)
            l = alpha * l + jnp.sum(p, axis=0, keepdims=True)
            p_scr[c % 2] = p.astype(BF16)
            acc = alpha * acc
            m = m_new
    o_ref[0] = (acc * (1.0 / l)).T.astype(o_ref.dtype)


def _flash(qt, k, vt, *, tq):
    b, _, lq = qt.shape
    lk = k.shape[2]
    nc, _, tk = vt.shape[2:]
    return pl.pallas_call(
        _flash_kernel,
        grid=(b, MLA_HEADS, lq // tq),
        in_specs=[pl.BlockSpec((1, MLA_PAD, tq), lambda bi, h, i: (bi, h, i)),
                  pl.BlockSpec((1, 1, lk, MLA_PAD), lambda bi, h, i: (bi, h, 0, 0)),
                  pl.BlockSpec((1, 1, nc, MLA_V, tk), lambda bi, h, i: (bi, h, 0, 0, 0))],
        out_specs=pl.BlockSpec((1, tq, MLA_V), lambda bi, h, i: (bi, i, h)),
        out_shape=jax.ShapeDtypeStruct((b, lq, MLA_HEADS * MLA_V), BF16),
        scratch_shapes=[pltpu.VMEM((2, tk, tq), F32), pltpu.VMEM((2, tk, tq), BF16)],
        compiler_params=_cp("parallel", "parallel", "arbitrary"),
        name="mla_flash",
    )(qt, k, vt)


def _sort_network(n):
    pairs, p = [], 1
    while p < n:
        k = p
        while k >= 1:
            for j in range(k % p, n - k, 2 * k):
                for i in range(min(k, n - j - k)):
                    if (i + j) // (2 * p) == (i + j + k) // (2 * p):
                        pairs.append((i + j, i + j + k))
            k //= 2
        p *= 2
    return pairs


SUBLANES = 8


def _stack_rows(rows):
    n = rows[0].shape[1]
    sub = lax.broadcasted_iota(jnp.int32, (SUBLANES, n), 0)
    out = jnp.broadcast_to(rows[0], (SUBLANES, n))
    for r in range(1, len(rows)):
        out = jnp.where(sub == r, rows[r], out)
    return out


def _column_top(st, k):
    groups = st.shape[0] // SUBLANES
    v = [st[SUBLANES * j:SUBLANES * (j + 1), :] for j in range(groups)]
    for i, j in _sort_network(groups):
        v[i], v[j] = jnp.maximum(v[i], v[j]), jnp.minimum(v[i], v[j])
    vals = []
    for t in range(k):
        mx = jnp.max(v[0], axis=0, keepdims=True)
        vals.append(mx)
        pop = v[0] >= mx
        for j in range(k - 1 - t):
            v[j] = jnp.where(pop, v[j + 1] if j + 1 < groups else MASK_VALUE, v[j])
    return vals


def _pair_top(a, b, k):
    lo = _stack_rows(a[:SUBLANES])
    v = [lo + b[j] for j in range(k)]
    hi = [_stack_rows(a[SUBLANES * g:SUBLANES * (g + 1)]) + b[0] for g in range(1, k // SUBLANES)]
    vals = []
    for t in range(k + 1):
        head = v[0]
        for hgrp in hi:
            head = jnp.maximum(head, hgrp)
        mx = jnp.max(head, axis=0, keepdims=True)
        vals.append(mx)
        pop = v[0] >= mx
        for j in range(min(k, k - t)):
            v[j] = jnp.where(pop, v[j + 1] if j + 1 < k else MASK_VALUE, v[j])
        hi = [jnp.where(hgrp >= mx, MASK_VALUE, hgrp) for hgrp in hi]
    return vals


def _peer_route_kernel(x_ref, g_ref, mod_ref, wq_ref, ks_ref, h_ref, r1_ref, e1_ref, cut_ref, e0_ref):
    h2 = _hmod(x_ref[0], g_ref[...], mod_ref[0], 3).astype(BF16)
    h_ref[...] = h2
    qp = jnp.dot(h2, wq_ref[...], preferred_element_type=F32).astype(BF16)
    k = PEER_TOPK
    for h in range(PEER_HEADS):
        s0 = lax.dot_general(ks_ref[2 * h], qp[:, (2 * h) * PEER_HALF:(2 * h + 1) * PEER_HALF], NT_DIMS,
                             preferred_element_type=F32)
        s1 = lax.dot_general(ks_ref[2 * h + 1], qp[:, (2 * h + 1) * PEER_HALF:(2 * h + 2) * PEER_HALF], NT_DIMS,
                             preferred_element_type=F32)
        a = _column_top(s0, k)
        bb = _column_top(s1, k)
        c = _pair_top(a, bb, k)
        z = jnp.zeros_like(c[0])
        for j in range(k):
            z = z + jnp.exp(c[j] - c[0])
        tau = 0.5 * (c[k - 1] + c[k])
        theta = tau - s0
        rank1 = jnp.full(s1.shape, float(k), F32)
        cut = jnp.zeros(s0.shape, F32)
        for j in range(k):
            rank1 = jnp.where(s1 >= bb[k - 1 - j], float(k - 1 - j), rank1)
            cut = jnp.where(bb[j] >= theta, float(j + 1), cut)
        r1_ref[h] = rank1.astype(r1_ref.dtype)
        e1_ref[h] = jnp.exp(s1 - bb[0]).astype(e1_ref.dtype)
        cut_ref[h] = jnp.where(s0 >= a[k - 1], cut, 0.0)
        e0_ref[h] = jnp.exp(s0 - a[0]) * (1.0 / z)


def _peer_route(x, g, mod, wq, kt, *, tm):
    b, t, d = x.shape
    nt = t // tm
    rt = lambda: pl.BlockSpec((PEER_HEADS, PEER_KEYS, tm), lambda bi, i: (0, 0, bi * nt + i))
    rshape = lambda dt: jax.ShapeDtypeStruct((PEER_HEADS, PEER_KEYS, b * t), dt)
    return pl.pallas_call(
        _peer_route_kernel,
        grid=(b, nt),
        in_specs=[pl.BlockSpec((1, tm, d), lambda bi, i: (bi, i, 0)),
                  pl.BlockSpec((1, d), lambda bi, i: (0, 0)),
                  pl.BlockSpec((1, 6, d), lambda bi, i: (bi, 0, 0)),
                  pl.BlockSpec(wq.shape, lambda bi, i: (0, 0)),
                  pl.BlockSpec(kt.shape, lambda bi, i: (0, 0, 0))],
        out_specs=[pl.BlockSpec((tm, d), lambda bi, i: (bi * nt + i, 0)), rt(), rt(), rt(), rt()],
        out_shape=[jax.ShapeDtypeStruct((b * t, d), BF16), rshape(BF16), rshape(BF16), rshape(F32), rshape(F32)],
        compiler_params=_cp("parallel", "parallel"),
        name="peer_route",
    )(x, g.reshape(1, d), mod, wq, kt)


def _gelu_tanh(a):
    return 0.5 * a * (1.0 + jnp.tanh(math.sqrt(2.0 / math.pi) * (a + 0.044715 * (a * a * a))))


def _peer_expert_kernel(*refs, final, ne):
    if final:
        (h_ref, hn_ref, u0_ref, un_ref, vt_ref, r1_ref, e1_ref, cut_ref, e0_ref, x_ref, mod_ref, fn_ref, o_ref,
         acc_scr, at_scr) = refs
    else:
        (h_ref, hn_ref, u0_ref, un_ref, vt_ref, r1_ref, e1_ref, cut_ref, e0_ref, x_ref, mod_ref, o_ref,
         acc_scr, at_scr) = refs
    j = pl.program_id(2)
    first_tile = (pl.program_id(0) == 0) & (pl.program_id(1) == 0)

    def pre_activation(u_ref, tok_ref):
        return lax.dot_general(u_ref[...], tok_ref[...], NT_DIMS, preferred_element_type=F32).astype(BF16)

    @pl.when(j == 0)
    def _():
        acc_scr[...] = jnp.zeros_like(acc_scr)

    @pl.when((j == 0) & first_tile)
    def _():
        at_scr[0] = pre_activation(u0_ref, h_ref)

    def body(cur, nxt, tok_ref):
        at_scr[nxt] = pre_activation(un_ref, tok_ref)
        ws = []
        for il in range(cut_ref.shape[1]):
            gate = None
            for h in range(PEER_HEADS):
                cut = cut_ref[h, il:il + 1, :].astype(BF16)
                e0 = e0_ref[h, il:il + 1, :].astype(BF16)
                term = jnp.where(r1_ref[h] < cut, e0 * e1_ref[h], jnp.zeros((), BF16))
                gate = term if gate is None else gate + term
            ws.append(gate * _gelu_tanh(at_scr[cur, il * PEER_KEYS:(il + 1) * PEER_KEYS, :]))
        acc_scr[...] += jnp.dot(vt_ref[0], jnp.concatenate(ws, axis=0), preferred_element_type=F32)

    assert ne % 2 == 0

    @pl.when(j % 2 == 0)
    def _():
        body(0, 1, h_ref)

    @pl.when((j % 2 == 1) & (j != ne - 1))
    def _():
        body(1, 0, h_ref)

    @pl.when(j == ne - 1)
    def _():
        body(1, 0, hn_ref)

    @pl.when(j == pl.num_programs(2) - 1)
    def _():
        y = x_ref[0] + mod_ref[0, 5:6, :] * acc_scr[...].T
        if final:
            y = _rms_rows(y, fn_ref[...])
        o_ref[0] = y


def _peer_expert(h2, route, u, vt, x, mod, final_gain, *, tt, ec):
    b, t, d = x.shape
    nt = t // tt
    ne = u.shape[0] // ec
    r1, e1, cut, e0 = route
    once = dict(pipeline_mode=pl.Buffered(1))
    rt = lambda: pl.BlockSpec((PEER_HEADS, PEER_KEYS, tt), lambda bi, i, j: (0, 0, bi * nt + i), **once)
    rj = lambda: pl.BlockSpec((PEER_HEADS, ec // PEER_KEYS, tt), lambda bi, i, j: (0, j, bi * nt + i))
    in_specs = [pl.BlockSpec((tt, d), lambda bi, i, j: (bi * nt + i, 0), **once),
                pl.BlockSpec((tt, d), lambda bi, i, j: (jnp.minimum(bi * nt + i + 1, b * nt - 1), 0), **once),
                pl.BlockSpec((ec, d), lambda bi, i, j: (0, 0), **once),
                pl.BlockSpec((ec, d), lambda bi, i, j: ((j + 1) % ne, 0)),
                pl.BlockSpec((1, d, ec), lambda bi, i, j: (j, 0, 0)),
                rt(), rt(), rj(), rj(),
                pl.BlockSpec((1, tt, d), lambda bi, i, j: (bi, i, 0), **once),
                pl.BlockSpec((1, 6, d), lambda bi, i, j: (bi, 0, 0))]
    args = [h2, h2, u, u, vt, r1, e1, cut, e0, x, mod]
    if final_gain is not None:
        in_specs.append(pl.BlockSpec((1, d), lambda bi, i, j: (0, 0)))
        args.append(final_gain.reshape(1, d))
    return pl.pallas_call(
        functools.partial(_peer_expert_kernel, final=final_gain is not None, ne=ne),
        grid=(b, nt, ne),
        in_specs=in_specs,
        out_specs=pl.BlockSpec((1, tt, d), lambda bi, i, j: (bi, i, 0)),
        out_shape=jax.ShapeDtypeStruct((b, t, d), F32),
        scratch_shapes=[pltpu.VMEM((d, tt), F32), pltpu.VMEM((2, ec, tt), BF16)],
        compiler_params=_cp("arbitrary", "arbitrary", "arbitrary"),
        name="peer_expert",
    )(*args)


def _peer(x, g, mod, wq, kt, u, vt, final_gain, *, tm, tt, ec):
    out = _peer_route(x, g, mod, wq, kt, tm=tm)
    return _peer_expert(out[0], out[1:], u, vt, x, mod, final_gain, tt=tt, ec=ec)


def _lat_tile(t, want):
    while t % want:
        want //= 2
    return want


def kernel(x, c, ctx, c_ctx, w_mod, b_mod, norm1, norm2, even_w_in, even_ret_gn, even_w_out, odd_w_in, odd_sink, odd_mla_q_norm, odd_mla_kv_norm, odd_mla_w_uq, odd_mla_w_ukv, odd_w_out, peer_w_q, peer_sub_keys, peer_u, peer_v, final_norm):
    b, s_len, d = x.shape
    n_ctx = ctx.shape[1]
    depth = w_mod.shape[0]
    assert depth == 2 and even_w_in.shape[0] == 1 and odd_w_in.shape[0] == 1
    tm_l = _lat_tile(s_len, 1024)
    tm_c = n_ctx

    rows = 8 * ((b + 1 + 7) // 8)
    cond = jnp.zeros((rows, d), F32).at[:b].set(c).at[b].set(c_ctx)
    mods = _adaln(cond, w_mod, b_mod).reshape(depth, rows, 6, d)
    mod_l = [mods[l, :b] for l in range(depth)]
    mod_c = [jnp.broadcast_to(mods[l, b:b + 1], (b, 6, d)) for l in range(depth)]

    x_lat, x_ctx = x, ctx

    def peer_params(layer):
        kt = peer_sub_keys[layer].reshape(2 * PEER_HEADS, PEER_KEYS, PEER_HALF).astype(BF16)
        n_exp = peer_v.shape[1]
        vt = jnp.transpose(peer_v[layer].astype(BF16).reshape(n_exp // EXPERT_CHUNK, EXPERT_CHUNK, d), (0, 2, 1))
        return (peer_w_q[layer].astype(BF16), kt, peer_u[layer].astype(BF16), vt)

    w_in = even_w_in[0].astype(BF16)
    w_out = even_w_out[0].astype(BF16)
    p_l = _proj(x_lat, norm1[0], mod_l[0], w_in, si=0, tm=tm_l, tn=1024, name="even_proj_lat")
    p_c = _proj(x_ctx, norm1[0], mod_c[0], w_in, si=0, tm=tm_c, tn=512, name="even_proj_ctx")
    fm_l = _fourier_lat(p_l[..., :FOURIER_WIDTH])
    fm_c = _fourier_small(p_c[..., :FOURIER_WIDTH])
    tabs_l = _rope_tables(s_len, RET_DIM, LANES)
    tabs_c = _rope_tables(n_ctx, RET_DIM, LANES, identity=True)
    zero = jnp.zeros((b, RET_HEADS, RET_DIM, RET_DIM), F32)
    gn = even_ret_gn[0]
    rf_c, st_f = _retention(p_c, tabs_c, gn, zero, None, reverse=False, gate_blk=4)
    r_c, st_b = _retention(p_c, tabs_c, gn, zero, rf_c, reverse=True, gate_blk=5)
    rf_l, _ = _retention(p_l, tabs_l, gn, st_f, None, reverse=False, gate_blk=4)
    r_l, _ = _retention(p_l, tabs_l, gn, st_b, rf_l, reverse=True, gate_blk=5)
    x_lat = _outproj(fm_l, r_l, w_out, x_lat, mod_l[0], tm=tm_l, tn=512, name="even_out_lat")
    x_ctx = _outproj(fm_c, r_c, w_out, x_ctx, mod_c[0], tm=tm_c, tn=512, name="even_out_ctx")
    pp = peer_params(0)
    tt_l = _lat_tile(s_len, 512)
    x_lat = _peer(x_lat, norm2[0], mod_l[0], *pp, None, tm=256, tt=tt_l, ec=EXPERT_CHUNK)
    x_ctx = _peer(x_ctx, norm2[0], mod_c[0], *pp, None, tm=256, tt=n_ctx, ec=EXPERT_CHUNK)

    wi = odd_w_in[0]
    o_sq, o_sk, o_sv = 0, SWA_Q_HEADS * SWA_HEAD_DIM, (SWA_Q_HEADS + SWA_KV_HEADS) * SWA_HEAD_DIM
    o_cq = o_sv + SWA_KV_HEADS * SWA_HEAD_DIM
    o_ckv = o_cq + MLA_Q_RANK
    o_kr = o_ckv + MLA_KV_RANK
    w_in = jnp.concatenate([wi[:, o_sq:o_sk], wi[:, o_cq:o_ckv], wi[:, o_ckv:o_kr], wi[:, o_sk:o_sv], wi[:, o_sv:o_cq],
                            wi[:, o_kr:], jnp.zeros((d, LANES - MLA_ROPE), wi.dtype)], axis=1).astype(BF16)
    n_odd = w_in.shape[1]
    q_blk, cq_blk, ckv_blk = 0, (o_sk - o_sq) // MLA_Q_RANK, (o_sk - o_sq + MLA_Q_RANK) // MLA_KV_RANK
    k_blk = (o_sk - o_sq + MLA_Q_RANK + MLA_KV_RANK) // LANES
    v_blk, kr_blk = k_blk + 1, k_blk + 2
    tm_o = _lat_tile(s_len, 512)
    p_l = _proj(x_lat, norm1[1], mod_l[1], w_in, si=0, tm=tm_o, tn=n_odd, name="odd_proj_lat")
    p_c = _proj(x_ctx, norm1[1], mod_c[1], w_in, si=0, tm=tm_c, tn=n_odd, name="odd_proj_ctx")
    tabs_swa = _rope_tables(s_len, SWA_HEAD_DIM, LANES)
    swa = _swa(p_l, p_c, tabs_swa, odd_sink[0], q_blk=q_blk, k_blk=k_blk, v_blk=v_blk)
    wuq = odd_mla_w_uq[0].reshape(MLA_Q_RANK, MLA_HEADS, MLA_QK_DIM)
    wuq = jnp.pad(wuq, ((0, 0), (0, 0), (0, MLA_PAD - MLA_QK_DIM))).reshape(MLA_Q_RANK, MLA_HEADS * MLA_PAD)
    wukv = odd_mla_w_ukv[0].reshape(MLA_KV_RANK, MLA_HEADS, MLA_NOPE + MLA_V)
    wk = wukv[:, :, :MLA_NOPE].reshape(MLA_KV_RANK, -1).astype(BF16)
    wvt = wukv[:, :, MLA_NOPE:].reshape(MLA_KV_RANK, -1).T.astype(BF16)
    tabs_mla = _pad_rope_tables(_rope_tables(s_len, MLA_ROPE, MLA_ROPE), LANES)
    tabs_id = _rope_tables(n_ctx, MLA_ROPE, LANES, identity=True)
    qt = _mla_q(p_l, odd_mla_q_norm[0], wuq.T.astype(BF16), tuple(a.T for a in tabs_mla), x_blk=cq_blk, tm=tm_o)
    k_l, vt_l = _mla_kv(p_l, odd_mla_kv_norm[0], wk, wvt, tabs_mla, x_blk=ckv_blk, kr_blk=kr_blk, tm=tm_o)
    k_c, vt_c = _mla_kv(p_c, odd_mla_kv_norm[0], wk, wvt, tabs_id, x_blk=ckv_blk, kr_blk=kr_blk, tm=tm_c)
    k_all = jnp.concatenate([k_c, k_l], axis=2)
    lk = n_ctx + s_len
    tk = max(t for t in (1280, 1024, 768, 512, 256, 128) if lk % t == 0)
    vt_all = jnp.concatenate([vt_c, vt_l], axis=2).reshape(b, MLA_HEADS, MLA_V, lk // tk, tk)
    vt_all = jnp.transpose(vt_all, (0, 1, 3, 2, 4))
    mla = _flash(qt, k_all, vt_all, tq=_lat_tile(s_len, 512))
    x_lat = _outproj(swa, mla, odd_w_out[0].astype(BF16), x_lat, mod_l[1], tm=tm_l, tn=512, name="odd_out_lat")
    pp = peer_params(1)
    return _peer(x_lat, norm2[1], mod_l[1], *pp, final_norm, tm=256, tt=tt_l, ec=EXPERT_CHUNK)
```

```python
import functools
import math

import numpy as np
import jax
import jax.numpy as jnp
from jax import lax
from jax.experimental import pallas as pl
from jax.experimental.pallas import tpu as pltpu

F32 = jnp.float32
BF16 = jnp.bfloat16

GRID_W = 64
NORM_EPS = 1e-6
ROPE_BASE = 10000.0
MASK_VALUE = -1e30
LANES = 128

FOURIER_GROUPS = 4
FOURIER_GROUP_DIM = 256
FOURIER_WIDTH = FOURIER_GROUPS * FOURIER_GROUP_DIM
RET_HEADS = 8
RET_DIM = 128
RET_CHUNK = 128
RET_WIDTH = RET_HEADS * RET_DIM

SWA_Q_HEADS = 16
SWA_KV_HEADS = 2
SWA_GROUP = SWA_Q_HEADS // SWA_KV_HEADS
SWA_HEAD_DIM = 64
SWA_WINDOW = 128
SWA_BLOCK = 128
MLA_HEADS = 8
MLA_Q_RANK = 512
MLA_KV_RANK = 256
MLA_NOPE = 128
MLA_ROPE = 64
MLA_V = 128
MLA_QK_DIM = MLA_NOPE + MLA_ROPE
MLA_PAD = 2 * LANES

PEER_HEADS = 8
PEER_KEYS = 128
PEER_TOPK = 16
PEER_HALF = 128
EXPERT_CHUNK = 1024

VMEM_LIMIT = 56 * 1024 * 1024
EXPERT_VMEM_LIMIT = 60 * 1024 * 1024


def _cp(*sem, vmem=VMEM_LIMIT):
    return pltpu.CompilerParams(dimension_semantics=sem, vmem_limit_bytes=vmem)


def _adaln_kernel(c_ref, w_ref, b_ref, o_ref):
    c = c_ref[...]
    s = c * jax.nn.sigmoid(c)
    o_ref[0] = jnp.dot(s, w_ref[0], preferred_element_type=F32) + b_ref[0]


def _adaln(cond, w_mod, b_mod):
    depth, d, n = w_mod.shape
    r = cond.shape[0]
    tn = 1024
    return pl.pallas_call(
        _adaln_kernel,
        grid=(depth, n // tn),
        in_specs=[pl.BlockSpec((r, d), lambda l, j: (0, 0)),
                  pl.BlockSpec((1, d, tn), lambda l, j: (l, 0, j)),
                  pl.BlockSpec((1, 1, tn), lambda l, j: (l, 0, j))],
        out_specs=pl.BlockSpec((1, r, tn), lambda l, j: (l, 0, j)),
        out_shape=jax.ShapeDtypeStruct((depth, r, n), F32),
        compiler_params=_cp("parallel", "parallel"),
        name="adaln",
    )(cond, w_mod, b_mod.reshape(depth, 1, n))


def _hmod(x, g, mod, si):
    ms = jnp.mean(x * x, axis=-1, keepdims=True)
    y = x * lax.rsqrt(ms + NORM_EPS) * g
    return y * (1.0 + mod[si + 1:si + 2, :]) + mod[si:si + 1, :]


def _proj_kernel(x_ref, g_ref, mod_ref, w_ref, o_ref, h_scr, *, si):
    @pl.when(pl.program_id(2) == 0)
    def _():
        h_scr[...] = _hmod(x_ref[0], g_ref[...], mod_ref[0], si).astype(BF16)

    o_ref[0] = jnp.dot(h_scr[...], w_ref[...], preferred_element_type=F32).astype(o_ref.dtype)


def _proj(x, g, mod, w, *, si, tm, tn, name):
    b, t, d = x.shape
    n = w.shape[1]
    return pl.pallas_call(
        functools.partial(_proj_kernel, si=si),
        grid=(b, t // tm, n // tn),
        in_specs=[pl.BlockSpec((1, tm, d), lambda bi, i, j: (bi, i, 0)),
                  pl.BlockSpec((1, d), lambda bi, i, j: (0, 0)),
                  pl.BlockSpec((1, 6, d), lambda bi, i, j: (bi, 0, 0)),
                  pl.BlockSpec((d, tn), lambda bi, i, j: (0, j))],
        out_specs=pl.BlockSpec((1, tm, tn), lambda bi, i, j: (bi, i, j)),
        out_shape=jax.ShapeDtypeStruct((b, t, n), BF16),
        scratch_shapes=[pltpu.VMEM((tm, d), BF16)],
        compiler_params=_cp("parallel", "parallel", "arbitrary"),
        name=name,
    )(x, g.reshape(1, d), mod, w)


def _outproj_kernel(a_ref, b_ref, w_ref, x_ref, mod_ref, o_ref):
    ka = a_ref.shape[2]
    acc = jnp.dot(a_ref[0], w_ref[:ka, :], preferred_element_type=F32)
    acc = acc + jnp.dot(b_ref[0], w_ref[ka:, :], preferred_element_type=F32)
    o_ref[0] = x_ref[0] + mod_ref[0, 2:3, :] * acc


def _outproj(a, b2, w, x, mod, *, tm, tn, name):
    b, t, d = x.shape
    ka, kb = a.shape[2], b2.shape[2]
    return pl.pallas_call(
        _outproj_kernel,
        grid=(b, t // tm, d // tn),
        in_specs=[pl.BlockSpec((1, tm, ka), lambda bi, i, j: (bi, i, 0)),
                  pl.BlockSpec((1, tm, kb), lambda bi, i, j: (bi, i, 0)),
                  pl.BlockSpec((ka + kb, tn), lambda bi, i, j: (0, j)),
                  pl.BlockSpec((1, tm, tn), lambda bi, i, j: (bi, i, j)),
                  pl.BlockSpec((1, 6, tn), lambda bi, i, j: (bi, 0, j))],
        out_specs=pl.BlockSpec((1, tm, tn), lambda bi, i, j: (bi, i, j)),
        out_shape=jax.ShapeDtypeStruct((b, t, d), F32),
        compiler_params=_cp("parallel", "parallel", "arbitrary"),
        name=name,
    )(a, b2, w, x, mod)


def _rope_tables(n_pos, d_rot, width, identity=False):
    q = d_rot // 4
    if identity:
        return (jnp.ones((n_pos, width), F32), jnp.zeros((n_pos, width), F32), jnp.zeros((n_pos, width), F32))
    d_axis = d_rot // 2
    inv = ROPE_BASE ** (-jnp.arange(0, d_axis, 2, dtype=F32) / d_axis)
    t = jnp.arange(n_pos)
    row = (t // GRID_W).astype(F32)
    col = (t % GRID_W).astype(F32)
    ar = row[:, None] * inv[None, :]
    ac = col[:, None] * inv[None, :]
    ang = jnp.concatenate([ar, ar, ac, ac], axis=-1)
    cos, sin = jnp.cos(ang), jnp.sin(ang)
    quarter = np.arange(d_rot) // q
    up = jnp.asarray((quarter % 2 == 1).astype(np.float32))
    dn = jnp.asarray((quarter % 2 == 0).astype(np.float32))
    sin_p, sin_m = sin * up, -sin * dn
    reps = width // d_rot
    return tuple(jnp.tile(a, (1, reps)) for a in (cos, sin_p, sin_m))


def _pad_rope_tables(tabs, width):
    cos, sp, sm = tabs
    pad = width - cos.shape[1]
    return (jnp.pad(cos, ((0, 0), (0, pad)), constant_values=1.0), jnp.pad(sp, ((0, 0), (0, pad))),
            jnp.pad(sm, ((0, 0), (0, pad))))


def _rope(t, cos, sp, sm, q):
    w = t.shape[-1]
    return t * cos + pltpu.roll(t, q, 1) * sp + pltpu.roll(t, w - q, 1) * sm


def _dft_mats(n):
    k = np.arange(n)
    ang = 2.0 * np.pi * ((k[:, None] * k[None, :]) % n) / n
    return np.cos(ang), np.sin(ang)


def _fourier_a_kernel(u_ref, fa_ref, tc_ref, ts_ref, o_ref):
    l1 = u_ref.shape[1]
    t1 = jnp.dot(fa_ref[...], u_ref[0], preferred_element_type=F32)
    tr, ti = t1[:l1], t1[l1:]
    reps = tr.shape[1] // LANES
    c = jnp.concatenate([tc_ref[0]] * reps, axis=1)
    s = jnp.concatenate([ts_ref[0]] * reps, axis=1)
    o_ref[0, 0] = (tr * c + ti * s).astype(o_ref.dtype)
    o_ref[0, 1] = (ti * c - tr * s).astype(o_ref.dtype)


def _fourier_b_kernel(t_ref, fb_ref, cs_ref, o_ref):
    l2 = t_ref.shape[2]
    t2 = jnp.concatenate([t_ref[0, 0], t_ref[0, 1]], axis=0)
    v = jnp.dot(fb_ref[...], t2, preferred_element_type=F32).astype(BF16)
    vr, vi = v[:l2], v[l2:]
    outs = []
    for g in range(FOURIER_GROUPS):
        sl = slice(g * FOURIER_GROUP_DIM, (g + 1) * FOURIER_GROUP_DIM)
        vg = jnp.concatenate([vr[:, sl], vi[:, sl]], axis=1)
        outs.append(jnp.dot(vg, cs_ref[...], preferred_element_type=F32))
    o_ref[0] = jnp.concatenate(outs, axis=1).astype(o_ref.dtype)


def _fourier_lat(u):
    b, l, w = u.shape
    l1 = int(round(math.sqrt(l)))
    l2 = l // l1
    assert l1 * l2 == l and l1 % 16 == 0 and l2 % 16 == 0
    ca, sa = _dft_mats(l1)
    fa = jnp.asarray(np.concatenate([ca, -sa], axis=0), BF16)
    kk = np.arange(l1)[None, :, None] * np.arange(l2)[:, None, None]
    ang = 2.0 * np.pi * (kk % l) / l
    tc = jnp.asarray(np.broadcast_to(np.cos(ang), (l2, l1, LANES)), F32)
    ts = jnp.asarray(np.broadcast_to(np.sin(ang), (l2, l1, LANES)), F32)
    t2 = pl.pallas_call(
        _fourier_a_kernel,
        grid=(b, l2),
        in_specs=[pl.BlockSpec((1, l1, w), lambda bi, j: (bi, 0, j)),
                  pl.BlockSpec((2 * l1, l1), lambda bi, j: (0, 0)),
                  pl.BlockSpec((1, l1, LANES), lambda bi, j: (j, 0, 0)),
                  pl.BlockSpec((1, l1, LANES), lambda bi, j: (j, 0, 0))],
        out_specs=pl.BlockSpec((1, 2, l1, w), lambda bi, j: (bi, 0, 0, j)),
        out_shape=jax.ShapeDtypeStruct((b, 2, l1, l2 * w), BF16),
        compiler_params=_cp("parallel", "parallel"),
        name="fourier_a",
    )(u.reshape(b, l1, l2 * w), fa, tc, ts)
    cb, sb = _dft_mats(l2)
    scale = 1.0 / math.sqrt(l * FOURIER_GROUP_DIM)
    fb = jnp.asarray(np.block([[cb, sb], [-sb, cb]]) * scale, BF16)
    cc, sc = _dft_mats(FOURIER_GROUP_DIM)
    cs = jnp.asarray(np.concatenate([cc, sc], axis=0), BF16)
    y = pl.pallas_call(
        _fourier_b_kernel,
        grid=(b, l1),
        in_specs=[pl.BlockSpec((1, 2, l2, w), lambda bi, j: (bi, 0, j, 0)),
                  pl.BlockSpec((2 * l2, 2 * l2), lambda bi, j: (0, 0)),
                  pl.BlockSpec((2 * FOURIER_GROUP_DIM, FOURIER_GROUP_DIM), lambda bi, j: (0, 0))],
        out_specs=pl.BlockSpec((1, l2, w), lambda bi, j: (bi, 0, j)),
        out_shape=jax.ShapeDtypeStruct((b, l2, l1 * w), BF16),
        compiler_params=_cp("parallel", "parallel"),
        name="fourier_b",
    )(t2.reshape(b, 2, l1 * l2, w), fb, cs)
    return y.reshape(b, l, w)


def _fourier_small_kernel(u_ref, cl_ref, sl_ref, cs_ref, o_ref):
    gd = FOURIER_GROUP_DIM
    outs = []
    for g in range(FOURIER_GROUPS):
        ug = u_ref[0][:, g * gd:(g + 1) * gd]
        pq = jnp.dot(ug, cs_ref[...], preferred_element_type=F32).astype(BF16)
        y = jnp.dot(cl_ref[...], pq[:, :gd], preferred_element_type=F32)
        y = y - jnp.dot(sl_ref[...], pq[:, gd:], preferred_element_type=F32)
        outs.append(y)
    o_ref[0] = jnp.concatenate(outs, axis=1).astype(o_ref.dtype)


def _fourier_small(u):
    b, l, w = u.shape
    cl, sl = _dft_mats(l)
    scale = 1.0 / math.sqrt(l * FOURIER_GROUP_DIM)
    cc, sc = _dft_mats(FOURIER_GROUP_DIM)
    cs = jnp.asarray(np.concatenate([cc, sc], axis=1), BF16)
    return pl.pallas_call(
        _fourier_small_kernel,
        grid=(b,),
        in_specs=[pl.BlockSpec((1, l, w), lambda bi: (bi, 0, 0)),
                  pl.BlockSpec((l, l), lambda bi: (0, 0)),
                  pl.BlockSpec((l, l), lambda bi: (0, 0)),
                  pl.BlockSpec((FOURIER_GROUP_DIM, 2 * FOURIER_GROUP_DIM), lambda bi: (0, 0))],
        out_specs=pl.BlockSpec((1, l, w), lambda bi: (bi, 0, 0)),
        out_shape=jax.ShapeDtypeStruct((b, l, w), BF16),
        compiler_params=_cp("parallel"),
        name="fourier_small",
    )(u, jnp.asarray(cl * scale, BF16), jnp.asarray(sl * scale, BF16), cs)


def _ret_consts(reverse):
    c = RET_CHUNK
    lg = np.log(1.0 - np.exp2(-5.0 - np.arange(RET_HEADS, dtype=np.float32))).astype(np.float32)
    pos = np.arange(c, dtype=np.float32)
    diff = pos[:, None] - pos[None, :]
    if reverse:
        diff = -diff
    decay = np.where(diff >= 0, np.exp(lg[:, None, None] * np.maximum(diff, 0.0)), 0.0)
    if reverse:
        zeta = np.exp(lg[None, :] * pos[:, None])
        xi = np.exp(lg[None, :] * (c - pos[:, None]))
    else:
        zeta = np.exp(lg[None, :] * (c - 1.0 - pos[:, None]))
        xi = np.exp(lg[None, :] * (pos[:, None] + 1.0))
    wide = lambda a: np.repeat(a, RET_DIM, axis=1)
    chunk_decay = [float(v) for v in np.exp(lg * c)]
    return (jnp.asarray(decay, F32), jnp.asarray(wide(zeta), F32), jnp.asarray(wide(xi), F32), chunk_decay)


def _ret_kernel(*refs, chunk_decay, has_prev):
    if has_prev:
        (q_ref, k_ref, v_ref, g_ref, cos_ref, sp_ref, sm_ref, dec_ref, zeta_ref, xi_ref, gn_ref, s0_ref, prev_ref,
         o_ref, sf_ref, st_scr) = refs
    else:
        (q_ref, k_ref, v_ref, g_ref, cos_ref, sp_ref, sm_ref, dec_ref, zeta_ref, xi_ref, gn_ref, s0_ref,
         o_ref, sf_ref, st_scr) = refs
        prev_ref = None
    c = pl.program_id(1)

    @pl.when(c == 0)
    def _():
        st_scr[...] = s0_ref[0]

    cos, sp, sm = cos_ref[...], sp_ref[...], sm_ref[...]
    outs = []
    for h in range(RET_HEADS):
        sl = slice(h * RET_DIM, (h + 1) * RET_DIM)
        q = _rope(q_ref[0][:, sl].astype(F32), cos, sp, sm, RET_DIM // 4)
        k = _rope(k_ref[0][:, sl].astype(F32), cos, sp, sm, RET_DIM // 4) * (RET_DIM ** -0.5)
        v = v_ref[0][:, sl].astype(F32)
        kt = k.T.astype(BF16)
        scores = jnp.dot(q.astype(BF16), kt, preferred_element_type=F32) * dec_ref[h]
        inner = jnp.dot(scores.astype(BF16), v.astype(BF16), preferred_element_type=F32)
        state = st_scr[h]
        cross = jnp.dot((q * xi_ref[:, sl]).astype(BF16), state.astype(BF16), preferred_element_type=F32)
        o = inner + cross
        st_scr[h] = state * chunk_decay[h] + jnp.dot(kt, (v * zeta_ref[:, sl]).astype(BF16),
                                                      preferred_element_type=F32)
        y = o * lax.rsqrt(jnp.mean(o * o, axis=-1, keepdims=True) + NORM_EPS) * gn_ref[:, sl]
        gate = g_ref[0][:, sl].astype(F32)
        outs.append(y * (gate * jax.nn.sigmoid(gate)))
    res = jnp.concatenate(outs, axis=1)
    if prev_ref is not None:
        res = res + prev_ref[0].astype(F32)
    o_ref[0] = res.astype(o_ref.dtype)

    @pl.when(c == pl.num_programs(1) - 1)
    def _():
        sf_ref[0] = st_scr[...]


def _retention(proj, tabs, gn, state0, prev, *, reverse, gate_blk):
    b, t, _ = proj.shape
    w = RET_WIDTH
    n = t // RET_CHUNK
    decay, zeta, xi, chunk_decay = _ret_consts(reverse)
    pos = (lambda ci: n - 1 - ci) if reverse else (lambda ci: ci)
    blk = lambda col: pl.BlockSpec((1, RET_CHUNK, w), lambda bi, ci: (bi, pos(ci), col))
    tab = pl.BlockSpec((RET_CHUNK, LANES), lambda bi, ci: (pos(ci), 0))
    const2 = pl.BlockSpec((RET_CHUNK, w), lambda bi, ci: (0, 0))
    st = pl.BlockSpec((1, RET_HEADS, RET_DIM, RET_DIM), lambda bi, ci: (bi, 0, 0, 0))
    in_specs = [blk(1), blk(2), blk(3), blk(gate_blk), tab, tab, tab,
                pl.BlockSpec((RET_HEADS, RET_CHUNK, RET_CHUNK), lambda bi, ci: (0, 0, 0)), const2, const2,
                pl.BlockSpec((1, w), lambda bi, ci: (0, 0)), st]
    args = [proj, proj, proj, proj, *tabs, decay, zeta, xi, gn.reshape(1, w), state0]
    if prev is not None:
        in_specs.append(pl.BlockSpec((1, RET_CHUNK, w), lambda bi, ci: (bi, pos(ci), 0)))
        args.append(prev)
    return pl.pallas_call(
        functools.partial(_ret_kernel, chunk_decay=chunk_decay, has_prev=prev is not None),
        grid=(b, n),
        in_specs=in_specs,
        out_specs=[pl.BlockSpec((1, RET_CHUNK, w), lambda bi, ci: (bi, pos(ci), 0)), st],
        out_shape=[jax.ShapeDtypeStruct((b, t, w), BF16),
                   jax.ShapeDtypeStruct((b, RET_HEADS, RET_DIM, RET_DIM), F32)],
        scratch_shapes=[pltpu.VMEM((RET_HEADS, RET_DIM, RET_DIM), F32)],
        compiler_params=_cp("parallel", "arbitrary"),
        name="retention_bwd" if reverse else "retention_fwd",
    )(*args)


def _swa_kernel(q_ref, kc_ref, vc_ref, kl_ref, km_ref, kr_ref, vl_ref, vm_ref, vr_ref,
                cq_ref, spq_ref, smq_ref, cl_ref, spl_ref, sml_ref, cm_ref, spm_ref, smm_ref,
                cr_ref, spr_ref, smr_ref, sink_ref, o_ref, *, seq_len):
    i = pl.program_id(1)
    qd = SWA_HEAD_DIM // 4
    hd = SWA_HEAD_DIM
    cq, spq, smq = cq_ref[...], spq_ref[...], smq_ref[...]
    rk = lambda kref, c, sp, sm: _rope(kref[0].astype(F32), c[...], sp[...], sm[...], qd)
    k_all = jnp.concatenate([kc_ref[0].astype(F32), rk(kl_ref, cl_ref, spl_ref, sml_ref),
                             rk(km_ref, cm_ref, spm_ref, smm_ref), rk(kr_ref, cr_ref, spr_ref, smr_ref)], axis=0)
    v_all = jnp.concatenate([vc_ref[0], vl_ref[0], vm_ref[0], vr_ref[0]], axis=0).astype(F32)
    vt = v_all.T
    n_ctx = kc_ref.shape[1]
    n_keys = k_all.shape[0]
    n_loc = 3 * SWA_BLOCK
    gw = SWA_GROUP * SWA_BLOCK
    cidx = lax.broadcasted_iota(jnp.int32, (n_loc, gw), 0)
    r = lax.broadcasted_iota(jnp.int32, (n_loc, gw), 1) % SWA_BLOCK
    kpos = (i - 1) * SWA_BLOCK + cidx
    valid = (jnp.abs(SWA_BLOCK + r - cidx) <= SWA_WINDOW) & (kpos >= 0) & (kpos < seq_len)
    klane = lax.broadcasted_iota(jnp.int32, (n_keys, LANES), 1)
    vrow = lax.broadcasted_iota(jnp.int32, (LANES, n_keys), 0)
    qt_blk = [(_rope(q_ref[0][:, blk * LANES:(blk + 1) * LANES].astype(F32), cq, spq, smq, qd) * (hd ** -0.5)).T
              for blk in range(SWA_Q_HEADS // 2)]
    zeros = jnp.zeros((hd, gw), F32)
    outs = []
    for kv in range(SWA_KV_HEADS):
        pieces = []
        for h in range(kv * SWA_GROUP, (kv + 1) * SWA_GROUP):
            pieces.append(qt_blk[h // 2][(h % 2) * hd:(h % 2 + 1) * hd, :])
        qg = jnp.concatenate(pieces, axis=1)
        qg = jnp.concatenate([qg, zeros] if kv == 0 else [zeros, qg], axis=0).astype(BF16)
        in_kv = (klane < hd) if kv == 0 else (klane >= hd)
        st = jnp.dot(jnp.where(in_kv, k_all, 0.0).astype(BF16), qg, preferred_element_type=F32)
        st = jnp.concatenate([st[:n_ctx], jnp.where(valid, st[n_ctx:], MASK_VALUE)], axis=0)
        sk = sink_ref[kv]
        m = jnp.maximum(jnp.max(st, axis=0, keepdims=True), sk)
        e = jnp.exp(st - m)
        p = e * (1.0 / (jnp.sum(e, axis=0, keepdims=True) + jnp.exp(sk - m)))
        v_kv = jnp.where((vrow < hd) if kv == 0 else (vrow >= hd), vt, 0.0).astype(BF16)
        ot = jnp.dot(v_kv, p.astype(BF16), preferred_element_type=F32)[kv * hd:(kv + 1) * hd, :]
        stacked = jnp.concatenate([ot[:, g * SWA_BLOCK:(g + 1) * SWA_BLOCK] for g in range(SWA_GROUP)], axis=0)
        outs.append(stacked.T)
    o_ref[0] = jnp.concatenate(outs, axis=1).astype(o_ref.dtype)


def _swa(proj_l, proj_c, tabs, sink, *, q_blk, k_blk, v_blk):
    b, l, _ = proj_l.shape
    n_ctx = proj_c.shape[1]
    nb = l // SWA_BLOCK
    qw = SWA_Q_HEADS * SWA_HEAD_DIM
    left = lambda i: jnp.maximum(i - 1, 0)
    right = lambda i: jnp.minimum(i + 1, nb - 1)
    kvspec = lambda col, f: pl.BlockSpec((1, SWA_BLOCK, LANES), lambda bi, i: (bi, f(i), col))
    ctxspec = lambda col: pl.BlockSpec((1, n_ctx, LANES), lambda bi, i: (bi, 0, col))
    tab = lambda f: pl.BlockSpec((SWA_BLOCK, LANES), lambda bi, i: (f(i), 0))
    ident = lambda i: i
    in_specs = [pl.BlockSpec((1, SWA_BLOCK, qw), lambda bi, i: (bi, i, q_blk)),
                ctxspec(k_blk), ctxspec(v_blk),
                kvspec(k_blk, left), kvspec(k_blk, ident), kvspec(k_blk, right),
                kvspec(v_blk, left), kvspec(v_blk, ident), kvspec(v_blk, right)]
    args = [proj_l, proj_c, proj_c, proj_l, proj_l, proj_l, proj_l, proj_l, proj_l]
    for f in (ident, left, ident, right):
        in_specs += [tab(f)] * 3
        args += list(tabs)
    gw = SWA_GROUP * SWA_BLOCK
    in_specs.append(pl.BlockSpec((SWA_KV_HEADS, 1, gw), lambda bi, i: (0, 0, 0)))
    args.append(jnp.repeat(sink.reshape(SWA_KV_HEADS, SWA_GROUP).astype(F32), SWA_BLOCK, axis=1)
                .reshape(SWA_KV_HEADS, 1, gw))
    return pl.pallas_call(
        functools.partial(_swa_kernel, seq_len=l),
        grid=(b, nb),
        in_specs=in_specs,
        out_specs=pl.BlockSpec((1, SWA_BLOCK, qw), lambda bi, i: (bi, i, 0)),
        out_shape=jax.ShapeDtypeStruct((b, l, qw), BF16),
        compiler_params=_cp("parallel", "parallel"),
        name="swa",
    )(*args)


def _rms_rows(x, g):
    return x * lax.rsqrt(jnp.mean(x * x, axis=-1, keepdims=True) + NORM_EPS) * g


NT_DIMS = (((1,), (1,)), ((), ()))


def _mla_q_kernel(x_ref, g_ref, wt_ref, cos_ref, sp_ref, sm_ref, o_ref):
    x = _rms_rows(x_ref[0].astype(F32), g_ref[...]).astype(BF16)
    mqt = lax.dot_general(wt_ref[...], x, NT_DIMS, preferred_element_type=F32)
    cos, sp, sm = cos_ref[...], sp_ref[...], sm_ref[...]
    scale = MLA_QK_DIM ** -0.5 * math.log2(math.e)
    q = MLA_ROPE // 4
    for h in range(MLA_HEADS):
        o_ref[0, h * MLA_PAD:h * MLA_PAD + LANES, :] = (mqt[h * MLA_PAD:h * MLA_PAD + LANES] * scale).astype(o_ref.dtype)
        r = mqt[h * MLA_PAD + LANES:(h + 1) * MLA_PAD]
        r = r * cos + pltpu.roll(r, q, 0) * sp + pltpu.roll(r, LANES - q, 0) * sm
        o_ref[0, h * MLA_PAD + LANES:(h + 1) * MLA_PAD, :] = (r * scale).astype(o_ref.dtype)


def _mla_q(proj, g, wt, tabs_t, *, x_blk, tm):
    b, t, _ = proj.shape
    n, kq = wt.shape
    tab = pl.BlockSpec((LANES, tm), lambda bi, i: (0, i))
    return pl.pallas_call(
        _mla_q_kernel,
        grid=(b, t // tm),
        in_specs=[pl.BlockSpec((1, tm, kq), lambda bi, i: (bi, i, x_blk)),
                  pl.BlockSpec((1, kq), lambda bi, i: (0, 0)),
                  pl.BlockSpec((n, kq), lambda bi, i: (0, 0)), tab, tab, tab],
        out_specs=pl.BlockSpec((1, n, tm), lambda bi, i: (bi, 0, i)),
        out_shape=jax.ShapeDtypeStruct((b, n, t), BF16),
        compiler_params=_cp("parallel", "parallel"),
        name="mla_q",
    )(proj, g.reshape(1, kq), wt, *tabs_t)


def _mla_kv_kernel(x_ref, kr_ref, g_ref, wk_ref, wvt_ref, cos_ref, sp_ref, sm_ref, k_ref, vt_ref):
    x = _rms_rows(x_ref[0].astype(F32), g_ref[...]).astype(BF16)
    kn = jnp.dot(x, wk_ref[...], preferred_element_type=F32)
    kr = _rope(kr_ref[0].astype(F32), cos_ref[...], sp_ref[...], sm_ref[...], MLA_ROPE // 4)
    for h in range(MLA_HEADS):
        k_ref[0, h] = jnp.concatenate([kn[:, h * MLA_NOPE:(h + 1) * MLA_NOPE], kr], axis=1).astype(k_ref.dtype)
    vt_ref[0] = lax.dot_general(wvt_ref[...], x, NT_DIMS, preferred_element_type=F32).astype(vt_ref.dtype)


def _mla_kv(proj, g, wk, wvt, tabs, *, x_blk, kr_blk, tm):
    b, t, _ = proj.shape
    kk = wk.shape[0]
    tab = pl.BlockSpec((tm, LANES), lambda bi, i: (i, 0))
    return pl.pallas_call(
        _mla_kv_kernel,
        grid=(b, t // tm),
        in_specs=[pl.BlockSpec((1, tm, kk), lambda bi, i: (bi, i, x_blk)),
                  pl.BlockSpec((1, tm, LANES), lambda bi, i: (bi, i, kr_blk)),
                  pl.BlockSpec((1, kk), lambda bi, i: (0, 0)),
                  pl.BlockSpec(wk.shape, lambda bi, i: (0, 0)),
                  pl.BlockSpec(wvt.shape, lambda bi, i: (0, 0)), tab, tab, tab],
        out_specs=[pl.BlockSpec((1, MLA_HEADS, tm, MLA_PAD), lambda bi, i: (bi, 0, i, 0)),
                   pl.BlockSpec((1, MLA_HEADS * MLA_V, tm), lambda bi, i: (bi, 0, i))],
        out_shape=[jax.ShapeDtypeStruct((b, MLA_HEADS, t, MLA_PAD), BF16),
                   jax.ShapeDtypeStruct((b, MLA_HEADS * MLA_V, t), BF16)],
        compiler_params=_cp("parallel", "parallel"),
        name="mla_kv",
    )(proj, proj, g.reshape(1, kk), wk, wvt, *tabs)


def _flash_kernel(qt_ref, k_ref, vt_ref, o_ref, s_scr, p_scr):
    qt = qt_ref[0]
    tq = qt.shape[1]
    nc, _, tk = vt_ref.shape[2:]
    m = jnp.full((1, tq), MASK_VALUE, F32)
    l = jnp.zeros((1, tq), F32)
    acc = jnp.zeros((MLA_V, tq), F32)
    s_scr[0] = jnp.dot(k_ref[0, 0, 0:tk, :], qt, preferred_element_type=F32)
    for c in range(nc + 1):
        if c + 1 < nc:
            s_scr[(c + 1) % 2] = jnp.dot(k_ref[0, 0, (c + 1) * tk:(c + 2) * tk, :], qt, preferred_element_type=F32)
        if c >= 1:
            acc = acc + jnp.dot(vt_ref[0, 0, c - 1], p_scr[(c - 1) % 2], preferred_element_type=F32)
        if c < nc:
            st = s_scr[c % 2]
            m_new = jnp.maximum(m, jnp.max(st, axis=0, keepdims=True))
            alpha = jnp.exp2(m - m_new)
            p = jnp.exp2(st - m_new)
            l = alpha * l + jnp.sum(p, axis=0, keepdims=True)
            p_scr[c % 2] = p.astype(BF16)
            acc = alpha * acc
            m = m_new
    o_ref[0] = (acc * (1.0 / l)).T.astype(o_ref.dtype)


def _flash(qt, k, vt, *, tq):
    b, _, lq = qt.shape
    lk = k.shape[2]
    nc, _, tk = vt.shape[2:]
    return pl.pallas_call(
        _flash_kernel,
        grid=(b, MLA_HEADS, lq // tq),
        in_specs=[pl.BlockSpec((1, MLA_PAD, tq), lambda bi, h, i: (bi, h, i)),
                  pl.BlockSpec((1, 1, lk, MLA_PAD), lambda bi, h, i: (bi, h, 0, 0)),
                  pl.BlockSpec((1, 1, nc, MLA_V, tk), lambda bi, h, i: (bi, h, 0, 0, 0))],
        out_specs=pl.BlockSpec((1, tq, MLA_V), lambda bi, h, i: (bi, i, h)),
        out_shape=jax.ShapeDtypeStruct((b, lq, MLA_HEADS * MLA_V), BF16),
        scratch_shapes=[pltpu.VMEM((2, tk, tq), F32), pltpu.VMEM((2, tk, tq), BF16)],
        compiler_params=_cp("parallel", "parallel", "arbitrary"),
        name="mla_flash",
    )(qt, k, vt)


def _sort_network(n):
    pairs, p = [], 1
    while p < n:
        k = p
        while k >= 1:
            for j in range(k % p, n - k, 2 * k):
                for i in range(min(k, n - j - k)):
                    if (i + j) // (2 * p) == (i + j + k) // (2 * p):
                        pairs.append((i + j, i + j + k))
            k //= 2
        p *= 2
    return pairs


SUBLANES = 8


def _stack_rows(rows):
    n = rows[0].shape[1]
    sub = lax.broadcasted_iota(jnp.int32, (SUBLANES, n), 0)
    out = jnp.broadcast_to(rows[0], (SUBLANES, n))
    for r in range(1, len(rows)):
        out = jnp.where(sub == r, rows[r], out)
    return out


def _column_top(st, k):
    groups = st.shape[0] // SUBLANES
    v = [st[SUBLANES * j:SUBLANES * (j + 1), :] for j in range(groups)]
    for i, j in _sort_network(groups):
        v[i], v[j] = jnp.maximum(v[i], v[j]), jnp.minimum(v[i], v[j])
    vals = []
    for t in range(k):
        mx = jnp.max(v[0], axis=0, keepdims=True)
        vals.append(mx)
        pop = v[0] >= mx
        for j in range(k - 1 - t):
            v[j] = jnp.where(pop, v[j + 1] if j + 1 < groups else MASK_VALUE, v[j])
    return vals


def _pair_top(a, b, k):
    lo = _stack_rows(a[:SUBLANES])
    v = [lo + b[j] for j in range(k)]
    hi = [_stack_rows(a[SUBLANES * g:SUBLANES * (g + 1)]) + b[0] for g in range(1, k // SUBLANES)]
    vals = []
    for t in range(k + 1):
        head = v[0]
        for hgrp in hi:
            head = jnp.maximum(head, hgrp)
        mx = jnp.max(head, axis=0, keepdims=True)
        vals.append(mx)
        pop = v[0] >= mx
        for j in range(min(k, k - t)):
            v[j] = jnp.where(pop, v[j + 1] if j + 1 < k else MASK_VALUE, v[j])
        hi = [jnp.where(hgrp >= mx, MASK_VALUE, hgrp) for hgrp in hi]
    return vals


def _peer_route_kernel(x_ref, g_ref, mod_ref, wq_ref, ks_ref, h_ref, r1_ref, e1_ref, cut_ref, e0_ref):
    h2 = _hmod(x_ref[0], g_ref[...], mod_ref[0], 3).astype(BF16)
    h_ref[...] = h2
    qp = jnp.dot(h2, wq_ref[...], preferred_element_type=F32).astype(BF16)
    k = PEER_TOPK
    for h in range(PEER_HEADS):
        s0 = lax.dot_general(ks_ref[2 * h], qp[:, (2 * h) * PEER_HALF:(2 * h + 1) * PEER_HALF], NT_DIMS,
                             preferred_element_type=F32)
        s1 = lax.dot_general(ks_ref[2 * h + 1], qp[:, (2 * h + 1) * PEER_HALF:(2 * h + 2) * PEER_HALF], NT_DIMS,
                             preferred_element_type=F32)
        a = _column_top(s0, k)
        bb = _column_top(s1, k)
        c = _pair_top(a, bb, k)
        z = jnp.zeros_like(c[0])
        for j in range(k):
            z = z + jnp.exp(c[j] - c[0])
        tau = 0.5 * (c[k - 1] + c[k])
        theta = tau - s0
        rank1 = jnp.full(s1.shape, float(k), F32)
        cut = jnp.zeros(s0.shape, F32)
        for j in range(k):
            rank1 = jnp.where(s1 >= bb[k - 1 - j], float(k - 1 - j), rank1)
            cut = jnp.where(bb[j] >= theta, float(j + 1), cut)
        r1_ref[h] = rank1.astype(r1_ref.dtype)
        e1_ref[h] = jnp.exp(s1 - bb[0]).astype(e1_ref.dtype)
        cut_ref[h] = jnp.where(s0 >= a[k - 1], cut, 0.0)
        e0_ref[h] = jnp.exp(s0 - a[0]) * (1.0 / z)


def _peer_route(x, g, mod, wq, kt, *, tm):
    b, t, d = x.shape
    nt = t // tm
    rt = lambda: pl.BlockSpec((PEER_HEADS, PEER_KEYS, tm), lambda bi, i: (0, 0, bi * nt + i))
    rshape = lambda dt: jax.ShapeDtypeStruct((PEER_HEADS, PEER_KEYS, b * t), dt)
    return pl.pallas_call(
        _peer_route_kernel,
        grid=(b, nt),
        in_specs=[pl.BlockSpec((1, tm, d), lambda bi, i: (bi, i, 0)),
                  pl.BlockSpec((1, d), lambda bi, i: (0, 0)),
                  pl.BlockSpec((1, 6, d), lambda bi, i: (bi, 0, 0)),
                  pl.BlockSpec(wq.shape, lambda bi, i: (0, 0)),
                  pl.BlockSpec(kt.shape, lambda bi, i: (0, 0, 0))],
        out_specs=[pl.BlockSpec((tm, d), lambda bi, i: (bi * nt + i, 0)), rt(), rt(), rt(), rt()],
        out_shape=[jax.ShapeDtypeStruct((b * t, d), BF16), rshape(BF16), rshape(BF16), rshape(F32), rshape(F32)],
        compiler_params=_cp("parallel", "parallel"),
        name="peer_route",
    )(x, g.reshape(1, d), mod, wq, kt)


def _gelu_tanh(a):
    return 0.5 * a * (1.0 + jnp.tanh(math.sqrt(2.0 / math.pi) * (a + 0.044715 * (a * a * a))))


def _peer_expert_kernel(*refs, final, ne):
    if final:
        (h_ref, hn_ref, u0_ref, un_ref, vt_ref, r1_ref, e1_ref, cut_ref, e0_ref, x_ref, mod_ref, fn_ref, o_ref,
         acc_scr, at_scr) = refs
    else:
        (h_ref, hn_ref, u0_ref, un_ref, vt_ref, r1_ref, e1_ref, cut_ref, e0_ref, x_ref, mod_ref, o_ref,
         acc_scr, at_scr) = refs
    j = pl.program_id(2)
    first_tile = (pl.program_id(0) == 0) & (pl.program_id(1) == 0)

    def pre_activation(u_ref, tok_ref):
        return lax.dot_general(u_ref[...], tok_ref[...], NT_DIMS, preferred_element_type=F32).astype(BF16)

    @pl.when(j == 0)
    def _():
        acc_scr[...] = jnp.zeros_like(acc_scr)

    @pl.when((j == 0) & first_tile)
    def _():
        at_scr[0] = pre_activation(u0_ref, h_ref)

    def body(cur, nxt, tok_ref):
        at_scr[nxt] = pre_activation(un_ref, tok_ref)
        ws = []
        for il in range(cut_ref.shape[1]):
            gate = None
            for h in range(PEER_HEADS):
                cut = cut_ref[h, il:il + 1, :].astype(BF16)
                e0 = e0_ref[h, il:il + 1, :].astype(BF16)
                term = jnp.where(r1_ref[h] < cut, e0 * e1_ref[h], jnp.zeros((), BF16))
                gate = term if gate is None else gate + term
            ws.append(gate * _gelu_tanh(at_scr[cur, il * PEER_KEYS:(il + 1) * PEER_KEYS, :]))
        acc_scr[...] += jnp.dot(vt_ref[0], jnp.concatenate(ws, axis=0), preferred_element_type=F32)

    assert ne % 2 == 0

    @pl.when(j % 2 == 0)
    def _():
        body(0, 1, h_ref)

    @pl.when((j % 2 == 1) & (j != ne - 1))
    def _():
        body(1, 0, h_ref)

    @pl.when(j == ne - 1)
    def _():
        body(1, 0, hn_ref)

    @pl.when(j == pl.num_programs(2) - 1)
    def _():
        y = x_ref[0] + mod_ref[0, 5:6, :] * acc_scr[...].T
        if final:
            y = _rms_rows(y, fn_ref[...])
        o_ref[0] = y


def _peer_expert(h2, route, u, vt, x, mod, final_gain, *, tt, ec):
    b, t, d = x.shape
    nt = t // tt
    ne = u.shape[0] // ec
    r1, e1, cut, e0 = route
    once = dict(pipeline_mode=pl.Buffered(1))
    rt = lambda: pl.BlockSpec((PEER_HEADS, PEER_KEYS, tt), lambda bi, i, j: (0, 0, bi * nt + i))
    rj = lambda: pl.BlockSpec((PEER_HEADS, ec // PEER_KEYS, tt), lambda bi, i, j: (0, j, bi * nt + i))
    in_specs = [pl.BlockSpec((tt, d), lambda bi, i, j: (bi * nt + i, 0)),
                pl.BlockSpec((tt, d), lambda bi, i, j: (jnp.minimum(bi * nt + i + 1, b * nt - 1), 0)),
                pl.BlockSpec((ec, d), lambda bi, i, j: (0, 0), **once),
                pl.BlockSpec((ec, d), lambda bi, i, j: ((j + 1) % ne, 0)),
                pl.BlockSpec((1, d, ec), lambda bi, i, j: (j, 0, 0)),
                rt(), rt(), rj(), rj(),
                pl.BlockSpec((1, tt, d), lambda bi, i, j: (bi, i, 0), **once),
                pl.BlockSpec((1, 6, d), lambda bi, i, j: (bi, 0, 0))]
    args = [h2, h2, u, u, vt, r1, e1, cut, e0, x, mod]
    if final_gain is not None:
        in_specs.append(pl.BlockSpec((1, d), lambda bi, i, j: (0, 0)))
        args.append(final_gain.reshape(1, d))
    return pl.pallas_call(
        functools.partial(_peer_expert_kernel, final=final_gain is not None, ne=ne),
        grid=(b, nt, ne),
        in_specs=in_specs,
        out_specs=pl.BlockSpec((1, tt, d), lambda bi, i, j: (bi, i, 0)),
        out_shape=jax.ShapeDtypeStruct((b, t, d), F32),
        scratch_shapes=[pltpu.VMEM((d, tt), F32), pltpu.VMEM((2, ec, tt), BF16)],
        compiler_params=_cp("arbitrary", "arbitrary", "arbitrary", vmem=EXPERT_VMEM_LIMIT),
        name="peer_expert",
    )(*args)


def _peer(x, g, mod, wq, kt, u, vt, final_gain, *, tm, tt, ec):
    out = _peer_route(x, g, mod, wq, kt, tm=tm)
    return _peer_expert(out[0], out[1:], u, vt, x, mod, final_gain, tt=tt, ec=ec)


def _lat_tile(t, want):
    while t % want:
        want //= 2
    return want


def kernel(x, c, ctx, c_ctx, w_mod, b_mod, norm1, norm2, even_w_in, even_ret_gn, even_w_out, odd_w_in, odd_sink, odd_mla_q_norm, odd_mla_kv_norm, odd_mla_w_uq, odd_mla_w_ukv, odd_w_out, peer_w_q, peer_sub_keys, peer_u, peer_v, final_norm):
    b, s_len, d = x.shape
    n_ctx = ctx.shape[1]
    depth = w_mod.shape[0]
    assert depth == 2 and even_w_in.shape[0] == 1 and odd_w_in.shape[0] == 1
    tm_l = _lat_tile(s_len, 1024)
    tm_c = n_ctx

    rows = 8 * ((b + 1 + 7) // 8)
    cond = jnp.zeros((rows, d), F32).at[:b].set(c).at[b].set(c_ctx)
    mods = _adaln(cond, w_mod, b_mod).reshape(depth, rows, 6, d)
    mod_l = [mods[l, :b] for l in range(depth)]
    mod_c = [jnp.broadcast_to(mods[l, b:b + 1], (b, 6, d)) for l in range(depth)]

    x_lat, x_ctx = x, ctx

    def peer_params(layer):
        kt = peer_sub_keys[layer].reshape(2 * PEER_HEADS, PEER_KEYS, PEER_HALF).astype(BF16)
        n_exp = peer_v.shape[1]
        vt = jnp.transpose(peer_v[layer].astype(BF16).reshape(n_exp // EXPERT_CHUNK, EXPERT_CHUNK, d), (0, 2, 1))
        return (peer_w_q[layer].astype(BF16), kt, peer_u[layer].astype(BF16), vt)

    w_in = even_w_in[0].astype(BF16)
    w_out = even_w_out[0].astype(BF16)
    p_l = _proj(x_lat, norm1[0], mod_l[0], w_in, si=0, tm=tm_l, tn=1024, name="even_proj_lat")
    p_c = _proj(x_ctx, norm1[0], mod_c[0], w_in, si=0, tm=tm_c, tn=512, name="even_proj_ctx")
    fm_l = _fourier_lat(p_l[..., :FOURIER_WIDTH])
    fm_c = _fourier_small(p_c[..., :FOURIER_WIDTH])
    tabs_l = _rope_tables(s_len, RET_DIM, LANES)
    tabs_c = _rope_tables(n_ctx, RET_DIM, LANES, identity=True)
    zero = jnp.zeros((b, RET_HEADS, RET_DIM, RET_DIM), F32)
    gn = even_ret_gn[0]
    rf_c, st_f = _retention(p_c, tabs_c, gn, zero, None, reverse=False, gate_blk=4)
    r_c, st_b = _retention(p_c, tabs_c, gn, zero, rf_c, reverse=True, gate_blk=5)
    rf_l, _ = _retention(p_l, tabs_l, gn, st_f, None, reverse=False, gate_blk=4)
    r_l, _ = _retention(p_l, tabs_l, gn, st_b, rf_l, reverse=True, gate_blk=5)
    x_lat = _outproj(fm_l, r_l, w_out, x_lat, mod_l[0], tm=tm_l, tn=1024, name="even_out_lat")
    x_ctx = _outproj(fm_c, r_c, w_out, x_ctx, mod_c[0], tm=tm_c, tn=512, name="even_out_ctx")
    pp = peer_params(0)
    tt_l = _lat_tile(s_len, 512)
    x_lat = _peer(x_lat, norm2[0], mod_l[0], *pp, None, tm=256, tt=tt_l, ec=EXPERT_CHUNK)
    x_ctx = _peer(x_ctx, norm2[0], mod_c[0], *pp, None, tm=256, tt=n_ctx, ec=EXPERT_CHUNK)

    wi = odd_w_in[0]
    o_sq, o_sk, o_sv = 0, SWA_Q_HEADS * SWA_HEAD_DIM, (SWA_Q_HEADS + SWA_KV_HEADS) * SWA_HEAD_DIM
    o_cq = o_sv + SWA_KV_HEADS * SWA_HEAD_DIM
    o_ckv = o_cq + MLA_Q_RANK
    o_kr = o_ckv + MLA_KV_RANK
    w_in = jnp.concatenate([wi[:, o_sq:o_sk], wi[:, o_cq:o_ckv], wi[:, o_ckv:o_kr], wi[:, o_sk:o_sv], wi[:, o_sv:o_cq],
                            wi[:, o_kr:], jnp.zeros((d, LANES - MLA_ROPE), wi.dtype)], axis=1).astype(BF16)
    n_odd = w_in.shape[1]
    q_blk, cq_blk, ckv_blk = 0, (o_sk - o_sq) // MLA_Q_RANK, (o_sk - o_sq + MLA_Q_RANK) // MLA_KV_RANK
    k_blk = (o_sk - o_sq + MLA_Q_RANK + MLA_KV_RANK) // LANES
    v_blk, kr_blk = k_blk + 1, k_blk + 2
    tm_o = _lat_tile(s_len, 512)
    p_l = _proj(x_lat, norm1[1], mod_l[1], w_in, si=0, tm=tm_o, tn=n_odd, name="odd_proj_lat")
    p_c = _proj(x_ctx, norm1[1], mod_c[1], w_in, si=0, tm=tm_c, tn=n_odd, name="odd_proj_ctx")
    tabs_swa = _rope_tables(s_len, SWA_HEAD_DIM, LANES)
    swa = _swa(p_l, p_c, tabs_swa, odd_sink[0], q_blk=q_blk, k_blk=k_blk, v_blk=v_blk)
    wuq = odd_mla_w_uq[0].reshape(MLA_Q_RANK, MLA_HEADS, MLA_QK_DIM)
    wuq = jnp.pad(wuq, ((0, 0), (0, 0), (0, MLA_PAD - MLA_QK_DIM))).reshape(MLA_Q_RANK, MLA_HEADS * MLA_PAD)
    wukv = odd_mla_w_ukv[0].reshape(MLA_KV_RANK, MLA_HEADS, MLA_NOPE + MLA_V)
    wk = wukv[:, :, :MLA_NOPE].reshape(MLA_KV_RANK, -1).astype(BF16)
    wvt = wukv[:, :, MLA_NOPE:].reshape(MLA_KV_RANK, -1).T.astype(BF16)
    tabs_mla = _pad_rope_tables(_rope_tables(s_len, MLA_ROPE, MLA_ROPE), LANES)
    tabs_id = _rope_tables(n_ctx, MLA_ROPE, LANES, identity=True)
    qt = _mla_q(p_l, odd_mla_q_norm[0], wuq.T.astype(BF16), tuple(a.T for a in tabs_mla), x_blk=cq_blk, tm=tm_o)
    k_l, vt_l = _mla_kv(p_l, odd_mla_kv_norm[0], wk, wvt, tabs_mla, x_blk=ckv_blk, kr_blk=kr_blk, tm=tm_o)
    k_c, vt_c = _mla_kv(p_c, odd_mla_kv_norm[0], wk, wvt, tabs_id, x_blk=ckv_blk, kr_blk=kr_blk, tm=tm_c)
    k_all = jnp.concatenate([k_c, k_l], axis=2)
    lk = n_ctx + s_len
    tk = max(t for t in (1280, 1024, 768, 512, 256, 128) if lk % t == 0)
    vt_all = jnp.concatenate([vt_c, vt_l], axis=2).reshape(b, MLA_HEADS, MLA_V, lk // tk, tk)
    vt_all = jnp.transpose(vt_all, (0, 1, 3, 2, 4))
    mla = _flash(qt, k_all, vt_all, tq=_lat_tile(s_len, 512))
    x_lat = _outproj(swa, mla, odd_w_out[0].astype(BF16), x_lat, mod_l[1], tm=tm_l, tn=1024, name="odd_out_lat")
    pp = peer_params(1)
    return _peer(x_lat, norm2[1], mod_l[1], *pp, final_norm, tm=256, tt=tt_l, ec=EXPERT_CHUNK)
```

```python
import functools
import math

import numpy as np
import jax
import jax.numpy as jnp
from jax import lax
from jax.experimental import pallas as pl
from jax.experimental.pallas import tpu as pltpu

F32 = jnp.float32
BF16 = jnp.bfloat16

GRID_W = 64
NORM_EPS = 1e-6
ROPE_BASE = 10000.0
MASK_VALUE = -1e30
LANES = 128

FOURIER_GROUPS = 4
FOURIER_GROUP_DIM = 256
FOURIER_WIDTH = FOURIER_GROUPS * FOURIER_GROUP_DIM
RET_HEADS = 8
RET_DIM = 128
RET_CHUNK = 128
RET_WIDTH = RET_HEADS * RET_DIM

SWA_Q_HEADS = 16
SWA_KV_HEADS = 2
SWA_GROUP = SWA_Q_HEADS // SWA_KV_HEADS
SWA_HEAD_DIM = 64
SWA_WINDOW = 128
SWA_BLOCK = 128
MLA_HEADS = 8
MLA_Q_RANK = 512
MLA_KV_RANK = 256
MLA_NOPE = 128
MLA_ROPE = 64
MLA_V = 128
MLA_QK_DIM = MLA_NOPE + MLA_ROPE
MLA_PAD = 2 * LANES

PEER_HEADS = 8
PEER_KEYS = 128
PEER_TOPK = 16
PEER_HALF = 128
EXPERT_CHUNK = 1024

VMEM_LIMIT = 56 * 1024 * 1024
EXPERT_VMEM_LIMIT = 60 * 1024 * 1024


def _cp(*sem, vmem=VMEM_LIMIT):
    return pltpu.CompilerParams(dimension_semantics=sem, vmem_limit_bytes=vmem)


def _adaln_kernel(c_ref, w_ref, b_ref, o_ref):
    c = c_ref[...]
    s = c * jax.nn.sigmoid(c)
    o_ref[0] = jnp.dot(s, w_ref[0], preferred_element_type=F32) + b_ref[0]


def _adaln(cond, w_mod, b_mod):
    depth, d, n = w_mod.shape
    r = cond.shape[0]
    tn = 1024
    return pl.pallas_call(
        _adaln_kernel,
        grid=(depth, n // tn),
        in_specs=[pl.BlockSpec((r, d), lambda l, j: (0, 0)),
                  pl.BlockSpec((1, d, tn), lambda l, j: (l, 0, j)),
                  pl.BlockSpec((1, 1, tn), lambda l, j: (l, 0, j))],
        out_specs=pl.BlockSpec((1, r, tn), lambda l, j: (l, 0, j)),
        out_shape=jax.ShapeDtypeStruct((depth, r, n), F32),
        compiler_params=_cp("parallel", "parallel"),
        name="adaln",
    )(cond, w_mod, b_mod.reshape(depth, 1, n))


def _hmod(x, g, mod, si):
    ms = jnp.mean(x * x, axis=-1, keepdims=True)
    y = x * lax.rsqrt(ms + NORM_EPS) * g
    return y * (1.0 + mod[si + 1:si + 2, :]) + mod[si:si + 1, :]


def _proj_kernel(x_ref, g_ref, mod_ref, w_ref, o_ref, h_scr, *, si):
    @pl.when(pl.program_id(2) == 0)
    def _():
        h_scr[...] = _hmod(x_ref[0], g_ref[...], mod_ref[0], si).astype(BF16)

    o_ref[0] = jnp.dot(h_scr[...], w_ref[...], preferred_element_type=F32).astype(o_ref.dtype)


def _proj(x, g, mod, w, *, si, tm, tn, name):
    b, t, d = x.shape
    n = w.shape[1]
    return pl.pallas_call(
        functools.partial(_proj_kernel, si=si),
        grid=(b, t // tm, n // tn),
        in_specs=[pl.BlockSpec((1, tm, d), lambda bi, i, j: (bi, i, 0)),
                  pl.BlockSpec((1, d), lambda bi, i, j: (0, 0)),
                  pl.BlockSpec((1, 6, d), lambda bi, i, j: (bi, 0, 0)),
                  pl.BlockSpec((d, tn), lambda bi, i, j: (0, j))],
        out_specs=pl.BlockSpec((1, tm, tn), lambda bi, i, j: (bi, i, j)),
        out_shape=jax.ShapeDtypeStruct((b, t, n), BF16),
        scratch_shapes=[pltpu.VMEM((tm, d), BF16)],
        compiler_params=_cp("parallel", "parallel", "arbitrary"),
        name=name,
    )(x, g.reshape(1, d), mod, w)


def _outproj_kernel(a_ref, b_ref, w_ref, x_ref, mod_ref, o_ref):
    ka = a_ref.shape[2]
    acc = jnp.dot(a_ref[0], w_ref[:ka, :], preferred_element_type=F32)
    acc = acc + jnp.dot(b_ref[0], w_ref[ka:, :], preferred_element_type=F32)
    o_ref[0] = x_ref[0] + mod_ref[0, 2:3, :] * acc


def _outproj(a, b2, w, x, mod, *, tm, tn, name):
    b, t, d = x.shape
    ka, kb = a.shape[2], b2.shape[2]
    return pl.pallas_call(
        _outproj_kernel,
        grid=(b, t // tm, d // tn),
        in_specs=[pl.BlockSpec((1, tm, ka), lambda bi, i, j: (bi, i, 0)),
                  pl.BlockSpec((1, tm, kb), lambda bi, i, j: (bi, i, 0)),
                  pl.BlockSpec((ka + kb, tn), lambda bi, i, j: (0, j)),
                  pl.BlockSpec((1, tm, tn), lambda bi, i, j: (bi, i, j)),
                  pl.BlockSpec((1, 6, tn), lambda bi, i, j: (bi, 0, j))],
        out_specs=pl.BlockSpec((1, tm, tn), lambda bi, i, j: (bi, i, j)),
        out_shape=jax.ShapeDtypeStruct((b, t, d), F32),
        compiler_params=_cp("parallel", "parallel", "arbitrary"),
        name=name,
    )(a, b2, w, x, mod)


def _rope_tables(n_pos, d_rot, width, identity=False):
    q = d_rot // 4
    if identity:
        return (jnp.ones((n_pos, width), F32), jnp.zeros((n_pos, width), F32), jnp.zeros((n_pos, width), F32))
    d_axis = d_rot // 2
    inv = ROPE_BASE ** (-jnp.arange(0, d_axis, 2, dtype=F32) / d_axis)
    t = jnp.arange(n_pos)
    row = (t // GRID_W).astype(F32)
    col = (t % GRID_W).astype(F32)
    ar = row[:, None] * inv[None, :]
    ac = col[:, None] * inv[None, :]
    ang = jnp.concatenate([ar, ar, ac, ac], axis=-1)
    cos, sin = jnp.cos(ang), jnp.sin(ang)
    quarter = np.arange(d_rot) // q
    up = jnp.asarray((quarter % 2 == 1).astype(np.float32))
    dn = jnp.asarray((quarter % 2 == 0).astype(np.float32))
    sin_p, sin_m = sin * up, -sin * dn
    reps = width // d_rot
    return tuple(jnp.tile(a, (1, reps)) for a in (cos, sin_p, sin_m))


def _pad_rope_tables(tabs, width):
    cos, sp, sm = tabs
    pad = width - cos.shape[1]
    return (jnp.pad(cos, ((0, 0), (0, pad)), constant_values=1.0), jnp.pad(sp, ((0, 0), (0, pad))),
            jnp.pad(sm, ((0, 0), (0, pad))))


def _rope(t, cos, sp, sm, q):
    w = t.shape[-1]
    return t * cos + pltpu.roll(t, q, 1) * sp + pltpu.roll(t, w - q, 1) * sm


def _dft_mats(n):
    k = np.arange(n)
    ang = 2.0 * np.pi * ((k[:, None] * k[None, :]) % n) / n
    return np.cos(ang), np.sin(ang)


def _fourier_a_kernel(u_ref, fa_ref, tc_ref, ts_ref, o_ref):
    l1 = u_ref.shape[1]
    t1 = jnp.dot(fa_ref[...], u_ref[0], preferred_element_type=F32)
    tr, ti = t1[:l1], t1[l1:]
    reps = tr.shape[1] // LANES
    c = jnp.concatenate([tc_ref[0]] * reps, axis=1)
    s = jnp.concatenate([ts_ref[0]] * reps, axis=1)
    o_ref[0, 0] = (tr * c + ti * s).astype(o_ref.dtype)
    o_ref[0, 1] = (ti * c - tr * s).astype(o_ref.dtype)


def _fourier_b_kernel(t_ref, fb_ref, cs_ref, o_ref):
    l2 = t_ref.shape[2]
    t2 = jnp.concatenate([t_ref[0, 0], t_ref[0, 1]], axis=0)
    v = jnp.dot(fb_ref[...], t2, preferred_element_type=F32).astype(BF16)
    vr, vi = v[:l2], v[l2:]
    outs = []
    for g in range(FOURIER_GROUPS):
        sl = slice(g * FOURIER_GROUP_DIM, (g + 1) * FOURIER_GROUP_DIM)
        vg = jnp.concatenate([vr[:, sl], vi[:, sl]], axis=1)
        outs.append(jnp.dot(vg, cs_ref[...], preferred_element_type=F32))
    o_ref[0] = jnp.concatenate(outs, axis=1).astype(o_ref.dtype)


def _fourier_lat(u):
    b, l, w = u.shape
    l1 = int(round(math.sqrt(l)))
    l2 = l // l1
    assert l1 * l2 == l and l1 % 16 == 0 and l2 % 16 == 0
    ca, sa = _dft_mats(l1)
    fa = jnp.asarray(np.concatenate([ca, -sa], axis=0), BF16)
    kk = np.arange(l1)[None, :, None] * np.arange(l2)[:, None, None]
    ang = 2.0 * np.pi * (kk % l) / l
    tc = jnp.asarray(np.broadcast_to(np.cos(ang), (l2, l1, LANES)), F32)
    ts = jnp.asarray(np.broadcast_to(np.sin(ang), (l2, l1, LANES)), F32)
    t2 = pl.pallas_call(
        _fourier_a_kernel,
        grid=(b, l2),
        in_specs=[pl.BlockSpec((1, l1, w), lambda bi, j: (bi, 0, j)),
                  pl.BlockSpec((2 * l1, l1), lambda bi, j: (0, 0)),
                  pl.BlockSpec((1, l1, LANES), lambda bi, j: (j, 0, 0)),
                  pl.BlockSpec((1, l1, LANES), lambda bi, j: (j, 0, 0))],
        out_specs=pl.BlockSpec((1, 2, l1, w), lambda bi, j: (bi, 0, 0, j)),
        out_shape=jax.ShapeDtypeStruct((b, 2, l1, l2 * w), BF16),
        compiler_params=_cp("parallel", "parallel"),
        name="fourier_a",
    )(u.reshape(b, l1, l2 * w), fa, tc, ts)
    cb, sb = _dft_mats(l2)
    scale = 1.0 / math.sqrt(l * FOURIER_GROUP_DIM)
    fb = jnp.asarray(np.block([[cb, sb], [-sb, cb]]) * scale, BF16)
    cc, sc = _dft_mats(FOURIER_GROUP_DIM)
    cs = jnp.asarray(np.concatenate([cc, sc], axis=0), BF16)
    y = pl.pallas_call(
        _fourier_b_kernel,
        grid=(b, l1),
        in_specs=[pl.BlockSpec((1, 2, l2, w), lambda bi, j: (bi, 0, j, 0)),
                  pl.BlockSpec((2 * l2, 2 * l2), lambda bi, j: (0, 0)),
                  pl.BlockSpec((2 * FOURIER_GROUP_DIM, FOURIER_GROUP_DIM), lambda bi, j: (0, 0))],
        out_specs=pl.BlockSpec((1, l2, w), lambda bi, j: (bi, 0, j)),
        out_shape=jax.ShapeDtypeStruct((b, l2, l1 * w), BF16),
        compiler_params=_cp("parallel", "parallel"),
        name="fourier_b",
    )(t2.reshape(b, 2, l1 * l2, w), fb, cs)
    return y.reshape(b, l, w)


def _fourier_small_kernel(u_ref, cl_ref, sl_ref, cs_ref, o_ref):
    gd = FOURIER_GROUP_DIM
    outs = []
    for g in range(FOURIER_GROUPS):
        ug = u_ref[0][:, g * gd:(g + 1) * gd]
        pq = jnp.dot(ug, cs_ref[...], preferred_element_type=F32).astype(BF16)
        y = jnp.dot(cl_ref[...], pq[:, :gd], preferred_element_type=F32)
        y = y - jnp.dot(sl_ref[...], pq[:, gd:], preferred_element_type=F32)
        outs.append(y)
    o_ref[0] = jnp.concatenate(outs, axis=1).astype(o_ref.dtype)


def _fourier_small(u):
    b, l, w = u.shape
    cl, sl = _dft_mats(l)
    scale = 1.0 / math.sqrt(l * FOURIER_GROUP_DIM)
    cc, sc = _dft_mats(FOURIER_GROUP_DIM)
    cs = jnp.asarray(np.concatenate([cc, sc], axis=1), BF16)
    return pl.pallas_call(
        _fourier_small_kernel,
        grid=(b,),
        in_specs=[pl.BlockSpec((1, l, w), lambda bi: (bi, 0, 0)),
                  pl.BlockSpec((l, l), lambda bi: (0, 0)),
                  pl.BlockSpec((l, l), lambda bi: (0, 0)),
                  pl.BlockSpec((FOURIER_GROUP_DIM, 2 * FOURIER_GROUP_DIM), lambda bi: (0, 0))],
        out_specs=pl.BlockSpec((1, l, w), lambda bi: (bi, 0, 0)),
        out_shape=jax.ShapeDtypeStruct((b, l, w), BF16),
        compiler_params=_cp("parallel"),
        name="fourier_small",
    )(u, jnp.asarray(cl * scale, BF16), jnp.asarray(sl * scale, BF16), cs)


def _ret_consts(reverse):
    c = RET_CHUNK
    lg = np.log(1.0 - np.exp2(-5.0 - np.arange(RET_HEADS, dtype=np.float32))).astype(np.float32)
    pos = np.arange(c, dtype=np.float32)
    diff = pos[:, None] - pos[None, :]
    if reverse:
        diff = -diff
    decay = np.where(diff >= 0, np.exp(lg[:, None, None] * np.maximum(diff, 0.0)), 0.0)
    if reverse:
        zeta = np.exp(lg[None, :] * pos[:, None])
        xi = np.exp(lg[None, :] * (c - pos[:, None]))
    else:
        zeta = np.exp(lg[None, :] * (c - 1.0 - pos[:, None]))
        xi = np.exp(lg[None, :] * (pos[:, None] + 1.0))
    wide = lambda a: np.repeat(a, RET_DIM, axis=1)
    chunk_decay = [float(v) for v in np.exp(lg * c)]
    return (jnp.asarray(decay, F32), jnp.asarray(wide(zeta), F32), jnp.asarray(wide(xi), F32), chunk_decay)


def _ret_kernel(*refs, chunk_decay, has_prev):
    if has_prev:
        (q_ref, k_ref, v_ref, g_ref, cos_ref, sp_ref, sm_ref, dec_ref, zeta_ref, xi_ref, gn_ref, s0_ref, prev_ref,
         o_ref, sf_ref, st_scr) = refs
    else:
        (q_ref, k_ref, v_ref, g_ref, cos_ref, sp_ref, sm_ref, dec_ref, zeta_ref, xi_ref, gn_ref, s0_ref,
         o_ref, sf_ref, st_scr) = refs
        prev_ref = None
    c = pl.program_id(1)

    @pl.when(c == 0)
    def _():
        st_scr[...] = s0_ref[0]

    cos, sp, sm = cos_ref[...], sp_ref[...], sm_ref[...]
    outs = []
    for h in range(RET_HEADS):
        sl = slice(h * RET_DIM, (h + 1) * RET_DIM)
        q = _rope(q_ref[0][:, sl].astype(F32), cos, sp, sm, RET_DIM // 4)
        k = _rope(k_ref[0][:, sl].astype(F32), cos, sp, sm, RET_DIM // 4) * (RET_DIM ** -0.5)
        v = v_ref[0][:, sl].astype(F32)
        kt = k.T.astype(BF16)
        scores = jnp.dot(q.astype(BF16), kt, preferred_element_type=F32) * dec_ref[h]
        inner = jnp.dot(scores.astype(BF16), v.astype(BF16), preferred_element_type=F32)
        state = st_scr[h]
        cross = jnp.dot((q * xi_ref[:, sl]).astype(BF16), state.astype(BF16), preferred_element_type=F32)
        o = inner + cross
        st_scr[h] = state * chunk_decay[h] + jnp.dot(kt, (v * zeta_ref[:, sl]).astype(BF16),
                                                      preferred_element_type=F32)
        y = o * lax.rsqrt(jnp.mean(o * o, axis=-1, keepdims=True) + NORM_EPS) * gn_ref[:, sl]
        gate = g_ref[0][:, sl].astype(F32)
        outs.append(y * (gate * jax.nn.sigmoid(gate)))
    res = jnp.concatenate(outs, axis=1)
    if prev_ref is not None:
        res = res + prev_ref[0].astype(F32)
    o_ref[0] = res.astype(o_ref.dtype)

    @pl.when(c == pl.num_programs(1) - 1)
    def _():
        sf_ref[0] = st_scr[...]


def _retention(proj, tabs, gn, state0, prev, *, reverse, gate_blk):
    b, t, _ = proj.shape
    w = RET_WIDTH
    n = t // RET_CHUNK
    decay, zeta, xi, chunk_decay = _ret_consts(reverse)
    pos = (lambda ci: n - 1 - ci) if reverse else (lambda ci: ci)
    blk = lambda col: pl.BlockSpec((1, RET_CHUNK, w), lambda bi, ci: (bi, pos(ci), col))
    tab = pl.BlockSpec((RET_CHUNK, LANES), lambda bi, ci: (pos(ci), 0))
    const2 = pl.BlockSpec((RET_CHUNK, w), lambda bi, ci: (0, 0))
    st = pl.BlockSpec((1, RET_HEADS, RET_DIM, RET_DIM), lambda bi, ci: (bi, 0, 0, 0))
    in_specs = [blk(1), blk(2), blk(3), blk(gate_blk), tab, tab, tab,
                pl.BlockSpec((RET_HEADS, RET_CHUNK, RET_CHUNK), lambda bi, ci: (0, 0, 0)), const2, const2,
                pl.BlockSpec((1, w), lambda bi, ci: (0, 0)), st]
    args = [proj, proj, proj, proj, *tabs, decay, zeta, xi, gn.reshape(1, w), state0]
    if prev is not None:
        in_specs.append(pl.BlockSpec((1, RET_CHUNK, w), lambda bi, ci: (bi, pos(ci), 0)))
        args.append(prev)
    return pl.pallas_call(
        functools.partial(_ret_kernel, chunk_decay=chunk_decay, has_prev=prev is not None),
        grid=(b, n),
        in_specs=in_specs,
        out_specs=[pl.BlockSpec((1, RET_CHUNK, w), lambda bi, ci: (bi, pos(ci), 0)), st],
        out_shape=[jax.ShapeDtypeStruct((b, t, w), BF16),
                   jax.ShapeDtypeStruct((b, RET_HEADS, RET_DIM, RET_DIM), F32)],
        scratch_shapes=[pltpu.VMEM((RET_HEADS, RET_DIM, RET_DIM), F32)],
        compiler_params=_cp("parallel", "arbitrary"),
        name="retention_bwd" if reverse else "retention_fwd",
    )(*args)


def _swa_kernel(q_ref, kc_ref, vc_ref, kl_ref, km_ref, kr_ref, vl_ref, vm_ref, vr_ref,
                cq_ref, spq_ref, smq_ref, cl_ref, spl_ref, sml_ref, cm_ref, spm_ref, smm_ref,
                cr_ref, spr_ref, smr_ref, sink_ref, o_ref, s_scr, *, seq_len):
    i = pl.program_id(1)
    qd = SWA_HEAD_DIM // 4
    hd = SWA_HEAD_DIM
    cq, spq, smq = cq_ref[...], spq_ref[...], smq_ref[...]
    rk = lambda kref, c, sp, sm: _rope(kref[0].astype(F32), c[...], sp[...], sm[...], qd)
    k_all = jnp.concatenate([kc_ref[0].astype(F32), rk(kl_ref, cl_ref, spl_ref, sml_ref),
                             rk(km_ref, cm_ref, spm_ref, smm_ref), rk(kr_ref, cr_ref, spr_ref, smr_ref)], axis=0)
    v_all = jnp.concatenate([vc_ref[0], vl_ref[0], vm_ref[0], vr_ref[0]], axis=0).astype(F32)
    vt = v_all.T
    n_ctx = kc_ref.shape[1]
    n_keys = k_all.shape[0]
    n_loc = 3 * SWA_BLOCK
    gw = SWA_GROUP * SWA_BLOCK
    cidx = lax.broadcasted_iota(jnp.int32, (n_loc, gw), 0)
    r = lax.broadcasted_iota(jnp.int32, (n_loc, gw), 1) % SWA_BLOCK
    kpos = (i - 1) * SWA_BLOCK + cidx
    valid = (jnp.abs(SWA_BLOCK + r - cidx) <= SWA_WINDOW) & (kpos >= 0) & (kpos < seq_len)
    klane = lax.broadcasted_iota(jnp.int32, (n_keys, LANES), 1)
    vrow = lax.broadcasted_iota(jnp.int32, (LANES, n_keys), 0)
    qt_blk = [(_rope(q_ref[0][:, blk * LANES:(blk + 1) * LANES].astype(F32), cq, spq, smq, qd) * (hd ** -0.5)).T
              for blk in range(SWA_Q_HEADS // 2)]
    zeros = jnp.zeros((hd, gw), F32)
    for kv in range(SWA_KV_HEADS):
        pieces = []
        for h in range(kv * SWA_GROUP, (kv + 1) * SWA_GROUP):
            pieces.append(qt_blk[h // 2][(h % 2) * hd:(h % 2 + 1) * hd, :])
        qg = jnp.concatenate(pieces, axis=1)
        qg = jnp.concatenate([qg, zeros] if kv == 0 else [zeros, qg], axis=0).astype(BF16)
        in_kv = (klane < hd) if kv == 0 else (klane >= hd)
        s_scr[kv] = jnp.dot(jnp.where(in_kv, k_all, 0.0).astype(BF16), qg, preferred_element_type=F32)
    outs = []
    for kv in range(SWA_KV_HEADS):
        st = s_scr[kv]
        st = jnp.concatenate([st[:n_ctx], jnp.where(valid, st[n_ctx:], MASK_VALUE)], axis=0)
        sk = sink_ref[kv]
        m = jnp.maximum(jnp.max(st, axis=0, keepdims=True), sk)
        e = jnp.exp(st - m)
        p = e * (1.0 / (jnp.sum(e, axis=0, keepdims=True) + jnp.exp(sk - m)))
        v_kv = jnp.where((vrow < hd) if kv == 0 else (vrow >= hd), vt, 0.0).astype(BF16)
        ot = jnp.dot(v_kv, p.astype(BF16), preferred_element_type=F32)[kv * hd:(kv + 1) * hd, :]
        stacked = jnp.concatenate([ot[:, g * SWA_BLOCK:(g + 1) * SWA_BLOCK] for g in range(SWA_GROUP)], axis=0)
        outs.append(stacked.T)
    o_ref[0] = jnp.concatenate(outs, axis=1).astype(o_ref.dtype)


def _swa(proj_l, proj_c, tabs, sink, *, q_blk, k_blk, v_blk):
    b, l, _ = proj_l.shape
    n_ctx = proj_c.shape[1]
    nb = l // SWA_BLOCK
    qw = SWA_Q_HEADS * SWA_HEAD_DIM
    left = lambda i: jnp.maximum(i - 1, 0)
    right = lambda i: jnp.minimum(i + 1, nb - 1)
    kvspec = lambda col, f: pl.BlockSpec((1, SWA_BLOCK, LANES), lambda bi, i: (bi, f(i), col))
    ctxspec = lambda col: pl.BlockSpec((1, n_ctx, LANES), lambda bi, i: (bi, 0, col))
    tab = lambda f: pl.BlockSpec((SWA_BLOCK, LANES), lambda bi, i: (f(i), 0))
    ident = lambda i: i
    in_specs = [pl.BlockSpec((1, SWA_BLOCK, qw), lambda bi, i: (bi, i, q_blk)),
                ctxspec(k_blk), ctxspec(v_blk),
                kvspec(k_blk, left), kvspec(k_blk, ident), kvspec(k_blk, right),
                kvspec(v_blk, left), kvspec(v_blk, ident), kvspec(v_blk, right)]
    args = [proj_l, proj_c, proj_c, proj_l, proj_l, proj_l, proj_l, proj_l, proj_l]
    for f in (ident, left, ident, right):
        in_specs += [tab(f)] * 3
        args += list(tabs)
    gw = SWA_GROUP * SWA_BLOCK
    in_specs.append(pl.BlockSpec((SWA_KV_HEADS, 1, gw), lambda bi, i: (0, 0, 0)))
    args.append(jnp.repeat(sink.reshape(SWA_KV_HEADS, SWA_GROUP).astype(F32), SWA_BLOCK, axis=1)
                .reshape(SWA_KV_HEADS, 1, gw))
    return pl.pallas_call(
        functools.partial(_swa_kernel, seq_len=l),
        grid=(b, nb),
        in_specs=in_specs,
        out_specs=pl.BlockSpec((1, SWA_BLOCK, qw), lambda bi, i: (bi, i, 0)),
        out_shape=jax.ShapeDtypeStruct((b, l, qw), BF16),
        scratch_shapes=[pltpu.VMEM((SWA_KV_HEADS, n_ctx + 3 * SWA_BLOCK, gw), F32)],
        compiler_params=_cp("parallel", "parallel"),
        name="swa",
    )(*args)


def _rms_rows(x, g):
    return x * lax.rsqrt(jnp.mean(x * x, axis=-1, keepdims=True) + NORM_EPS) * g


NT_DIMS = (((1,), (1,)), ((), ()))


def _mla_q_kernel(x_ref, g_ref, wt_ref, cos_ref, sp_ref, sm_ref, o_ref):
    x = _rms_rows(x_ref[0].astype(F32), g_ref[...]).astype(BF16)
    mqt = lax.dot_general(wt_ref[...], x, NT_DIMS, preferred_element_type=F32)
    cos, sp, sm = cos_ref[...], sp_ref[...], sm_ref[...]
    scale = MLA_QK_DIM ** -0.5 * math.log2(math.e)
    q = MLA_ROPE // 4
    for h in range(MLA_HEADS):
        o_ref[0, h * MLA_PAD:h * MLA_PAD + LANES, :] = (mqt[h * MLA_PAD:h * MLA_PAD + LANES] * scale).astype(o_ref.dtype)
        r = mqt[h * MLA_PAD + LANES:(h + 1) * MLA_PAD]
        r = r * cos + pltpu.roll(r, q, 0) * sp + pltpu.roll(r, LANES - q, 0) * sm
        o_ref[0, h * MLA_PAD + LANES:(h + 1) * MLA_PAD, :] = (r * scale).astype(o_ref.dtype)


def _mla_q(proj, g, wt, tabs_t, *, x_blk, tm):
    b, t, _ = proj.shape
    n, kq = wt.shape
    tab = pl.BlockSpec((LANES, tm), lambda bi, i: (0, i))
    return pl.pallas_call(
        _mla_q_kernel,
        grid=(b, t // tm),
        in_specs=[pl.BlockSpec((1, tm, kq), lambda bi, i: (bi, i, x_blk)),
                  pl.BlockSpec((1, kq), lambda bi, i: (0, 0)),
                  pl.BlockSpec((n, kq), lambda bi, i: (0, 0)), tab, tab, tab],
        out_specs=pl.BlockSpec((1, n, tm), lambda bi, i: (bi, 0, i)),
        out_shape=jax.ShapeDtypeStruct((b, n, t), BF16),
        compiler_params=_cp("parallel", "parallel"),
        name="mla_q",
    )(proj, g.reshape(1, kq), wt, *tabs_t)


def _mla_kv_kernel(x_ref, kr_ref, g_ref, wk_ref, wvt_ref, cos_ref, sp_ref, sm_ref, k_ref, vt_ref):
    x = _rms_rows(x_ref[0].astype(F32), g_ref[...]).astype(BF16)
    kn = jnp.dot(x, wk_ref[...], preferred_element_type=F32)
    kr = _rope(kr_ref[0].astype(F32), cos_ref[...], sp_ref[...], sm_ref[...], MLA_ROPE // 4)
    for h in range(MLA_HEADS):
        k_ref[0, h] = jnp.concatenate([kn[:, h * MLA_NOPE:(h + 1) * MLA_NOPE], kr], axis=1).astype(k_ref.dtype)
    vt_ref[0] = lax.dot_general(wvt_ref[...], x, NT_DIMS, preferred_element_type=F32).astype(vt_ref.dtype)


def _mla_kv(proj, g, wk, wvt, tabs, *, x_blk, kr_blk, tm):
    b, t, _ = proj.shape
    kk = wk.shape[0]
    tab = pl.BlockSpec((tm, LANES), lambda bi, i: (i, 0))
    return pl.pallas_call(
        _mla_kv_kernel,
        grid=(b, t // tm),
        in_specs=[pl.BlockSpec((1, tm, kk), lambda bi, i: (bi, i, x_blk)),
                  pl.BlockSpec((1, tm, LANES), lambda bi, i: (bi, i, kr_blk)),
                  pl.BlockSpec((1, kk), lambda bi, i: (0, 0)),
                  pl.BlockSpec(wk.shape, lambda bi, i: (0, 0)),
                  pl.BlockSpec(wvt.shape, lambda bi, i: (0, 0)), tab, tab, tab],
        out_specs=[pl.BlockSpec((1, MLA_HEADS, tm, MLA_PAD), lambda bi, i: (bi, 0, i, 0)),
                   pl.BlockSpec((1, MLA_HEADS * MLA_V, tm), lambda bi, i: (bi, 0, i))],
        out_shape=[jax.ShapeDtypeStruct((b, MLA_HEADS, t, MLA_PAD), BF16),
                   jax.ShapeDtypeStruct((b, MLA_HEADS * MLA_V, t), BF16)],
        compiler_params=_cp("parallel", "parallel"),
        name="mla_kv",
    )(proj, proj, g.reshape(1, kk), wk, wvt, *tabs)


def _flash_kernel(qt_ref, k_ref, vt_ref, o_ref, s_scr, p_scr):
    qt = qt_ref[0]
    tq = qt.shape[1]
    nc, _, tk = vt_ref.shape[2:]
    m = jnp.full((1, tq), MASK_VALUE, F32)
    l = jnp.zeros((1, tq), F32)
    acc = jnp.zeros((MLA_V, tq), F32)
    s_scr[0] = jnp.dot(k_ref[0, 0, 0:tk, :], qt, preferred_element_type=F32)
    for c in range(nc + 1):
        if c + 1 < nc:
            s_scr[(c + 1) % 2] = jnp.dot(k_ref[0, 0, (c + 1) * tk:(c + 2) * tk, :], qt, preferred_element_type=F32)
        if c >= 1:
            acc = acc + jnp.dot(vt_ref[0, 0, c - 1], p_scr[(c - 1) % 2], preferred_element_type=F32)
        if c < nc:
            st = s_scr[c % 2]
            m_new = jnp.maximum(m, jnp.max(st, axis=0, keepdims=True))
            alpha = jnp.exp2(m - m_new)
            p = jnp.exp2(st - m_new)
            l = alpha * l + jnp.sum(p, axis=0, keepdims=True)
            p_scr[c % 2] = p.astype(BF16)
            acc = alpha * acc
            m = m_new
    o_ref[0] = (acc * (1.0 / l)).T.astype(o_ref.dtype)


def _flash(qt, k, vt, *, tq):
    b, _, lq = qt.shape
    lk = k.shape[2]
    nc, _, tk = vt.shape[2:]
    return pl.pallas_call(
        _flash_kernel,
        grid=(b, MLA_HEADS, lq // tq),
        in_specs=[pl.BlockSpec((1, MLA_PAD, tq), lambda bi, h, i: (bi, h, i)),
                  pl.BlockSpec((1, 1, lk, MLA_PAD), lambda bi, h, i: (bi, h, 0, 0)),
                  pl.BlockSpec((1, 1, nc, MLA_V, tk), lambda bi, h, i: (bi, h, 0, 0, 0))],
        out_specs=pl.BlockSpec((1, tq, MLA_V), lambda bi, h, i: (bi, i, h)),
        out_shape=jax.ShapeDtypeStruct((b, lq, MLA_HEADS * MLA_V), BF16),
        scratch_shapes=[pltpu.VMEM((2, tk, tq), F32), pltpu.VMEM((2, tk, tq), BF16)],
        compiler_params=_cp("parallel", "parallel", "arbitrary"),
        name="mla_flash",
    )(qt, k, vt)


def _sort_network(n):
    pairs, p = [], 1
    while p < n:
        k = p
        while k >= 1:
            for j in range(k % p, n - k, 2 * k):
                for i in range(min(k, n - j - k)):
                    if (i + j) // (2 * p) == (i + j + k) // (2 * p):
                        pairs.append((i + j, i + j + k))
            k //= 2
        p *= 2
    return pairs


SUBLANES = 8


def _stack_rows(rows):
    n = rows[0].shape[1]
    sub = lax.broadcasted_iota(jnp.int32, (SUBLANES, n), 0)
    out = jnp.broadcast_to(rows[0], (SUBLANES, n))
    for r in range(1, len(rows)):
        out = jnp.where(sub == r, rows[r], out)
    return out


def _column_top(st, k):
    groups = st.shape[0] // SUBLANES
    v = [st[SUBLANES * j:SUBLANES * (j + 1), :] for j in range(groups)]
    for i, j in _sort_network(groups):
        v[i], v[j] = jnp.maximum(v[i], v[j]), jnp.minimum(v[i], v[j])
    vals = []
    for t in range(k):
        mx = jnp.max(v[0], axis=0, keepdims=True)
        vals.append(mx)
        pop = v[0] >= mx
        for j in range(k - 1 - t):
            v[j] = jnp.where(pop, v[j + 1] if j + 1 < groups else MASK_VALUE, v[j])
    return vals


def _pair_top(a, b, k):
    lo = _stack_rows(a[:SUBLANES])
    v = [lo + b[j] for j in range(k)]
    hi = [_stack_rows(a[SUBLANES * g:SUBLANES * (g + 1)]) + b[0] for g in range(1, k // SUBLANES)]
    vals = []
    for t in range(k + 1):
        head = v[0]
        for hgrp in hi:
            head = jnp.maximum(head, hgrp)
        mx = jnp.max(head, axis=0, keepdims=True)
        vals.append(mx)
        pop = v[0] >= mx
        for j in range(min(k, k - t)):
            v[j] = jnp.where(pop, v[j + 1] if j + 1 < k else MASK_VALUE, v[j])
        hi = [jnp.where(hgrp >= mx, MASK_VALUE, hgrp) for hgrp in hi]
    return vals


def _peer_route_kernel(x_ref, g_ref, mod_ref, wq_ref, ks_ref, h_ref, r1_ref, e1_ref, cut_ref, e0_ref):
    h2 = _hmod(x_ref[0], g_ref[...], mod_ref[0], 3).astype(BF16)
    h_ref[...] = h2
    qp = jnp.dot(h2, wq_ref[...], preferred_element_type=F32).astype(BF16)
    k = PEER_TOPK
    for h in range(PEER_HEADS):
        s0 = lax.dot_general(ks_ref[2 * h], qp[:, (2 * h) * PEER_HALF:(2 * h + 1) * PEER_HALF], NT_DIMS,
                             preferred_element_type=F32)
        s1 = lax.dot_general(ks_ref[2 * h + 1], qp[:, (2 * h + 1) * PEER_HALF:(2 * h + 2) * PEER_HALF], NT_DIMS,
                             preferred_element_type=F32)
        a = _column_top(s0, k)
        bb = _column_top(s1, k)
        c = _pair_top(a, bb, k)
        z = jnp.zeros_like(c[0])
        for j in range(k):
            z = z + jnp.exp(c[j] - c[0])
        tau = 0.5 * (c[k - 1] + c[k])
        theta = tau - s0
        rank1 = jnp.full(s1.shape, float(k), F32)
        cut = jnp.zeros(s0.shape, F32)
        for j in range(k):
            rank1 = jnp.where(s1 >= bb[k - 1 - j], float(k - 1 - j), rank1)
            cut = jnp.where(bb[j] >= theta, float(j + 1), cut)
        r1_ref[h] = rank1.astype(r1_ref.dtype)
        e1_ref[h] = jnp.exp(s1 - bb[0]).astype(e1_ref.dtype)
        cut_ref[h] = jnp.where(s0 >= a[k - 1], cut, 0.0)
        e0_ref[h] = jnp.exp(s0 - a[0]) * (1.0 / z)


def _peer_route(x, g, mod, wq, kt, *, tm):
    b, t, d = x.shape
    nt = t // tm
    rt = lambda: pl.BlockSpec((PEER_HEADS, PEER_KEYS, tm), lambda bi, i: (0, 0, bi * nt + i))
    rshape = lambda dt: jax.ShapeDtypeStruct((PEER_HEADS, PEER_KEYS, b * t), dt)
    return pl.pallas_call(
        _peer_route_kernel,
        grid=(b, nt),
        in_specs=[pl.BlockSpec((1, tm, d), lambda bi, i: (bi, i, 0)),
                  pl.BlockSpec((1, d), lambda bi, i: (0, 0)),
                  pl.BlockSpec((1, 6, d), lambda bi, i: (bi, 0, 0)),
                  pl.BlockSpec(wq.shape, lambda bi, i: (0, 0)),
                  pl.BlockSpec(kt.shape, lambda bi, i: (0, 0, 0))],
        out_specs=[pl.BlockSpec((tm, d), lambda bi, i: (bi * nt + i, 0)), rt(), rt(), rt(), rt()],
        out_shape=[jax.ShapeDtypeStruct((b * t, d), BF16), rshape(BF16), rshape(BF16), rshape(F32), rshape(F32)],
        compiler_params=_cp("parallel", "parallel"),
        name="peer_route",
    )(x, g.reshape(1, d), mod, wq, kt)


def _gelu_tanh(a):
    return 0.5 * a * (1.0 + jnp.tanh(math.sqrt(2.0 / math.pi) * (a + 0.044715 * (a * a * a))))


def _peer_expert_kernel(*refs, final, ne):
    if final:
        (h_ref, hn_ref, u0_ref, un_ref, vt_ref, r1_ref, e1_ref, cut_ref, e0_ref, x_ref, mod_ref, fn_ref, o_ref,
         acc_scr, at_scr) = refs
    else:
        (h_ref, hn_ref, u0_ref, un_ref, vt_ref, r1_ref, e1_ref, cut_ref, e0_ref, x_ref, mod_ref, o_ref,
         acc_scr, at_scr) = refs
    j = pl.program_id(2)
    first_tile = (pl.program_id(0) == 0) & (pl.program_id(1) == 0)

    def pre_activation(u_ref, tok_ref):
        return lax.dot_general(u_ref[...], tok_ref[...], NT_DIMS, preferred_element_type=F32).astype(BF16)

    @pl.when(j == 0)
    def _():
        acc_scr[...] = jnp.zeros_like(acc_scr)

    @pl.when((j == 0) & first_tile)
    def _():
        at_scr[0] = pre_activation(u0_ref, h_ref)

    def body(cur, nxt, tok_ref):
        at_scr[nxt] = pre_activation(un_ref, tok_ref)
        ws = []
        for il in range(cut_ref.shape[1]):
            gate = None
            for h in range(PEER_HEADS):
                cut = cut_ref[h, il:il + 1, :].astype(BF16)
                e0 = e0_ref[h, il:il + 1, :].astype(BF16)
                term = jnp.where(r1_ref[h] < cut, e0 * e1_ref[h], jnp.zeros((), BF16))
                gate = term if gate is None else gate + term
            ws.append(gate * _gelu_tanh(at_scr[cur, il * PEER_KEYS:(il + 1) * PEER_KEYS, :]))
        acc_scr[...] += jnp.dot(vt_ref[0], jnp.concatenate(ws, axis=0), preferred_element_type=F32)

    assert ne % 2 == 0

    @pl.when(j % 2 == 0)
    def _():
        body(0, 1, h_ref)

    @pl.when((j % 2 == 1) & (j != ne - 1))
    def _():
        body(1, 0, h_ref)

    @pl.when(j == ne - 1)
    def _():
        body(1, 0, hn_ref)

    @pl.when(j == pl.num_programs(2) - 1)
    def _():
        y = x_ref[0] + mod_ref[0, 5:6, :] * acc_scr[...].T
        if final:
            y = _rms_rows(y, fn_ref[...])
        o_ref[0] = y


def _peer_expert(h2, route, u, vt, x, mod, final_gain, *, tt, ec):
    b, t, d = x.shape
    nt = t // tt
    ne = u.shape[0] // ec
    r1, e1, cut, e0 = route
    once = dict(pipeline_mode=pl.Buffered(1))
    rt = lambda: pl.BlockSpec((PEER_HEADS, PEER_KEYS, tt), lambda bi, i, j: (0, 0, bi * nt + i))
    rj = lambda: pl.BlockSpec((PEER_HEADS, ec // PEER_KEYS, tt), lambda bi, i, j: (0, j, bi * nt + i))
    in_specs = [pl.BlockSpec((tt, d), lambda bi, i, j: (bi * nt + i, 0)),
                pl.BlockSpec((tt, d), lambda bi, i, j: (jnp.minimum(bi * nt + i + 1, b * nt - 1), 0)),
                pl.BlockSpec((ec, d), lambda bi, i, j: (0, 0), **once),
                pl.BlockSpec((ec, d), lambda bi, i, j: ((j + 1) % ne, 0)),
                pl.BlockSpec((1, d, ec), lambda bi, i, j: (j, 0, 0)),
                rt(), rt(), rj(), rj(),
                pl.BlockSpec((1, tt, d), lambda bi, i, j: (bi, i, 0), **once),
                pl.BlockSpec((1, 6, d), lambda bi, i, j: (bi, 0, 0))]
    args = [h2, h2, u, u, vt, r1, e1, cut, e0, x, mod]
    if final_gain is not None:
        in_specs.append(pl.BlockSpec((1, d), lambda bi, i, j: (0, 0)))
        args.append(final_gain.reshape(1, d))
    return pl.pallas_call(
        functools.partial(_peer_expert_kernel, final=final_gain is not None, ne=ne),
        grid=(b, nt, ne),
        in_specs=in_specs,
        out_specs=pl.BlockSpec((1, tt, d), lambda bi, i, j: (bi, i, 0)),
        out_shape=jax.ShapeDtypeStruct((b, t, d), F32),
        scratch_shapes=[pltpu.VMEM((d, tt), F32), pltpu.VMEM((2, ec, tt), BF16)],
        compiler_params=_cp("arbitrary", "arbitrary", "arbitrary", vmem=EXPERT_VMEM_LIMIT),
        name="peer_expert",
    )(*args)


def _peer(x, g, mod, wq, kt, u, vt, final_gain, *, tm, tt, ec):
    out = _peer_route(x, g, mod, wq, kt, tm=tm)
    return _peer_expert(out[0], out[1:], u, vt, x, mod, final_gain, tt=tt, ec=ec)


def _lat_tile(t, want):
    while t % want:
        want //= 2
    return want


def kernel(x, c, ctx, c_ctx, w_mod, b_mod, norm1, norm2, even_w_in, even_ret_gn, even_w_out, odd_w_in, odd_sink, odd_mla_q_norm, odd_mla_kv_norm, odd_mla_w_uq, odd_mla_w_ukv, odd_w_out, peer_w_q, peer_sub_keys, peer_u, peer_v, final_norm):
    b, s_len, d = x.shape
    n_ctx = ctx.shape[1]
    depth = w_mod.shape[0]
    assert depth == 2 and even_w_in.shape[0] == 1 and odd_w_in.shape[0] == 1
    tm_l = _lat_tile(s_len, 1024)
    tm_c = n_ctx

    rows = 8 * ((b + 1 + 7) // 8)
    cond = jnp.zeros((rows, d), F32).at[:b].set(c).at[b].set(c_ctx)
    mods = _adaln(cond, w_mod, b_mod).reshape(depth, rows, 6, d)
    mod_l = [mods[l, :b] for l in range(depth)]
    mod_c = [jnp.broadcast_to(mods[l, b:b + 1], (b, 6, d)) for l in range(depth)]

    x_lat, x_ctx = x, ctx

    def peer_params(layer):
        kt = peer_sub_keys[layer].reshape(2 * PEER_HEADS, PEER_KEYS, PEER_HALF).astype(BF16)
        n_exp = peer_v.shape[1]
        vt = jnp.transpose(peer_v[layer].astype(BF16).reshape(n_exp // EXPERT_CHUNK, EXPERT_CHUNK, d), (0, 2, 1))
        return (peer_w_q[layer].astype(BF16), kt, peer_u[layer].astype(BF16), vt)

    w_in = even_w_in[0].astype(BF16)
    w_out = even_w_out[0].astype(BF16)
    p_l = _proj(x_lat, norm1[0], mod_l[0], w_in, si=0, tm=tm_l, tn=1024, name="even_proj_lat")
    p_c = _proj(x_ctx, norm1[0], mod_c[0], w_in, si=0, tm=tm_c, tn=512, name="even_proj_ctx")
    fm_l = _fourier_lat(p_l[..., :FOURIER_WIDTH])
    fm_c = _fourier_small(p_c[..., :FOURIER_WIDTH])
    tabs_l = _rope_tables(s_len, RET_DIM, LANES)
    tabs_c = _rope_tables(n_ctx, RET_DIM, LANES, identity=True)
    zero = jnp.zeros((b, RET_HEADS, RET_DIM, RET_DIM), F32)
    gn = even_ret_gn[0]
    rf_c, st_f = _retention(p_c, tabs_c, gn, zero, None, reverse=False, gate_blk=4)
    r_c, st_b = _retention(p_c, tabs_c, gn, zero, rf_c, reverse=True, gate_blk=5)
    rf_l, _ = _retention(p_l, tabs_l, gn, st_f, None, reverse=False, gate_blk=4)
    r_l, _ = _retention(p_l, tabs_l, gn, st_b, rf_l, reverse=True, gate_blk=5)
    x_lat = _outproj(fm_l, r_l, w_out, x_lat, mod_l[0], tm=tm_l, tn=1024, name="even_out_lat")
    x_ctx = _outproj(fm_c, r_c, w_out, x_ctx, mod_c[0], tm=tm_c, tn=512, name="even_out_ctx")
    pp = peer_params(0)
    tt_l = _lat_tile(s_len, 512)
    x_lat = _peer(x_lat, norm2[0], mod_l[0], *pp, None, tm=256, tt=tt_l, ec=EXPERT_CHUNK)
    x_ctx = _peer(x_ctx, norm2[0], mod_c[0], *pp, None, tm=256, tt=n_ctx, ec=EXPERT_CHUNK)

    wi = odd_w_in[0]
    o_sq, o_sk, o_sv = 0, SWA_Q_HEADS * SWA_HEAD_DIM, (SWA_Q_HEADS + SWA_KV_HEADS) * SWA_HEAD_DIM
    o_cq = o_sv + SWA_KV_HEADS * SWA_HEAD_DIM
    o_ckv = o_cq + MLA_Q_RANK
    o_kr = o_ckv + MLA_KV_RANK
    w_in = jnp.concatenate([wi[:, o_sq:o_sk], wi[:, o_cq:o_ckv], wi[:, o_ckv:o_kr], wi[:, o_sk:o_sv], wi[:, o_sv:o_cq],
                            wi[:, o_kr:], jnp.zeros((d, LANES - MLA_ROPE), wi.dtype)], axis=1).astype(BF16)
    n_odd = w_in.shape[1]
    q_blk, cq_blk, ckv_blk = 0, (o_sk - o_sq) // MLA_Q_RANK, (o_sk - o_sq + MLA_Q_RANK) // MLA_KV_RANK
    k_blk = (o_sk - o_sq + MLA_Q_RANK + MLA_KV_RANK) // LANES
    v_blk, kr_blk = k_blk + 1, k_blk + 2
    tm_o = _lat_tile(s_len, 512)
    p_l = _proj(x_lat, norm1[1], mod_l[1], w_in, si=0, tm=tm_o, tn=n_odd, name="odd_proj_lat")
    p_c = _proj(x_ctx, norm1[1], mod_c[1], w_in, si=0, tm=tm_c, tn=n_odd, name="odd_proj_ctx")
    tabs_swa = _rope_tables(s_len, SWA_HEAD_DIM, LANES)
    swa = _swa(p_l, p_c, tabs_swa, odd_sink[0], q_blk=q_blk, k_blk=k_blk, v_blk=v_blk)
    wuq = odd_mla_w_uq[0].reshape(MLA_Q_RANK, MLA_HEADS, MLA_QK_DIM)
    wuq = jnp.pad(wuq, ((0, 0), (0, 0), (0, MLA_PAD - MLA_QK_DIM))).reshape(MLA_Q_RANK, MLA_HEADS * MLA_PAD)
    wukv = odd_mla_w_ukv[0].reshape(MLA_KV_RANK, MLA_HEADS, MLA_NOPE + MLA_V)
    wk = wukv[:, :, :MLA_NOPE].reshape(MLA_KV_RANK, -1).astype(BF16)
    wvt = wukv[:, :, MLA_NOPE:].reshape(MLA_KV_RANK, -1).T.astype(BF16)
    tabs_mla = _pad_rope_tables(_rope_tables(s_len, MLA_ROPE, MLA_ROPE), LANES)
    tabs_id = _rope_tables(n_ctx, MLA_ROPE, LANES, identity=True)
    qt = _mla_q(p_l, odd_mla_q_norm[0], wuq.T.astype(BF16), tuple(a.T for a in tabs_mla), x_blk=cq_blk, tm=tm_o)
    k_l, vt_l = _mla_kv(p_l, odd_mla_kv_norm[0], wk, wvt, tabs_mla, x_blk=ckv_blk, kr_blk=kr_blk, tm=tm_o)
    k_c, vt_c = _mla_kv(p_c, odd_mla_kv_norm[0], wk, wvt, tabs_id, x_blk=ckv_blk, kr_blk=kr_blk, tm=tm_c)
    k_all = jnp.concatenate([k_c, k_l], axis=2)
    lk = n_ctx + s_len
    tk = max(t for t in (1280, 1024, 768, 512, 256, 128) if lk % t == 0)
    vt_all = jnp.concatenate([vt_c, vt_l], axis=2).reshape(b, MLA_HEADS, MLA_V, lk // tk, tk)
    vt_all = jnp.transpose(vt_all, (0, 1, 3, 2, 4))
    mla = _flash(qt, k_all, vt_all, tq=_lat_tile(s_len, 512))
    x_lat = _outproj(swa, mla, odd_w_out[0].astype(BF16), x_lat, mod_l[1], tm=tm_l, tn=1024, name="odd_out_lat")
    pp = peer_params(1)
    return _peer(x_lat, norm2[1], mod_l[1], *pp, final_norm, tm=256, tt=tt_l, ec=EXPERT_CHUNK)
```

```python
import functools
import math

import numpy as np
import jax
import jax.numpy as jnp
from jax import lax
from jax.experimental import pallas as pl
from jax.experimental.pallas import tpu as pltpu

F32 = jnp.float32
BF16 = jnp.bfloat16

GRID_W = 64
NORM_EPS = 1e-6
ROPE_BASE = 10000.0
MASK_VALUE = -1e30
LANES = 128

FOURIER_GROUPS = 4
FOURIER_GROUP_DIM = 256
FOURIER_WIDTH = FOURIER_GROUPS * FOURIER_GROUP_DIM
RET_HEADS = 8
RET_DIM = 128
RET_CHUNK = 128
RET_WIDTH = RET_HEADS * RET_DIM

SWA_Q_HEADS = 16
SWA_KV_HEADS = 2
SWA_GROUP = SWA_Q_HEADS // SWA_KV_HEADS
SWA_HEAD_DIM = 64
SWA_WINDOW = 128
SWA_BLOCK = 128
MLA_HEADS = 8
MLA_Q_RANK = 512
MLA_KV_RANK = 256
MLA_NOPE = 128
MLA_ROPE = 64
MLA_V = 128
MLA_QK_DIM = MLA_NOPE + MLA_ROPE
MLA_PAD = 2 * LANES

PEER_HEADS = 8
PEER_KEYS = 128
PEER_TOPK = 16
PEER_HALF = 128
EXPERT_CHUNK = 1024

VMEM_LIMIT = 56 * 1024 * 1024
EXPERT_VMEM_LIMIT = 60 * 1024 * 1024


def _cp(*sem, vmem=VMEM_LIMIT):
    return pltpu.CompilerParams(dimension_semantics=sem, vmem_limit_bytes=vmem)


def _adaln_kernel(c_ref, w_ref, b_ref, o_ref):
    c = c_ref[...]
    s = c * jax.nn.sigmoid(c)
    o_ref[0] = jnp.dot(s, w_ref[0], preferred_element_type=F32) + b_ref[0]


def _adaln(cond, w_mod, b_mod):
    depth, d, n = w_mod.shape
    r = cond.shape[0]
    tn = 1024
    return pl.pallas_call(
        _adaln_kernel,
        grid=(depth, n // tn),
        in_specs=[pl.BlockSpec((r, d), lambda l, j: (0, 0)),
                  pl.BlockSpec((1, d, tn), lambda l, j: (l, 0, j)),
                  pl.BlockSpec((1, 1, tn), lambda l, j: (l, 0, j))],
        out_specs=pl.BlockSpec((1, r, tn), lambda l, j: (l, 0, j)),
        out_shape=jax.ShapeDtypeStruct((depth, r, n), F32),
        compiler_params=_cp("parallel", "parallel"),
        name="adaln",
    )(cond, w_mod, b_mod.reshape(depth, 1, n))


def _hmod(x, g, mod, si):
    ms = jnp.mean(x * x, axis=-1, keepdims=True)
    y = x * lax.rsqrt(ms + NORM_EPS) * g
    return y * (1.0 + mod[si + 1:si + 2, :]) + mod[si:si + 1, :]


def _proj_kernel(*refs, si, first_block_out):
    if first_block_out:
        x_ref, g_ref, mod_ref, w_ref, o_ref, first_ref, h_scr = refs
    else:
        x_ref, g_ref, mod_ref, w_ref, o_ref, h_scr = refs

    @pl.when(pl.program_id(2) == 0)
    def _():
        h_scr[...] = _hmod(x_ref[0], g_ref[...], mod_ref[0], si).astype(BF16)

    res = jnp.dot(h_scr[...], w_ref[...], preferred_element_type=F32).astype(o_ref.dtype)
    o_ref[0] = res
    if first_block_out:
        @pl.when(pl.program_id(2) == 0)
        def _():
            first_ref[0] = res


def _proj(x, g, mod, w, *, si, tm, tn, name, first_block_out=False):
    b, t, d = x.shape
    n = w.shape[1]
    out_specs = [pl.BlockSpec((1, tm, tn), lambda bi, i, j: (bi, i, j))]
    out_shape = [jax.ShapeDtypeStruct((b, t, n), BF16)]
    if first_block_out:
        out_specs.append(pl.BlockSpec((1, tm, tn), lambda bi, i, j: (bi, i, 0)))
        out_shape.append(jax.ShapeDtypeStruct((b, t, tn), BF16))
    outs = pl.pallas_call(
        functools.partial(_proj_kernel, si=si, first_block_out=first_block_out),
        grid=(b, t // tm, n // tn),
        in_specs=[pl.BlockSpec((1, tm, d), lambda bi, i, j: (bi, i, 0)),
                  pl.BlockSpec((1, d), lambda bi, i, j: (0, 0)),
                  pl.BlockSpec((1, 6, d), lambda bi, i, j: (bi, 0, 0)),
                  pl.BlockSpec((d, tn), lambda bi, i, j: (0, j))],
        out_specs=out_specs,
        out_shape=out_shape,
        scratch_shapes=[pltpu.VMEM((tm, d), BF16)],
        compiler_params=_cp("parallel", "parallel", "arbitrary"),
        name=name,
    )(x, g.reshape(1, d), mod, w)
    return outs if first_block_out else outs[0]


def _outproj_kernel(a_ref, b_ref, w_ref, x_ref, mod_ref, o_ref):
    ka = a_ref.shape[2]
    acc = jnp.dot(a_ref[0], w_ref[:ka, :], preferred_element_type=F32)
    acc = acc + jnp.dot(b_ref[0], w_ref[ka:, :], preferred_element_type=F32)
    o_ref[0] = x_ref[0] + mod_ref[0, 2:3, :] * acc


def _outproj(a, b2, w, x, mod, *, tm, tn, name):
    b, t, d = x.shape
    ka, kb = a.shape[2], b2.shape[2]
    return pl.pallas_call(
        _outproj_kernel,
        grid=(b, t // tm, d // tn),
        in_specs=[pl.BlockSpec((1, tm, ka), lambda bi, i, j: (bi, i, 0)),
                  pl.BlockSpec((1, tm, kb), lambda bi, i, j: (bi, i, 0)),
                  pl.BlockSpec((ka + kb, tn), lambda bi, i, j: (0, j)),
                  pl.BlockSpec((1, tm, tn), lambda bi, i, j: (bi, i, j)),
                  pl.BlockSpec((1, 6, tn), lambda bi, i, j: (bi, 0, j))],
        out_specs=pl.BlockSpec((1, tm, tn), lambda bi, i, j: (bi, i, j)),
        out_shape=jax.ShapeDtypeStruct((b, t, d), F32),
        compiler_params=_cp("parallel", "parallel", "arbitrary"),
        name=name,
    )(a, b2, w, x, mod)


def _rope_tables(n_pos, d_rot, width, identity=False):
    q = d_rot // 4
    if identity:
        return (jnp.ones((n_pos, width), F32), jnp.zeros((n_pos, width), F32), jnp.zeros((n_pos, width), F32))
    d_axis = d_rot // 2
    inv = ROPE_BASE ** (-jnp.arange(0, d_axis, 2, dtype=F32) / d_axis)
    t = jnp.arange(n_pos)
    row = (t // GRID_W).astype(F32)
    col = (t % GRID_W).astype(F32)
    ar = row[:, None] * inv[None, :]
    ac = col[:, None] * inv[None, :]
    ang = jnp.concatenate([ar, ar, ac, ac], axis=-1)
    cos, sin = jnp.cos(ang), jnp.sin(ang)
    quarter = np.arange(d_rot) // q
    up = jnp.asarray((quarter % 2 == 1).astype(np.float32))
    dn = jnp.asarray((quarter % 2 == 0).astype(np.float32))
    sin_p, sin_m = sin * up, -sin * dn
    reps = width // d_rot
    return tuple(jnp.tile(a, (1, reps)) for a in (cos, sin_p, sin_m))


def _pad_rope_tables(tabs, width):
    cos, sp, sm = tabs
    pad = width - cos.shape[1]
    return (jnp.pad(cos, ((0, 0), (0, pad)), constant_values=1.0), jnp.pad(sp, ((0, 0), (0, pad))),
            jnp.pad(sm, ((0, 0), (0, pad))))


def _rope(t, cos, sp, sm, q):
    w = t.shape[-1]
    return t * cos + pltpu.roll(t, q, 1) * sp + pltpu.roll(t, w - q, 1) * sm


def _dft_mats(n):
    k = np.arange(n)
    ang = 2.0 * np.pi * ((k[:, None] * k[None, :]) % n) / n
    return np.cos(ang), np.sin(ang)


def _fourier_a_kernel(u_ref, fa_ref, tc_ref, ts_ref, o_ref):
    l1 = u_ref.shape[1]
    t1 = jnp.dot(fa_ref[...], u_ref[0], preferred_element_type=F32)
    tr, ti = t1[:l1], t1[l1:]
    reps = tr.shape[1] // LANES
    c = jnp.concatenate([tc_ref[0]] * reps, axis=1)
    s = jnp.concatenate([ts_ref[0]] * reps, axis=1)
    o_ref[0, 0] = (tr * c + ti * s).astype(o_ref.dtype)
    o_ref[0, 1] = (ti * c - tr * s).astype(o_ref.dtype)


def _fourier_b_kernel(t_ref, fb_ref, cs_ref, o_ref):
    l2 = t_ref.shape[2]
    t2 = jnp.concatenate([t_ref[0, 0], t_ref[0, 1]], axis=0)
    v = jnp.dot(fb_ref[...], t2, preferred_element_type=F32).astype(BF16)
    vr, vi = v[:l2], v[l2:]
    outs = []
    for g in range(FOURIER_GROUPS):
        sl = slice(g * FOURIER_GROUP_DIM, (g + 1) * FOURIER_GROUP_DIM)
        vg = jnp.concatenate([vr[:, sl], vi[:, sl]], axis=1)
        outs.append(jnp.dot(vg, cs_ref[...], preferred_element_type=F32))
    o_ref[0] = jnp.concatenate(outs, axis=1).astype(o_ref.dtype)


def _fourier_lat(u):
    b, l, w = u.shape
    l1 = int(round(math.sqrt(l)))
    l2 = l // l1
    assert l1 * l2 == l and l1 % 16 == 0 and l2 % 16 == 0
    ca, sa = _dft_mats(l1)
    fa = jnp.asarray(np.concatenate([ca, -sa], axis=0), BF16)
    kk = np.arange(l1)[None, :, None] * np.arange(l2)[:, None, None]
    ang = 2.0 * np.pi * (kk % l) / l
    tc = jnp.asarray(np.broadcast_to(np.cos(ang), (l2, l1, LANES)), F32)
    ts = jnp.asarray(np.broadcast_to(np.sin(ang), (l2, l1, LANES)), F32)
    t2 = pl.pallas_call(
        _fourier_a_kernel,
        grid=(b, l2),
        in_specs=[pl.BlockSpec((1, l1, w), lambda bi, j: (bi, 0, j)),
                  pl.BlockSpec((2 * l1, l1), lambda bi, j: (0, 0)),
                  pl.BlockSpec((1, l1, LANES), lambda bi, j: (j, 0, 0)),
                  pl.BlockSpec((1, l1, LANES), lambda bi, j: (j, 0, 0))],
        out_specs=pl.BlockSpec((1, 2, l1, w), lambda bi, j: (bi, 0, 0, j)),
        out_shape=jax.ShapeDtypeStruct((b, 2, l1, l2 * w), BF16),
        compiler_params=_cp("parallel", "parallel"),
        name="fourier_a",
    )(u.reshape(b, l1, l2 * w), fa, tc, ts)
    cb, sb = _dft_mats(l2)
    scale = 1.0 / math.sqrt(l * FOURIER_GROUP_DIM)
    fb = jnp.asarray(np.block([[cb, sb], [-sb, cb]]) * scale, BF16)
    cc, sc = _dft_mats(FOURIER_GROUP_DIM)
    cs = jnp.asarray(np.concatenate([cc, sc], axis=0), BF16)
    y = pl.pallas_call(
        _fourier_b_kernel,
        grid=(b, l1),
        in_specs=[pl.BlockSpec((1, 2, l2, w), lambda bi, j: (bi, 0, j, 0)),
                  pl.BlockSpec((2 * l2, 2 * l2), lambda bi, j: (0, 0)),
                  pl.BlockSpec((2 * FOURIER_GROUP_DIM, FOURIER_GROUP_DIM), lambda bi, j: (0, 0))],
        out_specs=pl.BlockSpec((1, l2, w), lambda bi, j: (bi, 0, j)),
        out_shape=jax.ShapeDtypeStruct((b, l2, l1 * w), BF16),
        compiler_params=_cp("parallel", "parallel"),
        name="fourier_b",
    )(t2.reshape(b, 2, l1 * l2, w), fb, cs)
    return y.reshape(b, l, w)


def _fourier_small_kernel(u_ref, cl_ref, sl_ref, cs_ref, o_ref):
    gd = FOURIER_GROUP_DIM
    outs = []
    for g in range(FOURIER_GROUPS):
        ug = u_ref[0][:, g * gd:(g + 1) * gd]
        pq = jnp.dot(ug, cs_ref[...], preferred_element_type=F32).astype(BF16)
        y = jnp.dot(cl_ref[...], pq[:, :gd], preferred_element_type=F32)
        y = y - jnp.dot(sl_ref[...], pq[:, gd:], preferred_element_type=F32)
        outs.append(y)
    o_ref[0] = jnp.concatenate(outs, axis=1).astype(o_ref.dtype)


def _fourier_small(u):
    b, l, w = u.shape
    cl, sl = _dft_mats(l)
    scale = 1.0 / math.sqrt(l * FOURIER_GROUP_DIM)
    cc, sc = _dft_mats(FOURIER_GROUP_DIM)
    cs = jnp.asarray(np.concatenate([cc, sc], axis=1), BF16)
    return pl.pallas_call(
        _fourier_small_kernel,
        grid=(b,),
        in_specs=[pl.BlockSpec((1, l, w), lambda bi: (bi, 0, 0)),
                  pl.BlockSpec((l, l), lambda bi: (0, 0)),
                  pl.BlockSpec((l, l), lambda bi: (0, 0)),
                  pl.BlockSpec((FOURIER_GROUP_DIM, 2 * FOURIER_GROUP_DIM), lambda bi: (0, 0))],
        out_specs=pl.BlockSpec((1, l, w), lambda bi: (bi, 0, 0)),
        out_shape=jax.ShapeDtypeStruct((b, l, w), BF16),
        compiler_params=_cp("parallel"),
        name="fourier_small",
    )(u, jnp.asarray(cl * scale, BF16), jnp.asarray(sl * scale, BF16), cs)


def _ret_consts(reverse):
    c = RET_CHUNK
    lg = np.log(1.0 - np.exp2(-5.0 - np.arange(RET_HEADS, dtype=np.float32))).astype(np.float32)
    pos = np.arange(c, dtype=np.float32)
    diff = pos[:, None] - pos[None, :]
    if reverse:
        diff = -diff
    decay = np.where(diff >= 0, np.exp(lg[:, None, None] * np.maximum(diff, 0.0)), 0.0)
    if reverse:
        zeta = np.exp(lg[None, :] * pos[:, None])
        xi = np.exp(lg[None, :] * (c - pos[:, None]))
    else:
        zeta = np.exp(lg[None, :] * (c - 1.0 - pos[:, None]))
        xi = np.exp(lg[None, :] * (pos[:, None] + 1.0))
    wide = lambda a: np.repeat(a, RET_DIM, axis=1)
    chunk_decay = [float(v) for v in np.exp(lg * c)]
    return (jnp.asarray(decay, F32), jnp.asarray(wide(zeta), F32), jnp.asarray(wide(xi), F32), chunk_decay)


def _ret_kernel(*refs, chunk_decay, has_prev):
    if has_prev:
        (q_ref, k_ref, v_ref, g_ref, cos_ref, sp_ref, sm_ref, dec_ref, zeta_ref, xi_ref, gn_ref, s0_ref, prev_ref,
         o_ref, sf_ref, st_scr) = refs
    else:
        (q_ref, k_ref, v_ref, g_ref, cos_ref, sp_ref, sm_ref, dec_ref, zeta_ref, xi_ref, gn_ref, s0_ref,
         o_ref, sf_ref, st_scr) = refs
        prev_ref = None
    c = pl.program_id(1)

    @pl.when(c == 0)
    def _():
        st_scr[...] = s0_ref[0]

    cos, sp, sm = cos_ref[...], sp_ref[...], sm_ref[...]
    outs = []
    for h in range(RET_HEADS):
        sl = slice(h * RET_DIM, (h + 1) * RET_DIM)
        q = _rope(q_ref[0][:, sl].astype(F32), cos, sp, sm, RET_DIM // 4)
        k = _rope(k_ref[0][:, sl].astype(F32), cos, sp, sm, RET_DIM // 4) * (RET_DIM ** -0.5)
        v = v_ref[0][:, sl].astype(F32)
        kt = k.T.astype(BF16)
        scores = jnp.dot(q.astype(BF16), kt, preferred_element_type=F32) * dec_ref[h]
        inner = jnp.dot(scores.astype(BF16), v.astype(BF16), preferred_element_type=F32)
        state = st_scr[h]
        cross = jnp.dot((q * xi_ref[:, sl]).astype(BF16), state.astype(BF16), preferred_element_type=F32)
        o = inner + cross
        st_scr[h] = state * chunk_decay[h] + jnp.dot(kt, (v * zeta_ref[:, sl]).astype(BF16),
                                                      preferred_element_type=F32)
        y = o * lax.rsqrt(jnp.mean(o * o, axis=-1, keepdims=True) + NORM_EPS) * gn_ref[:, sl]
        gate = g_ref[0][:, sl].astype(F32)
        outs.append(y * (gate * jax.nn.sigmoid(gate)))
    res = jnp.concatenate(outs, axis=1)
    if prev_ref is not None:
        res = res + prev_ref[0].astype(F32)
    o_ref[0] = res.astype(o_ref.dtype)

    @pl.when(c == pl.num_programs(1) - 1)
    def _():
        sf_ref[0] = st_scr[...]


def _retention(proj, tabs, gn, state0, prev, *, reverse, gate_blk):
    b, t, _ = proj.shape
    w = RET_WIDTH
    n = t // RET_CHUNK
    decay, zeta, xi, chunk_decay = _ret_consts(reverse)
    pos = (lambda ci: n - 1 - ci) if reverse else (lambda ci: ci)
    blk = lambda col: pl.BlockSpec((1, RET_CHUNK, w), lambda bi, ci: (bi, pos(ci), col))
    tab = pl.BlockSpec((RET_CHUNK, LANES), lambda bi, ci: (pos(ci), 0))
    const2 = pl.BlockSpec((RET_CHUNK, w), lambda bi, ci: (0, 0))
    st = pl.BlockSpec((1, RET_HEADS, RET_DIM, RET_DIM), lambda bi, ci: (bi, 0, 0, 0))
    in_specs = [blk(1), blk(2), blk(3), blk(gate_blk), tab, tab, tab,
                pl.BlockSpec((RET_HEADS, RET_CHUNK, RET_CHUNK), lambda bi, ci: (0, 0, 0)), const2, const2,
                pl.BlockSpec((1, w), lambda bi, ci: (0, 0)), st]
    args = [proj, proj, proj, proj, *tabs, decay, zeta, xi, gn.reshape(1, w), state0]
    if prev is not None:
        in_specs.append(pl.BlockSpec((1, RET_CHUNK, w), lambda bi, ci: (bi, pos(ci), 0)))
        args.append(prev)
    return pl.pallas_call(
        functools.partial(_ret_kernel, chunk_decay=chunk_decay, has_prev=prev is not None),
        grid=(b, n),
        in_specs=in_specs,
        out_specs=[pl.BlockSpec((1, RET_CHUNK, w), lambda bi, ci: (bi, pos(ci), 0)), st],
        out_shape=[jax.ShapeDtypeStruct((b, t, w), BF16),
                   jax.ShapeDtypeStruct((b, RET_HEADS, RET_DIM, RET_DIM), F32)],
        scratch_shapes=[pltpu.VMEM((RET_HEADS, RET_DIM, RET_DIM), F32)],
        compiler_params=_cp("parallel", "arbitrary"),
        name="retention_bwd" if reverse else "retention_fwd",
    )(*args)


def _swa_kernel(q_ref, kc_ref, vc_ref, kl_ref, km_ref, kr_ref, vl_ref, vm_ref, vr_ref,
                cq_ref, spq_ref, smq_ref, cl_ref, spl_ref, sml_ref, cm_ref, spm_ref, smm_ref,
                cr_ref, spr_ref, smr_ref, sink_ref, o_ref, s_scr, *, seq_len):
    i = pl.program_id(1)
    qd = SWA_HEAD_DIM // 4
    hd = SWA_HEAD_DIM
    cq, spq, smq = cq_ref[...], spq_ref[...], smq_ref[...]
    rk = lambda kref, c, sp, sm: _rope(kref[0].astype(F32), c[...], sp[...], sm[...], qd)
    k_all = jnp.concatenate([kc_ref[0].astype(F32), rk(kl_ref, cl_ref, spl_ref, sml_ref),
                             rk(km_ref, cm_ref, spm_ref, smm_ref), rk(kr_ref, cr_ref, spr_ref, smr_ref)], axis=0)
    v_all = jnp.concatenate([vc_ref[0], vl_ref[0], vm_ref[0], vr_ref[0]], axis=0).astype(F32)
    vt = v_all.T
    n_ctx = kc_ref.shape[1]
    n_keys = k_all.shape[0]
    n_loc = 3 * SWA_BLOCK
    gw = SWA_GROUP * SWA_BLOCK
    cidx = lax.broadcasted_iota(jnp.int32, (n_loc, gw), 0)
    r = lax.broadcasted_iota(jnp.int32, (n_loc, gw), 1) % SWA_BLOCK
    kpos = (i - 1) * SWA_BLOCK + cidx
    valid = (jnp.abs(SWA_BLOCK + r - cidx) <= SWA_WINDOW) & (kpos >= 0) & (kpos < seq_len)
    klane = lax.broadcasted_iota(jnp.int32, (n_keys, LANES), 1)
    vrow = lax.broadcasted_iota(jnp.int32, (LANES, n_keys), 0)
    qt_blk = [(_rope(q_ref[0][:, blk * LANES:(blk + 1) * LANES].astype(F32), cq, spq, smq, qd) * (hd ** -0.5)).T
              for blk in range(SWA_Q_HEADS // 2)]
    zeros = jnp.zeros((hd, gw), F32)
    for kv in range(SWA_KV_HEADS):
        pieces = []
        for h in range(kv * SWA_GROUP, (kv + 1) * SWA_GROUP):
            pieces.append(qt_blk[h // 2][(h % 2) * hd:(h % 2 + 1) * hd, :])
        qg = jnp.concatenate(pieces, axis=1)
        qg = jnp.concatenate([qg, zeros] if kv == 0 else [zeros, qg], axis=0).astype(BF16)
        in_kv = (klane < hd) if kv == 0 else (klane >= hd)
        s_scr[kv] = jnp.dot(jnp.where(in_kv, k_all, 0.0).astype(BF16), qg, preferred_element_type=F32)
    outs = []
    for kv in range(SWA_KV_HEADS):
        st = s_scr[kv]
        st = jnp.concatenate([st[:n_ctx], jnp.where(valid, st[n_ctx:], MASK_VALUE)], axis=0)
        sk = sink_ref[kv]
        m = jnp.maximum(jnp.max(st, axis=0, keepdims=True), sk)
        e = jnp.exp(st - m)
        p = e * (1.0 / (jnp.sum(e, axis=0, keepdims=True) + jnp.exp(sk - m)))
        v_kv = jnp.where((vrow < hd) if kv == 0 else (vrow >= hd), vt, 0.0).astype(BF16)
        ot = jnp.dot(v_kv, p.astype(BF16), preferred_element_type=F32)[kv * hd:(kv + 1) * hd, :]
        stacked = jnp.concatenate([ot[:, g * SWA_BLOCK:(g + 1) * SWA_BLOCK] for g in range(SWA_GROUP)], axis=0)
        outs.append(stacked.T)
    o_ref[0] = jnp.concatenate(outs, axis=1).astype(o_ref.dtype)


def _swa(proj_l, proj_c, tabs, sink, *, q_blk, k_blk, v_blk):
    b, l, _ = proj_l.shape
    n_ctx = proj_c.shape[1]
    nb = l // SWA_BLOCK
    qw = SWA_Q_HEADS * SWA_HEAD_DIM
    left = lambda i: jnp.maximum(i - 1, 0)
    right = lambda i: jnp.minimum(i + 1, nb - 1)
    kvspec = lambda col, f: pl.BlockSpec((1, SWA_BLOCK, LANES), lambda bi, i: (bi, f(i), col))
    ctxspec = lambda col: pl.BlockSpec((1, n_ctx, LANES), lambda bi, i: (bi, 0, col))
    tab = lambda f: pl.BlockSpec((SWA_BLOCK, LANES), lambda bi, i: (f(i), 0))
    ident = lambda i: i
    in_specs = [pl.BlockSpec((1, SWA_BLOCK, qw), lambda bi, i: (bi, i, q_blk)),
                ctxspec(k_blk), ctxspec(v_blk),
                kvspec(k_blk, left), kvspec(k_blk, ident), kvspec(k_blk, right),
                kvspec(v_blk, left), kvspec(v_blk, ident), kvspec(v_blk, right)]
    args = [proj_l, proj_c, proj_c, proj_l, proj_l, proj_l, proj_l, proj_l, proj_l]
    for f in (ident, left, ident, right):
        in_specs += [tab(f)] * 3
        args += list(tabs)
    gw = SWA_GROUP * SWA_BLOCK
    in_specs.append(pl.BlockSpec((SWA_KV_HEADS, 1, gw), lambda bi, i: (0, 0, 0)))
    args.append(jnp.repeat(sink.reshape(SWA_KV_HEADS, SWA_GROUP).astype(F32), SWA_BLOCK, axis=1)
                .reshape(SWA_KV_HEADS, 1, gw))
    return pl.pallas_call(
        functools.partial(_swa_kernel, seq_len=l),
        grid=(b, nb),
        in_specs=in_specs,
        out_specs=pl.BlockSpec((1, SWA_BLOCK, qw), lambda bi, i: (bi, i, 0)),
        out_shape=jax.ShapeDtypeStruct((b, l, qw), BF16),
        scratch_shapes=[pltpu.VMEM((SWA_KV_HEADS, n_ctx + 3 * SWA_BLOCK, gw), F32)],
        compiler_params=_cp("parallel", "parallel"),
        name="swa",
    )(*args)


def _rms_rows(x, g):
    return x * lax.rsqrt(jnp.mean(x * x, axis=-1, keepdims=True) + NORM_EPS) * g


NT_DIMS = (((1,), (1,)), ((), ()))


def _mla_q_kernel(x_ref, g_ref, wt_ref, cos_ref, sp_ref, sm_ref, o_ref):
    x = _rms_rows(x_ref[0].astype(F32), g_ref[...]).astype(BF16)
    mqt = lax.dot_general(wt_ref[...], x, NT_DIMS, preferred_element_type=F32)
    cos, sp, sm = cos_ref[...], sp_ref[...], sm_ref[...]
    scale = MLA_QK_DIM ** -0.5 * math.log2(math.e)
    q = MLA_ROPE // 4
    for h in range(MLA_HEADS):
        o_ref[0, h * MLA_PAD:h * MLA_PAD + LANES, :] = (mqt[h * MLA_PAD:h * MLA_PAD + LANES] * scale).astype(o_ref.dtype)
        r = mqt[h * MLA_PAD + LANES:(h + 1) * MLA_PAD]
        r = r * cos + pltpu.roll(r, q, 0) * sp + pltpu.roll(r, LANES - q, 0) * sm
        o_ref[0, h * MLA_PAD + LANES:(h + 1) * MLA_PAD, :] = (r * scale).astype(o_ref.dtype)


def _mla_q(proj, g, wt, tabs_t, *, x_blk, tm):
    b, t, _ = proj.shape
    n, kq = wt.shape
    tab = pl.BlockSpec((LANES, tm), lambda bi, i: (0, i))
    return pl.pallas_call(
        _mla_q_kernel,
        grid=(b, t // tm),
        in_specs=[pl.BlockSpec((1, tm, kq), lambda bi, i: (bi, i, x_blk)),
                  pl.BlockSpec((1, kq), lambda bi, i: (0, 0)),
                  pl.BlockSpec((n, kq), lambda bi, i: (0, 0)), tab, tab, tab],
        out_specs=pl.BlockSpec((1, n, tm), lambda bi, i: (bi, 0, i)),
        out_shape=jax.ShapeDtypeStruct((b, n, t), BF16),
        compiler_params=_cp("parallel", "parallel"),
        name="mla_q",
    )(proj, g.reshape(1, kq), wt, *tabs_t)


def _mla_kv_kernel(x_ref, kr_ref, g_ref, wk_ref, wvt_ref, cos_ref, sp_ref, sm_ref, k_ref, vt_ref):
    x = _rms_rows(x_ref[0].astype(F32), g_ref[...]).astype(BF16)
    kn = jnp.dot(x, wk_ref[...], preferred_element_type=F32)
    kr = _rope(kr_ref[0].astype(F32), cos_ref[...], sp_ref[...], sm_ref[...], MLA_ROPE // 4)
    for h in range(MLA_HEADS):
        k_ref[0, h] = jnp.concatenate([kn[:, h * MLA_NOPE:(h + 1) * MLA_NOPE], kr], axis=1).astype(k_ref.dtype)
    vt_ref[0] = lax.dot_general(wvt_ref[...], x, NT_DIMS, preferred_element_type=F32).astype(vt_ref.dtype)


def _mla_kv(proj, g, wk, wvt, tabs, *, x_blk, kr_blk, tm):
    b, t, _ = proj.shape
    kk = wk.shape[0]
    tab = pl.BlockSpec((tm, LANES), lambda bi, i: (i, 0))
    return pl.pallas_call(
        _mla_kv_kernel,
        grid=(b, t // tm),
        in_specs=[pl.BlockSpec((1, tm, kk), lambda bi, i: (bi, i, x_blk)),
                  pl.BlockSpec((1, tm, LANES), lambda bi, i: (bi, i, kr_blk)),
                  pl.BlockSpec((1, kk), lambda bi, i: (0, 0)),
                  pl.BlockSpec(wk.shape, lambda bi, i: (0, 0)),
                  pl.BlockSpec(wvt.shape, lambda bi, i: (0, 0)), tab, tab, tab],
        out_specs=[pl.BlockSpec((1, MLA_HEADS, tm, MLA_PAD), lambda bi, i: (bi, 0, i, 0)),
                   pl.BlockSpec((1, MLA_HEADS * MLA_V, tm), lambda bi, i: (bi, 0, i))],
        out_shape=[jax.ShapeDtypeStruct((b, MLA_HEADS, t, MLA_PAD), BF16),
                   jax.ShapeDtypeStruct((b, MLA_HEADS * MLA_V, t), BF16)],
        compiler_params=_cp("parallel", "parallel"),
        name="mla_kv",
    )(proj, proj, g.reshape(1, kk), wk, wvt, *tabs)


def _flash_kernel(qt_ref, k_ref, vt_ref, o_ref, s_scr, p_scr):
    qt = qt_ref[0]
    tq = qt.shape[1]
    nc, _, tk = vt_ref.shape[2:]
    m = jnp.full((1, tq), MASK_VALUE, F32)
    l = jnp.zeros((1, tq), F32)
    acc = jnp.zeros((MLA_V, tq), F32)
    s_scr[0] = jnp.dot(k_ref[0, 0, 0:tk, :], qt, preferred_element_type=F32)
    for c in range(nc + 1):
        if c + 1 < nc:
            s_scr[(c + 1) % 2] = jnp.dot(k_ref[0, 0, (c + 1) * tk:(c + 2) * tk, :], qt, preferred_element_type=F32)
        if c >= 1:
            acc = acc + jnp.dot(vt_ref[0, 0, c - 1], p_scr[(c - 1) % 2], preferred_element_type=F32)
        if c < nc:
            st = s_scr[c % 2]
            m_new = jnp.maximum(m, jnp.max(st, axis=0, keepdims=True))
            alpha = jnp.exp2(m - m_new)
            p = jnp.exp2(st - m_new)
            l = alpha * l + jnp.sum(p, axis=0, keepdims=True)
            p_scr[c % 2] = p.astype(BF16)
            acc = alpha * acc
            m = m_new
    o_ref[0] = (acc * (1.0 / l)).T.astype(o_ref.dtype)


def _flash(qt, k, vt, *, tq):
    b, _, lq = qt.shape
    lk = k.shape[2]
    nc, _, tk = vt.shape[2:]
    return pl.pallas_call(
        _flash_kernel,
        grid=(b, MLA_HEADS, lq // tq),
        in_specs=[pl.BlockSpec((1, MLA_PAD, tq), lambda bi, h, i: (bi, h, i)),
                  pl.BlockSpec((1, 1, lk, MLA_PAD), lambda bi, h, i: (bi, h, 0, 0)),
                  pl.BlockSpec((1, 1, nc, MLA_V, tk), lambda bi, h, i: (bi, h, 0, 0, 0))],
        out_specs=pl.BlockSpec((1, tq, MLA_V), lambda bi, h, i: (bi, i, h)),
        out_shape=jax.ShapeDtypeStruct((b, lq, MLA_HEADS * MLA_V), BF16),
        scratch_shapes=[pltpu.VMEM((2, tk, tq), F32), pltpu.VMEM((2, tk, tq), BF16)],
        compiler_params=_cp("parallel", "parallel", "arbitrary"),
        name="mla_flash",
    )(qt, k, vt)


def _sort_network(n):
    pairs, p = [], 1
    while p < n:
        k = p
        while k >= 1:
            for j in range(k % p, n - k, 2 * k):
                for i in range(min(k, n - j - k)):
                    if (i + j) // (2 * p) == (i + j + k) // (2 * p):
                        pairs.append((i + j, i + j + k))
            k //= 2
        p *= 2
    return pairs


SUBLANES = 8


def _stack_rows(rows):
    n = rows[0].shape[1]
    sub = lax.broadcasted_iota(jnp.int32, (SUBLANES, n), 0)
    out = jnp.broadcast_to(rows[0], (SUBLANES, n))
    for r in range(1, len(rows)):
        out = jnp.where(sub == r, rows[r], out)
    return out


def _column_top(st, k):
    groups = st.shape[0] // SUBLANES
    v = [st[SUBLANES * j:SUBLANES * (j + 1), :] for j in range(groups)]
    for i, j in _sort_network(groups):
        v[i], v[j] = jnp.maximum(v[i], v[j]), jnp.minimum(v[i], v[j])
    vals = []
    for t in range(k):
        mx = jnp.max(v[0], axis=0, keepdims=True)
        vals.append(mx)
        pop = v[0] >= mx
        for j in range(k - 1 - t):
            v[j] = jnp.where(pop, v[j + 1] if j + 1 < groups else MASK_VALUE, v[j])
    return vals


def _pair_top(a, b, k):
    lo = _stack_rows(a[:SUBLANES])
    v = [lo + b[j] for j in range(k)]
    hi = [_stack_rows(a[SUBLANES * g:SUBLANES * (g + 1)]) + b[0] for g in range(1, k // SUBLANES)]
    vals = []
    for t in range(k + 1):
        head = v[0]
        for hgrp in hi:
            head = jnp.maximum(head, hgrp)
        mx = jnp.max(head, axis=0, keepdims=True)
        vals.append(mx)
        pop = v[0] >= mx
        for j in range(min(k, k - t)):
            v[j] = jnp.where(pop, v[j + 1] if j + 1 < k else MASK_VALUE, v[j])
        hi = [jnp.where(hgrp >= mx, MASK_VALUE, hgrp) for hgrp in hi]
    return vals


def _peer_route_kernel(x_ref, g_ref, mod_ref, wq_ref, ks_ref, h_ref, r1_ref, e1_ref, cut_ref, e0_ref):
    h2 = _hmod(x_ref[0], g_ref[...], mod_ref[0], 3).astype(BF16)
    h_ref[...] = h2
    qp = jnp.dot(h2, wq_ref[...], preferred_element_type=F32).astype(BF16)
    k = PEER_TOPK
    for h in range(PEER_HEADS):
        s0 = lax.dot_general(ks_ref[2 * h], qp[:, (2 * h) * PEER_HALF:(2 * h + 1) * PEER_HALF], NT_DIMS,
                             preferred_element_type=F32)
        s1 = lax.dot_general(ks_ref[2 * h + 1], qp[:, (2 * h + 1) * PEER_HALF:(2 * h + 2) * PEER_HALF], NT_DIMS,
                             preferred_element_type=F32)
        a = _column_top(s0, k)
        bb = _column_top(s1, k)
        c = _pair_top(a, bb, k)
        z = jnp.zeros_like(c[0])
        for j in range(k):
            z = z + jnp.exp(c[j] - c[0])
        tau = 0.5 * (c[k - 1] + c[k])
        theta = tau - s0
        rank1 = jnp.full(s1.shape, float(k), F32)
        cut = jnp.zeros(s0.shape, F32)
        for j in range(k):
            rank1 = jnp.where(s1 >= bb[k - 1 - j], float(k - 1 - j), rank1)
            cut = jnp.where(bb[j] >= theta, float(j + 1), cut)
        r1_ref[h] = rank1.astype(r1_ref.dtype)
        e1_ref[h] = jnp.exp(s1 - bb[0]).astype(e1_ref.dtype)
        cut_ref[h] = jnp.where(s0 >= a[k - 1], cut, 0.0)
        e0_ref[h] = jnp.exp(s0 - a[0]) * (1.0 / z)


def _peer_route(x, g, mod, wq, kt, *, tm):
    b, t, d = x.shape
    nt = t // tm
    rt = lambda: pl.BlockSpec((PEER_HEADS, PEER_KEYS, tm), lambda bi, i: (0, 0, bi * nt + i))
    rshape = lambda dt: jax.ShapeDtypeStruct((PEER_HEADS, PEER_KEYS, b * t), dt)
    return pl.pallas_call(
        _peer_route_kernel,
        grid=(b, nt),
        in_specs=[pl.BlockSpec((1, tm, d), lambda bi, i: (bi, i, 0)),
                  pl.BlockSpec((1, d), lambda bi, i: (0, 0)),
                  pl.BlockSpec((1, 6, d), lambda bi, i: (bi, 0, 0)),
                  pl.BlockSpec(wq.shape, lambda bi, i: (0, 0)),
                  pl.BlockSpec(kt.shape, lambda bi, i: (0, 0, 0))],
        out_specs=[pl.BlockSpec((tm, d), lambda bi, i: (bi * nt + i, 0)), rt(), rt(), rt(), rt()],
        out_shape=[jax.ShapeDtypeStruct((b * t, d), BF16), rshape(BF16), rshape(BF16), rshape(F32), rshape(F32)],
        compiler_params=_cp("parallel", "parallel"),
        name="peer_route",
    )(x, g.reshape(1, d), mod, wq, kt)


def _gelu_tanh(a):
    return 0.5 * a * (1.0 + jnp.tanh(math.sqrt(2.0 / math.pi) * (a + 0.044715 * (a * a * a))))


def _peer_expert_kernel(*refs, final, ne):
    if final:
        (h_ref, hn_ref, u0_ref, un_ref, vt_ref, r1_ref, e1_ref, cut_ref, e0_ref, x_ref, mod_ref, fn_ref, o_ref,
         acc_scr, at_scr) = refs
    else:
        (h_ref, hn_ref, u0_ref, un_ref, vt_ref, r1_ref, e1_ref, cut_ref, e0_ref, x_ref, mod_ref, o_ref,
         acc_scr, at_scr) = refs
    j = pl.program_id(2)
    first_tile = (pl.program_id(0) == 0) & (pl.program_id(1) == 0)

    def pre_activation(u_ref, tok_ref):
        return lax.dot_general(u_ref[...], tok_ref[...], NT_DIMS, preferred_element_type=F32).astype(BF16)

    @pl.when(j == 0)
    def _():
        acc_scr[...] = jnp.zeros_like(acc_scr)

    @pl.when((j == 0) & first_tile)
    def _():
        at_scr[0] = pre_activation(u0_ref, h_ref)

    def body(cur, nxt, tok_ref):
        at_scr[nxt] = pre_activation(un_ref, tok_ref)
        ws = []
        for il in range(cut_ref.shape[1]):
            gate = None
            for h in range(PEER_HEADS):
                cut = cut_ref[h, il:il + 1, :].astype(BF16)
                e0 = e0_ref[h, il:il + 1, :].astype(BF16)
                term = jnp.where(r1_ref[h] < cut, e0 * e1_ref[h], jnp.zeros((), BF16))
                gate = term if gate is None else gate + term
            ws.append(gate * _gelu_tanh(at_scr[cur, il * PEER_KEYS:(il + 1) * PEER_KEYS, :]))
        acc_scr[...] += jnp.dot(vt_ref[0], jnp.concatenate(ws, axis=0), preferred_element_type=F32)

    assert ne % 2 == 0

    @pl.when(j % 2 == 0)
    def _():
        body(0, 1, h_ref)

    @pl.when((j % 2 == 1) & (j != ne - 1))
    def _():
        body(1, 0, h_ref)

    @pl.when(j == ne - 1)
    def _():
        body(1, 0, hn_ref)

    @pl.when(j == pl.num_programs(2) - 1)
    def _():
        y = x_ref[0] + mod_ref[0, 5:6, :] * acc_scr[...].T
        if final:
            y = _rms_rows(y, fn_ref[...])
        o_ref[0] = y


def _peer_expert(h2, route, u, vt, x, mod, final_gain, *, tt, ec):
    b, t, d = x.shape
    nt = t // tt
    ne = u.shape[0] // ec
    r1, e1, cut, e0 = route
    once = dict(pipeline_mode=pl.Buffered(1))
    rt = lambda: pl.BlockSpec((PEER_HEADS, PEER_KEYS, tt), lambda bi, i, j: (0, 0, bi * nt + i))
    rj = lambda: pl.BlockSpec((PEER_HEADS, ec // PEER_KEYS, tt), lambda bi, i, j: (0, j, bi * nt + i))
    in_specs = [pl.BlockSpec((tt, d), lambda bi, i, j: (bi * nt + i, 0)),
                pl.BlockSpec((tt, d), lambda bi, i, j: (jnp.minimum(bi * nt + i + 1, b * nt - 1), 0)),
                pl.BlockSpec((ec, d), lambda bi, i, j: (0, 0), **once),
                pl.BlockSpec((ec, d), lambda bi, i, j: ((j + 1) % ne, 0)),
                pl.BlockSpec((1, d, ec), lambda bi, i, j: (j, 0, 0)),
                rt(), rt(), rj(), rj(),
                pl.BlockSpec((1, tt, d), lambda bi, i, j: (bi, i, 0), **once),
                pl.BlockSpec((1, 6, d), lambda bi, i, j: (bi, 0, 0))]
    args = [h2, h2, u, u, vt, r1, e1, cut, e0, x, mod]
    if final_gain is not None:
        in_specs.append(pl.BlockSpec((1, d), lambda bi, i, j: (0, 0)))
        args.append(final_gain.reshape(1, d))
    return pl.pallas_call(
        functools.partial(_peer_expert_kernel, final=final_gain is not None, ne=ne),
        grid=(b, nt, ne),
        in_specs=in_specs,
        out_specs=pl.BlockSpec((1, tt, d), lambda bi, i, j: (bi, i, 0)),
        out_shape=jax.ShapeDtypeStruct((b, t, d), F32),
        scratch_shapes=[pltpu.VMEM((d, tt), F32), pltpu.VMEM((2, ec, tt), BF16)],
        compiler_params=_cp("arbitrary", "arbitrary", "arbitrary", vmem=EXPERT_VMEM_LIMIT),
        name="peer_expert",
    )(*args)


def _peer(x, g, mod, wq, kt, u, vt, final_gain, *, tm, tt, ec):
    out = _peer_route(x, g, mod, wq, kt, tm=tm)
    return _peer_expert(out[0], out[1:], u, vt, x, mod, final_gain, tt=tt, ec=ec)


def _lat_tile(t, want):
    while t % want:
        want //= 2
    return want


def kernel(x, c, ctx, c_ctx, w_mod, b_mod, norm1, norm2, even_w_in, even_ret_gn, even_w_out, odd_w_in, odd_sink, odd_mla_q_norm, odd_mla_kv_norm, odd_mla_w_uq, odd_mla_w_ukv, odd_w_out, peer_w_q, peer_sub_keys, peer_u, peer_v, final_norm):
    b, s_len, d = x.shape
    n_ctx = ctx.shape[1]
    depth = w_mod.shape[0]
    assert depth == 2 and even_w_in.shape[0] == 1 and odd_w_in.shape[0] == 1
    tm_l = _lat_tile(s_len, 1024)
    tm_c = n_ctx

    rows = 8 * ((b + 1 + 7) // 8)
    cond = jnp.zeros((rows, d), F32).at[:b].set(c).at[b].set(c_ctx)
    mods = _adaln(cond, w_mod, b_mod).reshape(depth, rows, 6, d)
    mod_l = [mods[l, :b] for l in range(depth)]
    mod_c = [jnp.broadcast_to(mods[l, b:b + 1], (b, 6, d)) for l in range(depth)]

    x_lat, x_ctx = x, ctx

    def peer_params(layer):
        kt = peer_sub_keys[layer].reshape(2 * PEER_HEADS, PEER_KEYS, PEER_HALF).astype(BF16)
        n_exp = peer_v.shape[1]
        vt = jnp.transpose(peer_v[layer].astype(BF16).reshape(n_exp // EXPERT_CHUNK, EXPERT_CHUNK, d), (0, 2, 1))
        return (peer_w_q[layer].astype(BF16), kt, peer_u[layer].astype(BF16), vt)

    w_in = even_w_in[0].astype(BF16)
    w_out = even_w_out[0].astype(BF16)
    p_l, f_l = _proj(x_lat, norm1[0], mod_l[0], w_in, si=0, tm=tm_l, tn=FOURIER_WIDTH, name="even_proj_lat",
                     first_block_out=True)
    p_c = _proj(x_ctx, norm1[0], mod_c[0], w_in, si=0, tm=tm_c, tn=512, name="even_proj_ctx")
    fm_l = _fourier_lat(f_l)
    fm_c = _fourier_small(p_c[..., :FOURIER_WIDTH])
    tabs_l = _rope_tables(s_len, RET_DIM, LANES)
    tabs_c = _rope_tables(n_ctx, RET_DIM, LANES, identity=True)
    zero = jnp.zeros((b, RET_HEADS, RET_DIM, RET_DIM), F32)
    gn = even_ret_gn[0]
    rf_c, st_f = _retention(p_c, tabs_c, gn, zero, None, reverse=False, gate_blk=4)
    r_c, st_b = _retention(p_c, tabs_c, gn, zero, rf_c, reverse=True, gate_blk=5)
    rf_l, _ = _retention(p_l, tabs_l, gn, st_f, None, reverse=False, gate_blk=4)
    r_l, _ = _retention(p_l, tabs_l, gn, st_b, rf_l, reverse=True, gate_blk=5)
    x_lat = _outproj(fm_l, r_l, w_out, x_lat, mod_l[0], tm=tm_l, tn=1024, name="even_out_lat")
    x_ctx = _outproj(fm_c, r_c, w_out, x_ctx, mod_c[0], tm=tm_c, tn=512, name="even_out_ctx")
    pp = peer_params(0)
    tt_l = _lat_tile(s_len, 512)
    x_lat = _peer(x_lat, norm2[0], mod_l[0], *pp, None, tm=256, tt=tt_l, ec=EXPERT_CHUNK)
    x_ctx = _peer(x_ctx, norm2[0], mod_c[0], *pp, None, tm=256, tt=n_ctx, ec=EXPERT_CHUNK)

    wi = odd_w_in[0]
    o_sq, o_sk, o_sv = 0, SWA_Q_HEADS * SWA_HEAD_DIM, (SWA_Q_HEADS + SWA_KV_HEADS) * SWA_HEAD_DIM
    o_cq = o_sv + SWA_KV_HEADS * SWA_HEAD_DIM
    o_ckv = o_cq + MLA_Q_RANK
    o_kr = o_ckv + MLA_KV_RANK
    w_in = jnp.concatenate([wi[:, o_sq:o_sk], wi[:, o_cq:o_ckv], wi[:, o_ckv:o_kr], wi[:, o_sk:o_sv], wi[:, o_sv:o_cq],
                            wi[:, o_kr:], jnp.zeros((d, LANES - MLA_ROPE), wi.dtype)], axis=1).astype(BF16)
    n_odd = w_in.shape[1]
    q_blk, cq_blk, ckv_blk = 0, (o_sk - o_sq) // MLA_Q_RANK, (o_sk - o_sq + MLA_Q_RANK) // MLA_KV_RANK
    k_blk = (o_sk - o_sq + MLA_Q_RANK + MLA_KV_RANK) // LANES
    v_blk, kr_blk = k_blk + 1, k_blk + 2
    tm_o = _lat_tile(s_len, 512)
    p_l = _proj(x_lat, norm1[1], mod_l[1], w_in, si=0, tm=tm_o, tn=n_odd, name="odd_proj_lat")
    p_c = _proj(x_ctx, norm1[1], mod_c[1], w_in, si=0, tm=tm_c, tn=n_odd, name="odd_proj_ctx")
    tabs_swa = _rope_tables(s_len, SWA_HEAD_DIM, LANES)
    swa = _swa(p_l, p_c, tabs_swa, odd_sink[0], q_blk=q_blk, k_blk=k_blk, v_blk=v_blk)
    wuq = odd_mla_w_uq[0].reshape(MLA_Q_RANK, MLA_HEADS, MLA_QK_DIM)
    wuq = jnp.pad(wuq, ((0, 0), (0, 0), (0, MLA_PAD - MLA_QK_DIM))).reshape(MLA_Q_RANK, MLA_HEADS * MLA_PAD)
    wukv = odd_mla_w_ukv[0].reshape(MLA_KV_RANK, MLA_HEADS, MLA_NOPE + MLA_V)
    wk = wukv[:, :, :MLA_NOPE].reshape(MLA_KV_RANK, -1).astype(BF16)
    wvt = wukv[:, :, MLA_NOPE:].reshape(MLA_KV_RANK, -1).T.astype(BF16)
    tabs_mla = _pad_rope_tables(_rope_tables(s_len, MLA_ROPE, MLA_ROPE), LANES)
    tabs_id = _rope_tables(n_ctx, MLA_ROPE, LANES, identity=True)
    qt = _mla_q(p_l, odd_mla_q_norm[0], wuq.T.astype(BF16), tuple(a.T for a in tabs_mla), x_blk=cq_blk, tm=tm_o)
    k_l, vt_l = _mla_kv(p_l, odd_mla_kv_norm[0], wk, wvt, tabs_mla, x_blk=ckv_blk, kr_blk=kr_blk, tm=tm_o)
    k_c, vt_c = _mla_kv(p_c, odd_mla_kv_norm[0], wk, wvt, tabs_id, x_blk=ckv_blk, kr_blk=kr_blk, tm=tm_c)
    k_all = jnp.concatenate([k_c, k_l], axis=2)
    lk = n_ctx + s_len
    tk = max(t for t in (1280, 1024, 768, 512, 256, 128) if lk % t == 0)
    vt_all = jnp.concatenate([vt_c, vt_l], axis=2).reshape(b, MLA_HEADS, MLA_V, lk // tk, tk)
    vt_all = jnp.transpose(vt_all, (0, 1, 3, 2, 4))
    mla = _flash(qt, k_all, vt_all, tq=_lat_tile(s_len, 512))
    x_lat = _outproj(swa, mla, odd_w_out[0].astype(BF16), x_lat, mod_l[1], tm=tm_l, tn=1024, name="odd_out_lat")
    pp = peer_params(1)
    return _peer(x_lat, norm2[1], mod_l[1], *pp, final_norm, tm=256, tt=tt_l, ec=EXPERT_CHUNK)
```
